```python
import jax, jax.numpy as jnp
from jax import lax
import numpy as np

D_MODEL = 1024
BATCH = 4
SEQ = 4096
DEPTH = 1

CHUNK = 64
Q_BLOCK = 2 * CHUNK
N_META = 16
HEAD_DIM = 64
MIX_WIDTH = D_MODEL
ATTN_WIDTH = MIX_WIDTH // 2
CONV_WIDTH = MIX_WIDTH - ATTN_WIDTH
N_ATTN_HEADS = ATTN_WIDTH // HEAD_DIM
N_MIX_GROUPS = MIX_WIDTH // HEAD_DIM
CONV_K = 3
SPLIT_SIZES = (ATTN_WIDTH, ATTN_WIDTH, ATTN_WIDTH, N_ATTN_HEADS, CONV_WIDTH, CONV_WIDTH, CONV_WIDTH)
IN_WIDTH = sum(SPLIT_SIZES)
N_GROUPS = 4
EXPERTS_PER_GROUP = 8
N_EXPERTS = N_GROUPS * EXPERTS_PER_GROUP
TOP_K = 2
D_EXPERT = D_MODEL // 2
MOE_BLOCK = 128
EPS = 1e-6

kernel_name = "hymba_fox_shortconv_hmoe_block"


def rmsnorm(x, g):
    xf = x.astype(jnp.float32)
    y = xf * lax.rsqrt(jnp.mean(xf * xf, axis=-1, keepdims=True) + EPS)
    return (y * g.astype(jnp.float32)).astype(x.dtype)


def forgetting_attention(q, k, v, f_logit):
    bsz, L, H, hd = q.shape
    Lp = -(-L // Q_BLOCK) * Q_BLOCK
    pad = Lp - L
    q = jnp.pad(q, ((0, 0), (0, pad), (0, 0), (0, 0)))
    k = jnp.pad(k, ((0, 0), (0, pad), (0, 0), (0, 0)))
    v = jnp.pad(v, ((0, 0), (0, pad), (0, 0), (0, 0)))
    log_f = jnp.pad(jax.nn.log_sigmoid(f_logit.astype(jnp.float32)), ((0, 0), (0, pad), (0, 0)))
    cum = jnp.cumsum(log_f, axis=1).transpose(0, 2, 1)
    pos = jnp.arange(Lp)
    nb = Lp // Q_BLOCK
    qb = q.reshape(bsz, nb, Q_BLOCK, H, hd).transpose(1, 0, 3, 2, 4)
    cqb = cum.reshape(bsz, H, nb, Q_BLOCK).transpose(2, 0, 1, 3)
    pb = pos.reshape(nb, Q_BLOCK)
    scale = HEAD_DIM ** -0.5

    def block(args):
        qi, ci, pi = args
        s = jnp.einsum('bhqd,bkhd->bhqk', qi, k, preferred_element_type=jnp.float32) * scale
        s = s + ci[..., None] - cum[:, :, None, :]
        s = jnp.where(pos[None, :] <= pi[:, None], s, -jnp.inf)
        p = jax.nn.softmax(s, axis=-1)
        return jnp.einsum('bhqk,bkhd->bqhd', p.astype(v.dtype), v)

    o = lax.map(block, (qb, cqb, pb))
    return o.transpose(1, 0, 2, 3, 4).reshape(bsz, Lp, H, hd)[:, :L]


def short_gated_conv(xc, b_gate, c_gate, w):
    u = c_gate * xc
    L = u.shape[1]
    up = jnp.pad(u, ((0, 0), (CONV_K - 1, 0), (0, 0)))
    z = sum(up[:, i:i + L] * w[i].astype(u.dtype) for i in range(CONV_K))
    return b_gate * z


def hierarchical_moe(h, w_rg, b_rg, w_re, b_re, w_gate, w_up, w_down):
    n_tok, d = h.shape
    g_prob = jax.nn.softmax((h @ w_rg).astype(jnp.float32) + b_rg.astype(jnp.float32), axis=-1)
    g_idx = jnp.argmax(g_prob, axis=-1)
    g_w = jnp.max(g_prob, axis=-1)
    e_logits = ((h @ w_re).astype(jnp.float32) + b_re.astype(jnp.float32)).reshape(n_tok, N_GROUPS, EXPERTS_PER_GROUP)
    e_logits = jnp.take_along_axis(e_logits, g_idx[:, None, None], axis=1)[:, 0]
    top_p, top_i = lax.top_k(jax.nn.softmax(e_logits, axis=-1), TOP_K)
    top_p = top_p / jnp.sum(top_p, axis=-1, keepdims=True)
    weights = g_w[:, None] * top_p
    expert = g_idx[:, None] * EXPERTS_PER_GROUP + top_i

    n_asg = n_tok * TOP_K
    flat_e = expert.reshape(-1)
    flat_tok = jnp.repeat(jnp.arange(n_tok), TOP_K)
    flat_w = weights.reshape(-1)
    order = jnp.argsort(flat_e)
    se, stok, sw = flat_e[order], flat_tok[order], flat_w[order]
    counts = jnp.bincount(flat_e, length=N_EXPERTS)
    starts = jnp.cumsum(counts) - counts
    padded = (counts + MOE_BLOCK - 1) // MOE_BLOCK * MOE_BLOCK
    pad_end = jnp.cumsum(padded)
    pad_start = pad_end - padded
    dest = pad_start[se] + (jnp.arange(n_asg) - starts[se])
    n_blocks = -(-n_asg // MOE_BLOCK) + N_EXPERTS
    rows = n_blocks * MOE_BLOCK
    xbuf = jnp.zeros((rows, d), h.dtype).at[dest].set(h[stok])
    blk_e = jnp.minimum(jnp.searchsorted(pad_end, jnp.arange(n_blocks) * MOE_BLOCK, side='right'), N_EXPERTS - 1)

    def expert_block(args):
        xb, e = args
        return (jax.nn.silu(xb @ w_gate[e]) * (xb @ w_up[e])) @ w_down[e]

    ybuf = lax.map(expert_block, (xbuf.reshape(n_blocks, MOE_BLOCK, d), blk_e)).reshape(rows, d)
    y = ybuf[dest] * sw[:, None].astype(h.dtype)
    return jax.ops.segment_sum(y, stok, num_segments=n_tok)


def setup_inputs(seed: int = 0) -> dict:
    key = jax.random.key(seed)
    ks = jax.random.split(key, 20)
    f32 = jnp.float32
    nrm = lambda k, shape, scale: jax.random.normal(k, shape, f32) * scale
    return {
        'x': nrm(ks[0], (BATCH, SEQ, D_MODEL), 1.0),
        'meta_tokens': nrm(ks[1], (N_META, D_MODEL), 1.0),
        'attn_norm': 1.0 + nrm(ks[2], (DEPTH, D_MODEL), 0.02),
        'w_in': nrm(ks[3], (DEPTH, D_MODEL, IN_WIDTH), D_MODEL ** -0.5),
        'b_forget': jax.random.uniform(ks[4], (DEPTH, N_ATTN_HEADS), f32, 1.0, 4.0),
        'conv_w': nrm(ks[5], (DEPTH, CONV_K, CONV_WIDTH), CONV_K ** -0.5),
        'mix_norm': 1.0 + nrm(ks[6], (DEPTH, MIX_WIDTH), 0.02),
        'w_out': nrm(ks[7], (DEPTH, MIX_WIDTH, D_MODEL), MIX_WIDTH ** -0.5),
        'ffn_norm': 1.0 + nrm(ks[8], (DEPTH, D_MODEL), 0.02),
        'w_router_group': nrm(ks[9], (DEPTH, D_MODEL, N_GROUPS), D_MODEL ** -0.5),
        'b_router_group': nrm(ks[10], (DEPTH, N_GROUPS), 0.01),
        'w_router_expert': nrm(ks[11], (DEPTH, D_MODEL, N_EXPERTS), D_MODEL ** -0.5),
        'b_router_expert': nrm(ks[12], (DEPTH, N_EXPERTS), 0.01),
        'w_gate': nrm(ks[13], (DEPTH, N_EXPERTS, D_MODEL, D_EXPERT), D_MODEL ** -0.5),
        'w_up': nrm(ks[14], (DEPTH, N_EXPERTS, D_MODEL, D_EXPERT), D_MODEL ** -0.5),
        'w_down': nrm(ks[15], (DEPTH, N_EXPERTS, D_EXPERT, D_MODEL), D_EXPERT ** -0.5),
        'final_norm': 1.0 + nrm(ks[16], (D_MODEL,), 0.02),
    }


def reference(x, meta_tokens, attn_norm, w_in, b_forget, conv_w, mix_norm, w_out, ffn_norm,
              w_router_group, b_router_group, w_router_expert, b_router_expert,
              w_gate, w_up, w_down, final_norm):
    bsz = x.shape[0]
    meta = jnp.broadcast_to(meta_tokens[None].astype(x.dtype), (bsz, N_META, D_MODEL))
    h = jnp.concatenate([meta, x], axis=1)
    L = h.shape[1]
    split_at = [int(s) for s in np.cumsum(SPLIT_SIZES)[:-1]]
    for l in range(DEPTH):
        z = rmsnorm(h, attn_norm[l])
        q, k, v, f_logit, xc, b_gate, c_gate = jnp.split(z @ w_in[l], split_at, axis=-1)
        f_logit = f_logit.astype(jnp.float32) + b_forget[l].astype(jnp.float32)
        heads = (bsz, L, N_ATTN_HEADS, HEAD_DIM)
        attn = forgetting_attention(q.reshape(heads), k.reshape(heads), v.reshape(heads), f_logit)
        attn = attn.reshape(bsz, L, ATTN_WIDTH)
        conv = short_gated_conv(xc, b_gate, c_gate, conv_w[l])
        mixed = jnp.concatenate([attn, conv.astype(attn.dtype)], axis=-1)
        mixed = rmsnorm(mixed.reshape(bsz, L, N_MIX_GROUPS, HEAD_DIM),
                        mix_norm[l].reshape(N_MIX_GROUPS, HEAD_DIM)).reshape(bsz, L, MIX_WIDTH)
        h = h + mixed @ w_out[l]
        z = rmsnorm(h, ffn_norm[l]).reshape(bsz * L, D_MODEL)
        ffn = hierarchical_moe(z, w_router_group[l], b_router_group[l], w_router_expert[l],
                               b_router_expert[l], w_gate[l], w_up[l], w_down[l])
        h = h + ffn.reshape(bsz, L, D_MODEL)
    out = rmsnorm(h, final_norm)
    return out[:, N_META:]
```

```python
import functools

import numpy as np
import jax
import jax.numpy as jnp
from jax import lax
from jax.experimental import pallas as pl
from jax.experimental.pallas import tpu as pltpu

F32 = jnp.float32
BF16 = jnp.bfloat16

D_MODEL = 1024
HEAD_DIM = 64
N_HEADS = 8
ATTN_W = N_HEADS * HEAD_DIM
CONV_W = 512
N_META = 16
N_GROUPS = 4
EXPERTS_PER_GROUP = 8
N_EXPERTS = N_GROUPS * EXPERTS_PER_GROUP
D_EXPERT = 512
EPS = 1e-6
NEG_BIG = -1e30

LANES = 128
HEAD_SLOT = 2 * HEAD_DIM
AUG_W = N_HEADS * HEAD_SLOT
N_PARTS = 3
PART_ROWS = 16
ONES_LANE = 8
ROUTER_LANE0 = N_GROUPS

PROJ_T = 512
ATT_T = 256
META_PAD = 128
MIX_T = 256
DISPATCH_T = 256
EXPERT_BM = 256
COMBINE_T = 256
VMEM_LIMIT = 48 * 1024 * 1024


def _split3(x):
    hi = x.astype(BF16)
    r1 = x - hi.astype(F32)
    mid = r1.astype(BF16)
    r2 = r1 - mid.astype(F32)
    return hi, mid, r2.astype(BF16)


def _log_sigmoid(x):
    return jnp.minimum(x, 0.0) - jnp.log1p(jnp.exp(-jnp.abs(x)))


def _head_offset(h):
    return HEAD_DIM if h % 2 == 0 else 0


def _selection_matrices():
    selk = np.zeros((N_PARTS * LANES, AUG_W), np.float32)
    selq = np.zeros((AUG_W, LANES), np.float32)
    for h in range(N_HEADS):
        base = h * HEAD_SLOT + _head_offset(h)
        for j in range(N_PARTS):
            selk[ONES_LANE, base + j] = 1.0
            selk[j * LANES + h, base + N_PARTS + j] = 1.0
            selq[base + j, j * PART_ROWS + h] = 1.0
            selq[base + N_PARTS + j, ONES_LANE] = 1.0
    return jnp.asarray(selk, BF16), jnp.asarray(selq, BF16)


def _proj_kernel(x_ref, g_ref, wrow_ref, wt_ref, bfr_ref, bfc_ref, convw_ref, selk_ref, selq_ref,
                 cin_row_ref, cin_col_ref, uin_ref,
                 kaug_ref, qaug_ref, vt_ref, conv_ref, *rest, t_rows, n_valid, emit_carry, tk):
    if emit_carry:
        cumrow_out, cumt_out, u_out, c_row, c_col, u_prev = rest
    else:
        c_row, c_col, u_prev = rest

    @pl.when(pl.program_id(1) == 0)
    def _():
        c_row[...] = cin_row_ref[...]
        c_col[...] = cin_col_ref[...]
        u_prev[...] = uin_ref[...]

    T = t_rows
    x = x_ref[...]
    ms = jnp.mean(x * x, axis=-1, keepdims=True)
    z = x * lax.rsqrt(ms + EPS) * g_ref[...]
    zb = z.astype(BF16)
    r = jnp.dot(zb, wrow_ref[...], preferred_element_type=F32)
    tt = lax.dot_general(wt_ref[...], zb, (((1,), (1,)), ((), ())),
                         preferred_element_type=F32)
    k = r[:, 0:ATTN_W]
    xc = r[:, ATTN_W:ATTN_W + CONV_W]
    bg = r[:, ATTN_W + CONV_W:ATTN_W + 2 * CONV_W]
    cg = r[:, ATTN_W + 2 * CONV_W:ATTN_W + 3 * CONV_W]
    fr = r[:, ATTN_W + 3 * CONV_W:ATTN_W + 3 * CONV_W + LANES]
    qt = tt[0:ATTN_W]
    vt = tt[ATTN_W:2 * ATTN_W]
    ft = tt[2 * ATTN_W:2 * ATTN_W + PART_ROWS]

    lane_r = lax.broadcasted_iota(jnp.int32, (T, LANES), 1)
    row_r = lax.broadcasted_iota(jnp.int32, (T, LANES), 0)
    lfr = jnp.where(lane_r < N_HEADS, _log_sigmoid(fr + bfr_ref[...]), 0.0)
    lfr = lfr + jnp.where(row_r == 0, c_row[0:1, :], 0.0)
    tri_r = lax.broadcasted_iota(jnp.int32, (T, T), 0)
    tri_c = lax.broadcasted_iota(jnp.int32, (T, T), 1)
    tri_l = (tri_c <= tri_r).astype(BF16)
    tri_u = (tri_r <= tri_c).astype(BF16)
    c3 = jnp.dot(tri_l, jnp.concatenate(_split3(lfr), axis=1), preferred_element_type=F32)
    cum_row = c3[:, 0:LANES] + c3[:, LANES:2 * LANES] + c3[:, 2 * LANES:3 * LANES]

    kb = -cum_row
    if n_valid < T:
        kb = kb + jnp.where(row_r >= n_valid, NEG_BIG, 0.0)
    kb = jnp.where(lane_r == ONES_LANE, 1.0, kb)
    e = jnp.dot(jnp.concatenate(_split3(kb), axis=1), selk_ref[...],
                preferred_element_type=F32)
    for p in range(N_HEADS // 2):
        kp = k[:, p * LANES:(p + 1) * LANES].astype(BF16)
        e0 = e[:, 2 * p * LANES:(2 * p + 1) * LANES].astype(BF16)
        e1 = e[:, (2 * p + 1) * LANES:(2 * p + 2) * LANES].astype(BF16)
        kaug_ref[:, 2 * p * LANES:(2 * p + 1) * LANES] = jnp.where(lane_r < HEAD_DIM, kp, e0)
        kaug_ref[:, (2 * p + 1) * LANES:(2 * p + 2) * LANES] = jnp.where(lane_r >= HEAD_DIM, kp, e1)

    lane_c = lax.broadcasted_iota(jnp.int32, (PART_ROWS, T), 1)
    row_c = lax.broadcasted_iota(jnp.int32, (PART_ROWS, T), 0)
    lfc = jnp.where(row_c < N_HEADS, _log_sigmoid(ft + bfc_ref[:, 0:1]), 0.0)
    lfc = lfc + jnp.where(lane_c == 0, c_col[:, 0:1], 0.0)
    c3c = jnp.dot(jnp.concatenate(_split3(lfc), axis=0), tri_u, preferred_element_type=F32)
    cum_t = c3c[0:PART_ROWS] + c3c[PART_ROWS:2 * PART_ROWS] + c3c[2 * PART_ROWS:3 * PART_ROWS]
    cq = jnp.where(row_c == ONES_LANE, 1.0, cum_t)
    pm_t = jnp.concatenate(_split3(cq) + (jnp.zeros((LANES - N_PARTS * PART_ROWS, T), BF16),), axis=0)
    et = jnp.dot(selq_ref[...], pm_t, preferred_element_type=F32)
    for h in range(N_HEADS):
        qh = qt[h * HEAD_DIM:(h + 1) * HEAD_DIM].astype(BF16)
        lo = h * HEAD_SLOT
        if h % 2 == 0:
            qaug_ref[lo:lo + HEAD_DIM, :] = qh
            qaug_ref[lo + HEAD_DIM:lo + HEAD_SLOT, :] = et[lo + HEAD_DIM:lo + HEAD_SLOT].astype(BF16)
        else:
            qaug_ref[lo:lo + HEAD_DIM, :] = et[lo:lo + HEAD_DIM].astype(BF16)
            qaug_ref[lo + HEAD_DIM:lo + HEAD_SLOT, :] = qh

    vtb = vt.astype(BF16)
    for s in range(T // tk):
        vt_ref[s] = vtb[:, s * tk:(s + 1) * tk]

    u = cg * xc
    row_u = lax.broadcasted_iota(jnp.int32, (T, CONV_W), 0)
    p1 = u_prev[7:8, :]
    p2 = u_prev[6:7, :]
    u1 = jnp.where(row_u == 0, p1, pltpu.roll(u, 1, axis=0))
    u2 = jnp.where(row_u == 0, p2, jnp.where(row_u == 1, p1, pltpu.roll(u, 2, axis=0)))
    zc = convw_ref[0:1, :] * u2 + convw_ref[1:2, :] * u1 + convw_ref[2:3, :] * u
    conv_ref[...] = bg * zc

    if emit_carry:
        cumrow_out[...] = cum_row
        cumt_out[...] = cum_t
        u_out[...] = u
    c_row[0:1, :] = cum_row[T - 1:T, :]
    c_col[...] = jnp.broadcast_to(cum_t[:, T - 1:T], (PART_ROWS, LANES))
    u_prev[...] = u[T - 8:T, :]


def _proj_call(xs, g, wrow, wt, bfr, bfc, convw, selk, selq, cin_row, cin_col, uin,
               *, t_rows, n_valid, emit_carry):
    bn, ln, _ = xs.shape
    T = t_rows
    tk = min(ATT_T, T)
    nt = ln // T
    const2 = lambda b, t: (0, 0)
    in_specs = [
        pl.BlockSpec((None, T, D_MODEL), lambda b, t: (b, t, 0)),
        pl.BlockSpec(g.shape, const2),
        pl.BlockSpec(wrow.shape, const2),
        pl.BlockSpec(wt.shape, const2),
        pl.BlockSpec(bfr.shape, const2),
        pl.BlockSpec(bfc.shape, const2),
        pl.BlockSpec(convw.shape, const2),
        pl.BlockSpec(selk.shape, const2),
        pl.BlockSpec(selq.shape, const2),
        pl.BlockSpec(cin_row.shape, const2),
        pl.BlockSpec(cin_col.shape, const2),
        pl.BlockSpec(uin.shape, const2),
    ]
    out_shape = [
        jax.ShapeDtypeStruct((bn, ln, AUG_W), BF16),
        jax.ShapeDtypeStruct((bn, AUG_W, ln), BF16),
        jax.ShapeDtypeStruct((bn, ln // tk, ATTN_W, tk), BF16),
        jax.ShapeDtypeStruct((bn, ln, CONV_W), F32),
    ]
    out_specs = [
        pl.BlockSpec((None, T, AUG_W), lambda b, t: (b, t, 0)),
        pl.BlockSpec((None, AUG_W, T), lambda b, t: (b, 0, t)),
        pl.BlockSpec((None, T // tk, ATTN_W, tk), lambda b, t: (b, t, 0, 0)),
        pl.BlockSpec((None, T, CONV_W), lambda b, t: (b, t, 0)),
    ]
    if emit_carry:
        out_shape += [
            jax.ShapeDtypeStruct((bn, ln, LANES), F32),
            jax.ShapeDtypeStruct((bn, PART_ROWS, ln), F32),
            jax.ShapeDtypeStruct((bn, ln, CONV_W), F32),
        ]
        out_specs += [
            pl.BlockSpec((None, T, LANES), lambda b, t: (b, t, 0)),
            pl.BlockSpec((None, PART_ROWS, T), lambda b, t: (b, 0, t)),
            pl.BlockSpec((None, T, CONV_W), lambda b, t: (b, t, 0)),
        ]
    kern = functools.partial(_proj_kernel, t_rows=T, n_valid=n_valid, emit_carry=emit_carry, tk=tk)
    return pl.pallas_call(
        kern,
        grid=(bn, nt),
        in_specs=in_specs,
        out_specs=out_specs,
        out_shape=out_shape,
        scratch_shapes=[pltpu.VMEM((8, LANES), F32), pltpu.VMEM((PART_ROWS, LANES), F32),
                        pltpu.VMEM((8, CONV_W), F32)],
        compiler_params=pltpu.CompilerParams(dimension_semantics=("arbitrary", "arbitrary"),
                                             vmem_limit_bytes=VMEM_LIMIT),
        name="proj_meta" if emit_carry else "proj",
    )(xs, g, wrow, wt, bfr, bfc, convw, selk, selq, cin_row, cin_col, uin)


def _attn_kernel(q_ref, k_ref, v_ref, km_ref, vm_ref, g_ref, o_ref):
    i = pl.program_id(1)
    T = ATT_T
    causal = (lax.broadcasted_iota(jnp.int32, (T, T), 0) <= lax.broadcasted_iota(jnp.int32, (T, T), 1))

    for p in range(N_HEADS // 2):
        halves = []
        for h in (2 * p, 2 * p + 1):
            qa = q_ref[h * HEAD_SLOT:(h + 1) * HEAD_SLOT, :]
            hs = slice(h * HEAD_DIM, (h + 1) * HEAD_DIM)
            ks = slice(h * HEAD_SLOT, (h + 1) * HEAD_SLOT)

            s = jnp.dot(km_ref[:, ks], qa, preferred_element_type=F32)
            m = jnp.max(s, axis=0, keepdims=True)
            pm = jnp.exp(s - m)
            l = jnp.sum(pm, axis=0, keepdims=True)
            acc = jnp.dot(vm_ref[hs, :], pm.astype(BF16), preferred_element_type=F32)

            def step(j, carry, masked):
                m, l, acc = carry
                kblk = k_ref[pl.ds(pl.multiple_of(j * T, T), T), ks]
                s = jnp.dot(kblk, qa, preferred_element_type=F32)
                if masked:
                    s = jnp.where(causal, s, NEG_BIG)
                m_new = jnp.maximum(m, jnp.max(s, axis=0, keepdims=True))
                alpha = jnp.exp(m - m_new)
                pm = jnp.exp(s - m_new)
                l = alpha * l + jnp.sum(pm, axis=0, keepdims=True)
                acc = alpha * acc + jnp.dot(v_ref[j, hs, :], pm.astype(BF16),
                                            preferred_element_type=F32)
                return m_new, l, acc

            carry = lax.fori_loop(0, i, functools.partial(step, masked=False), (m, l, acc))
            m, l, acc = step(i, carry, True)
            o = acc * (1.0 / l)
            o = o * lax.rsqrt(jnp.mean(o * o, axis=0, keepdims=True) + EPS)
            halves.append(o)
        pair = jnp.concatenate(halves, axis=0)
        o_ref[:, p * LANES:(p + 1) * LANES] = (pair.T * g_ref[:, p * LANES:(p + 1) * LANES]).astype(o_ref.dtype)


def _attn_call(qaug, kaug, vt, kmeta, vmeta, gain):
    bn, _, ln = qaug.shape
    T = ATT_T
    nq = ln // T
    return pl.pallas_call(
        _attn_kernel,
        grid=(bn, nq),
        in_specs=[
            pl.BlockSpec((None, AUG_W, T), lambda b, i: (b, 0, i)),
            pl.BlockSpec((None, ln, AUG_W), lambda b, i: (b, 0, 0)),
            pl.BlockSpec((None, ln // T, ATTN_W, T), lambda b, i: (b, 0, 0, 0)),
            pl.BlockSpec(kmeta.shape, lambda b, i: (0, 0)),
            pl.BlockSpec(vmeta.shape, lambda b, i: (0, 0)),
            pl.BlockSpec(gain.shape, lambda b, i: (0, 0)),
        ],
        out_specs=pl.BlockSpec((None, T, ATTN_W), lambda b, i: (b, i, 0)),
        out_shape=jax.ShapeDtypeStruct((bn, ln, ATTN_W), BF16),
        compiler_params=pltpu.CompilerParams(dimension_semantics=("arbitrary", "arbitrary"),
                                             vmem_limit_bytes=VMEM_LIMIT),
        name="attention",
    )(qaug, kaug, vt, kmeta, vmeta, gain)


def _mix_kernel(attn_ref, conv_ref, x_ref, gc_ref, gsum_ref, woa_ref, woc_ref, fg_ref,
                wrh_ref, wrl_ref, br_ref, h_ref, z_ref, rinfo_ref, counts_ref, cnt):
    T = MIX_T

    @pl.when(pl.program_id(0) == 0)
    def _():
        cnt[...] = jnp.zeros_like(cnt)

    c = conv_ref[...]
    c2 = c * c
    c2h = c2.astype(BF16)
    c2l = (c2 - c2h.astype(F32)).astype(BF16)
    ss = (jnp.dot(c2h, gsum_ref[...], preferred_element_type=F32)
          + jnp.dot(c2l, gsum_ref[...], preferred_element_type=F32))
    cn = (c * lax.rsqrt(ss * (1.0 / HEAD_DIM) + EPS) * gc_ref[...]).astype(BF16)
    hadd = (jnp.dot(attn_ref[...], woa_ref[...], preferred_element_type=F32)
            + jnp.dot(cn, woc_ref[...], preferred_element_type=F32))
    h = x_ref[...] + hadd
    h_ref[...] = h
    z = h * lax.rsqrt(jnp.mean(h * h, axis=-1, keepdims=True) + EPS) * fg_ref[...]
    z_ref[...] = z
    zh = z.astype(BF16)
    zl = (z - zh.astype(F32)).astype(BF16)
    logits = (jnp.dot(zh, wrh_ref[...], preferred_element_type=F32)
              + jnp.dot(zl, wrh_ref[...], preferred_element_type=F32)
              + jnp.dot(zh, wrl_ref[...], preferred_element_type=F32)) + br_ref[...]

    lane = lax.broadcasted_iota(jnp.int32, (T, LANES), 1).astype(F32)
    big = float(LANES)
    gl = jnp.where(lane < N_GROUPS, logits, -jnp.inf)
    gmax = jnp.max(gl, axis=-1, keepdims=True)
    gsum = jnp.sum(jnp.exp(gl - gmax), axis=-1, keepdims=True)
    g_w = 1.0 / gsum
    g_idx = jnp.min(jnp.where(gl == gmax, lane, big), axis=-1, keepdims=True)
    e_lo = ROUTER_LANE0 + EXPERTS_PER_GROUP * g_idx
    emask = (lane >= e_lo) & (lane < e_lo + EXPERTS_PER_GROUP)
    el = jnp.where(emask, logits, -jnp.inf)
    emax = jnp.max(el, axis=-1, keepdims=True)
    eexp = jnp.exp(el - emax)
    probs = eexp / jnp.sum(eexp, axis=-1, keepdims=True)
    probs = jnp.where(emask, probs, -1.0)
    p1 = jnp.max(probs, axis=-1, keepdims=True)
    i1 = jnp.min(jnp.where(probs == p1, lane, big), axis=-1, keepdims=True)
    probs2 = jnp.where(lane == i1, -1.0, probs)
    p2 = jnp.max(probs2, axis=-1, keepdims=True)
    i2 = jnp.min(jnp.where(probs2 == p2, lane, big), axis=-1, keepdims=True)
    psum = p1 + p2
    w1 = g_w * (p1 / psum)
    w2 = g_w * (p2 / psum)

    hit = ((lane == i1) | (lane == i2)).astype(F32)
    tri_r = lax.broadcasted_iota(jnp.int32, (T, T), 0)
    tri_c = lax.broadcasted_iota(jnp.int32, (T, T), 1)
    strict_l = (tri_c < tri_r).astype(BF16)
    before = jnp.dot(strict_l, hit.astype(BF16), preferred_element_type=F32) + cnt[0:1, :]
    r1 = jnp.sum(jnp.where(lane == i1, before, 0.0), axis=-1, keepdims=True)
    r2 = jnp.sum(jnp.where(lane == i2, before, 0.0), axis=-1, keepdims=True)
    cnt[0:1, :] = cnt[0:1, :] + jnp.sum(hit, axis=0, keepdims=True)
    counts_ref[...] = cnt[...]

    e1 = i1 - ROUTER_LANE0
    e2 = i2 - ROUTER_LANE0
    rinfo = jnp.where(lane == 0, e1, jnp.where(lane == 1, e2, jnp.where(lane == 2, w1, jnp.where(
        lane == 3, w2, jnp.where(lane == 4, r1, jnp.where(lane == 5, r2, 0.0))))))
    rinfo_ref[...] = rinfo


def _mix_call(attn, conv, x, gc, gsum, woa, woc, fg, wrh, wrl, br):
    n = x.shape[0]
    T = MIX_T
    const = lambda i: (0, 0)
    row = lambda w: pl.BlockSpec((T, w), lambda i: (i, 0))
    return pl.pallas_call(
        _mix_kernel,
        grid=(n // T,),
        in_specs=[row(ATTN_W), row(CONV_W), row(D_MODEL)] + [
            pl.BlockSpec(a.shape, const) for a in (gc, gsum, woa, woc, fg, wrh, wrl, br)],
        out_specs=[row(D_MODEL), row(D_MODEL), row(LANES), pl.BlockSpec((8, LANES), const)],
        out_shape=[jax.ShapeDtypeStruct((n, D_MODEL), F32), jax.ShapeDtypeStruct((n, D_MODEL), F32),
                   jax.ShapeDtypeStruct((n, LANES), F32), jax.ShapeDtypeStruct((8, LANES), F32)],
        scratch_shapes=[pltpu.VMEM((8, LANES), F32)],
        compiler_params=pltpu.CompilerParams(dimension_semantics=("arbitrary",),
                                             vmem_limit_bytes=VMEM_LIMIT),
        name="mix",
    )(attn, conv, x, gc, gsum, woa, woc, fg, wrh, wrl, br)


def _dispatch_kernel(dest_ref, z_ref, xbuf_in, xbuf_ref, sem):
    del xbuf_in
    T = DISPATCH_T

    def row_copy(r, slot):
        d = dest_ref[0, 2 * r + slot]
        return pltpu.make_async_copy(z_ref.at[pl.ds(r, 1)], xbuf_ref.at[pl.ds(d, 1)], sem)

    def start(r, c):
        row_copy(r, 0).start()
        row_copy(r, 1).start()
        return c

    def wait(r, c):
        row_copy(r, 0).wait()
        row_copy(r, 1).wait()
        return c

    lax.fori_loop(0, T, start, 0)
    lax.fori_loop(0, T, wait, 0)


def _dispatch_call(dest, z, xbuf0):
    n = z.shape[0]
    T = DISPATCH_T
    return pl.pallas_call(
        _dispatch_kernel,
        grid=(n // T,),
        in_specs=[
            pl.BlockSpec((None, 1, 2 * T), lambda i: (i, 0, 0), memory_space=pltpu.SMEM),
            pl.BlockSpec((T, D_MODEL), lambda i: (i, 0)),
            pl.BlockSpec(memory_space=pl.ANY),
        ],
        out_specs=pl.BlockSpec(memory_space=pl.ANY),
        out_shape=jax.ShapeDtypeStruct(xbuf0.shape, xbuf0.dtype),
        scratch_shapes=[pltpu.SemaphoreType.DMA(())],
        input_output_aliases={2: 0},
        compiler_params=pltpu.CompilerParams(dimension_semantics=("arbitrary",)),
        name="dispatch",
    )(dest, z, xbuf0)


def _expert_kernel(blk_e_ref, blk_valid_ref, x_ref, wg_ref, wu_ref, wd_ref, y_ref):
    del blk_e_ref
    blk = pl.program_id(0)

    @pl.when(blk_valid_ref[blk] != 0)
    def _():
        xb = x_ref[...].astype(BF16)
        g = jnp.dot(xb, wg_ref[...].astype(BF16), preferred_element_type=F32)
        u = jnp.dot(xb, wu_ref[...].astype(BF16), preferred_element_type=F32)
        a = (g * jax.nn.sigmoid(g)) * u
        y_ref[...] = jnp.dot(a.astype(BF16), wd_ref[...].astype(BF16), preferred_element_type=F32)

    @pl.when(blk_valid_ref[blk] == 0)
    def _():
        y_ref[...] = jnp.zeros_like(y_ref)


def _expert_call(blk_e, blk_valid, xbuf, w_gate, w_up, w_down):
    rows = xbuf.shape[0]
    bm = EXPERT_BM
    grid_spec = pltpu.PrefetchScalarGridSpec(
        num_scalar_prefetch=2,
        grid=(rows // bm,),
        in_specs=[
            pl.BlockSpec((bm, D_MODEL), lambda i, be, bv: (i, 0)),
            pl.BlockSpec((None, D_MODEL, D_EXPERT), lambda i, be, bv: (be[i], 0, 0)),
            pl.BlockSpec((None, D_MODEL, D_EXPERT), lambda i, be, bv: (be[i], 0, 0)),
            pl.BlockSpec((None, D_EXPERT, D_MODEL), lambda i, be, bv: (be[i], 0, 0)),
        ],
        out_specs=pl.BlockSpec((bm, D_MODEL), lambda i, be, bv: (i, 0)),
    )
    return pl.pallas_call(
        _expert_kernel,
        grid_spec=grid_spec,
        out_shape=jax.ShapeDtypeStruct((rows, D_MODEL), F32),
        compiler_params=pltpu.CompilerParams(dimension_semantics=("arbitrary",),
                                             vmem_limit_bytes=VMEM_LIMIT),
        name="experts",
    )(blk_e, blk_valid, xbuf, w_gate, w_up, w_down)


def _combine_kernel(dest_ref, h_ref, rinfo_ref, fn_ref, ybuf_ref, o_ref, ybuf_v, sem):
    T = COMBINE_T

    def row_copy(r, slot):
        d = dest_ref[0, 2 * r + slot]
        return pltpu.make_async_copy(ybuf_ref.at[pl.ds(d, 1)], ybuf_v.at[slot, pl.ds(r, 1)], sem)

    def start(r, c):
        row_copy(r, 0).start()
        row_copy(r, 1).start()
        return c

    def wait(r, c):
        row_copy(r, 0).wait()
        row_copy(r, 1).wait()
        return c

    lax.fori_loop(0, T, start, 0)
    lax.fori_loop(0, T, wait, 0)
    rinfo = rinfo_ref[...]
    w1 = rinfo[:, 2:3]
    w2 = rinfo[:, 3:4]
    h = h_ref[...] + (ybuf_v[0] * w1 + ybuf_v[1] * w2)
    o_ref[...] = h * lax.rsqrt(jnp.mean(h * h, axis=-1, keepdims=True) + EPS) * fn_ref[...]


def _combine_call(dest, h, rinfo, fn, ybuf):
    n = h.shape[0]
    T = COMBINE_T
    return pl.pallas_call(
        _combine_kernel,
        grid=(n // T,),
        in_specs=[
            pl.BlockSpec((None, 1, 2 * T), lambda i: (i, 0, 0), memory_space=pltpu.SMEM),
            pl.BlockSpec((T, D_MODEL), lambda i: (i, 0)),
            pl.BlockSpec((T, LANES), lambda i: (i, 0)),
            pl.BlockSpec(fn.shape, lambda i: (0, 0)),
            pl.BlockSpec(memory_space=pl.ANY),
        ],
        out_specs=pl.BlockSpec((T, D_MODEL), lambda i: (i, 0)),
        out_shape=jax.ShapeDtypeStruct((n, D_MODEL), F32),
        scratch_shapes=[pltpu.VMEM((2, T, D_MODEL), F32), pltpu.SemaphoreType.DMA(())],
        compiler_params=pltpu.CompilerParams(dimension_semantics=("arbitrary",),
                                             vmem_limit_bytes=VMEM_LIMIT),
        name="combine",
    )(dest, h, rinfo, fn, ybuf)


def kernel(x, meta_tokens, attn_norm, w_in, b_forget, conv_w, mix_norm, w_out, ffn_norm,
           w_router_group, b_router_group, w_router_expert, b_router_expert,
           w_gate, w_up, w_down, final_norm):
    bsz, seq, d = x.shape
    assert d == D_MODEL and seq % PROJ_T == 0 and meta_tokens.shape[0] == N_META
    assert attn_norm.shape[0] == 1, "single-layer block"
    n_tok = bsz * seq

    wi = w_in[0]
    o = 0
    w_q = wi[:, o:o + ATTN_W]; o += ATTN_W
    w_k = wi[:, o:o + ATTN_W]; o += ATTN_W
    w_v = wi[:, o:o + ATTN_W]; o += ATTN_W
    w_f = wi[:, o:o + N_HEADS]; o += N_HEADS
    w_xc = wi[:, o:o + CONV_W]; o += CONV_W
    w_bg = wi[:, o:o + CONV_W]; o += CONV_W
    w_cg = wi[:, o:o + CONV_W]
    wrow = jnp.concatenate([w_k, w_xc, w_bg, w_cg, w_f, jnp.zeros((D_MODEL, LANES - N_HEADS), F32)],
                           axis=1).astype(BF16)
    wt = jnp.concatenate([w_q.T * (HEAD_DIM ** -0.5), w_v.T, w_f.T,
                          jnp.zeros((PART_ROWS - N_HEADS, D_MODEL), F32)], axis=0).astype(BF16)
    g_attn = attn_norm[0].reshape(1, D_MODEL)
    bfr = jnp.zeros((1, LANES), F32).at[0, :N_HEADS].set(b_forget[0])
    bfc = jnp.zeros((PART_ROWS, LANES), F32).at[:N_HEADS, :].set(b_forget[0][:, None])
    convw = jnp.zeros((8, CONV_W), F32).at[:3].set(conv_w[0])
    selk, selq = _selection_matrices()

    meta_pad = jnp.zeros((1, META_PAD, D_MODEL), F32).at[0, :N_META].set(meta_tokens)
    zrow = jnp.zeros((8, LANES), F32)
    zcol = jnp.zeros((PART_ROWS, LANES), F32)
    zu = jnp.zeros((8, CONV_W), F32)
    kmeta, _, vmeta, _, cumrow_m, cumt_m, u_m = _proj_call(
        meta_pad, g_attn, wrow, wt, bfr, bfc, convw, selk, selq, zrow, zcol, zu,
        t_rows=META_PAD, n_valid=N_META, emit_carry=True)
    cin_row = jnp.zeros((8, LANES), F32).at[0].set(cumrow_m[0, N_META - 1])
    cin_col = jnp.broadcast_to(cumt_m[0, :, N_META - 1:N_META], (PART_ROWS, LANES))
    uin = u_m[0, N_META - 8:N_META]

    kaug, qaug, vt, conv = _proj_call(
        x, g_attn, wrow, wt, bfr, bfc, convw, selk, selq, cin_row, cin_col, uin,
        t_rows=PROJ_T, n_valid=PROJ_T, emit_carry=False)

    mixg = mix_norm[0]
    attn = _attn_call(qaug, kaug, vt, kmeta[0], vmeta[0, 0], mixg[:ATTN_W].reshape(1, ATTN_W))

    gidx = np.arange(CONV_W) // HEAD_DIM
    gsum = jnp.asarray(gidx[:, None] == gidx[None, :], BF16)
    wo = w_out[0].astype(BF16)
    wr = jnp.zeros((D_MODEL, LANES), F32)
    wr = wr.at[:, :N_GROUPS].set(w_router_group[0]).at[:, ROUTER_LANE0:ROUTER_LANE0 + N_EXPERTS].set(
        w_router_expert[0])
    wrh = wr.astype(BF16)
    wrl = (wr - wrh.astype(F32)).astype(BF16)
    br = jnp.zeros((1, LANES), F32)
    br = br.at[0, :N_GROUPS].set(b_router_group[0]).at[0, ROUTER_LANE0:ROUTER_LANE0 + N_EXPERTS].set(
        b_router_expert[0])
    h, z, rinfo, counts = _mix_call(
        attn.reshape(n_tok, ATTN_W), conv.reshape(n_tok, CONV_W), x.reshape(n_tok, D_MODEL),
        mixg[ATTN_W:].reshape(1, CONV_W), gsum, wo[:ATTN_W], wo[ATTN_W:], ffn_norm[0].reshape(1, D_MODEL),
        wrh, wrl, br)

    cnt = counts[0, ROUTER_LANE0:ROUTER_LANE0 + N_EXPERTS].astype(jnp.int32)
    padded = (cnt + EXPERT_BM - 1) // EXPERT_BM * EXPERT_BM
    pad_end = jnp.cumsum(padded)
    pad_start = pad_end - padded
    eid = rinfo[:, 0:2].astype(jnp.int32)
    rank = rinfo[:, 4:6].astype(jnp.int32)
    dest = (pad_start[eid] + rank).reshape(n_tok // DISPATCH_T, 1, 2 * DISPATCH_T)
    n_blk = (2 * n_tok) // EXPERT_BM + N_EXPERTS
    blk_row = jnp.arange(n_blk, dtype=jnp.int32) * EXPERT_BM
    blk_e = jnp.minimum(jnp.searchsorted(pad_end, blk_row, side='right'), N_EXPERTS - 1).astype(jnp.int32)
    blk_valid = (blk_row < pad_end[-1]).astype(jnp.int32)

    xbuf = _dispatch_call(dest, z, jnp.zeros((n_blk * EXPERT_BM, D_MODEL), F32))
    ybuf = _expert_call(blk_e, blk_valid, xbuf, w_gate[0], w_up[0], w_down[0])
    out = _combine_call(dest, h, rinfo, final_norm.reshape(1, D_MODEL), ybuf)
    return out.reshape(bsz, seq, D_MODEL)
```

```python
import functools

import numpy as np
import jax
import jax.numpy as jnp
from jax import lax
from jax.experimental import pallas as pl
from jax.experimental.pallas import tpu as pltpu

F32 = jnp.float32
BF16 = jnp.bfloat16

D_MODEL = 1024
HEAD_DIM = 64
N_HEADS = 8
ATTN_W = N_HEADS * HEAD_DIM
CONV_W = 512
N_META = 16
N_GROUPS = 4
EXPERTS_PER_GROUP = 8
N_EXPERTS = N_GROUPS * EXPERTS_PER_GROUP
D_EXPERT = 512
EPS = 1e-6
NEG_BIG = -1e30
LOG2E = 1.4426950408889634

LANES = 128
HEAD_SLOT = 2 * HEAD_DIM
AUG_W = N_HEADS * HEAD_SLOT
N_PARTS = 3
PART_ROWS = 16
ONES_LANE = 8
ROUTER_LANE0 = N_GROUPS

PROJ_T = 512
ATT_T = 256
META_PAD = ATT_T
MIX_T = 256
DISPATCH_T = 256
EXPERT_BM = 256
COMBINE_T = 256
DMA_UNROLL = 8
VMEM_LIMIT = 48 * 1024 * 1024


def _split3(x):
    hi = x.astype(BF16)
    r1 = x - hi.astype(F32)
    mid = r1.astype(BF16)
    r2 = r1 - mid.astype(F32)
    return hi, mid, r2.astype(BF16)


def _log_sigmoid(x):
    return jnp.minimum(x, 0.0) - jnp.log1p(jnp.exp(-jnp.abs(x)))


def _head_offset(h):
    return HEAD_DIM if h % 2 == 0 else 0


def _selection_matrices():
    selk = np.zeros((N_PARTS * LANES, AUG_W), np.float32)
    selq = np.zeros((AUG_W, LANES), np.float32)
    for h in range(N_HEADS):
        base = h * HEAD_SLOT + _head_offset(h)
        for j in range(N_PARTS):
            selk[ONES_LANE, base + j] = 1.0
            selk[j * LANES + h, base + N_PARTS + j] = 1.0
            selq[base + j, j * PART_ROWS + h] = 1.0
            selq[base + N_PARTS + j, ONES_LANE] = 1.0
    return jnp.asarray(selk, BF16), jnp.asarray(selq, BF16)


def _proj_kernel(x_ref, g_ref, wrow_ref, wt_ref, bfr_ref, bfc_ref, convw_ref, selk_ref, selq_ref,
                 cin_row_ref, cin_col_ref, uin_ref,
                 kaug_ref, qaug_ref, vt_ref, conv_ref, *rest, t_rows, n_valid, emit_carry, tk):
    if emit_carry:
        cumrow_out, cumt_out, u_out, c_row, c_col, u_prev = rest
    else:
        c_row, c_col, u_prev = rest

    @pl.when(pl.program_id(1) == 0)
    def _():
        c_row[...] = cin_row_ref[...]
        c_col[...] = cin_col_ref[...]
        u_prev[...] = uin_ref[...]

    T = t_rows
    x = x_ref[...]
    ms = jnp.mean(x * x, axis=-1, keepdims=True)
    z = x * lax.rsqrt(ms + EPS) * g_ref[...]
    zb = z.astype(BF16)
    r = jnp.dot(zb, wrow_ref[...], preferred_element_type=F32)
    tt = lax.dot_general(wt_ref[...], zb, (((1,), (1,)), ((), ())),
                         preferred_element_type=F32)
    k = r[:, 0:ATTN_W]
    xc = r[:, ATTN_W:ATTN_W + CONV_W]
    bg = r[:, ATTN_W + CONV_W:ATTN_W + 2 * CONV_W]
    cg = r[:, ATTN_W + 2 * CONV_W:ATTN_W + 3 * CONV_W]
    fr = r[:, ATTN_W + 3 * CONV_W:ATTN_W + 3 * CONV_W + LANES]
    qt = tt[0:ATTN_W]
    vt = tt[ATTN_W:2 * ATTN_W]
    ft = tt[2 * ATTN_W:2 * ATTN_W + PART_ROWS]

    lane_r = lax.broadcasted_iota(jnp.int32, (T, LANES), 1)
    row_r = lax.broadcasted_iota(jnp.int32, (T, LANES), 0)
    lfr = jnp.where(lane_r < N_HEADS, _log_sigmoid(fr + bfr_ref[...]), 0.0)
    lfr = lfr + jnp.where(row_r == 0, c_row[0:1, :], 0.0)
    tri_r = lax.broadcasted_iota(jnp.int32, (T, T), 0)
    tri_c = lax.broadcasted_iota(jnp.int32, (T, T), 1)
    tri_l = (tri_c <= tri_r).astype(BF16)
    tri_u = (tri_r <= tri_c).astype(BF16)
    c3 = jnp.dot(tri_l, jnp.concatenate(_split3(lfr), axis=1), preferred_element_type=F32)
    cum_row = c3[:, 0:LANES] + c3[:, LANES:2 * LANES] + c3[:, 2 * LANES:3 * LANES]

    kb = cum_row * (-LOG2E)
    if n_valid < T:
        kb = kb + jnp.where(row_r >= n_valid, NEG_BIG, 0.0)
    kb = jnp.where(lane_r == ONES_LANE, 1.0, kb)
    e = jnp.dot(jnp.concatenate(_split3(kb), axis=1), selk_ref[...],
                preferred_element_type=F32)
    for p in range(N_HEADS // 2):
        kp = k[:, p * LANES:(p + 1) * LANES].astype(BF16)
        e0 = e[:, 2 * p * LANES:(2 * p + 1) * LANES].astype(BF16)
        e1 = e[:, (2 * p + 1) * LANES:(2 * p + 2) * LANES].astype(BF16)
        kaug_ref[:, 2 * p * LANES:(2 * p + 1) * LANES] = jnp.where(lane_r < HEAD_DIM, kp, e0)
        kaug_ref[:, (2 * p + 1) * LANES:(2 * p + 2) * LANES] = jnp.where(lane_r >= HEAD_DIM, kp, e1)

    lane_c = lax.broadcasted_iota(jnp.int32, (PART_ROWS, T), 1)
    row_c = lax.broadcasted_iota(jnp.int32, (PART_ROWS, T), 0)
    lfc = jnp.where(row_c < N_HEADS, _log_sigmoid(ft + bfc_ref[:, 0:1]), 0.0)
    lfc = lfc + jnp.where(lane_c == 0, c_col[:, 0:1], 0.0)
    c3c = jnp.dot(jnp.concatenate(_split3(lfc), axis=0), tri_u, preferred_element_type=F32)
    cum_t = c3c[0:PART_ROWS] + c3c[PART_ROWS:2 * PART_ROWS] + c3c[2 * PART_ROWS:3 * PART_ROWS]
    cq = jnp.where(row_c == ONES_LANE, 1.0, cum_t * LOG2E)
    pm_t = jnp.concatenate(_split3(cq) + (jnp.zeros((LANES - N_PARTS * PART_ROWS, T), BF16),), axis=0)
    et = jnp.dot(selq_ref[...], pm_t, preferred_element_type=F32)
    for h in range(N_HEADS):
        qh = (qt[h * HEAD_DIM:(h + 1) * HEAD_DIM] * LOG2E).astype(BF16)
        lo = h * HEAD_SLOT
        if h % 2 == 0:
            qaug_ref[lo:lo + HEAD_DIM, :] = qh
            qaug_ref[lo + HEAD_DIM:lo + HEAD_SLOT, :] = et[lo + HEAD_DIM:lo + HEAD_SLOT].astype(BF16)
        else:
            qaug_ref[lo:lo + HEAD_DIM, :] = et[lo:lo + HEAD_DIM].astype(BF16)
            qaug_ref[lo + HEAD_DIM:lo + HEAD_SLOT, :] = qh

    vtb = vt.astype(BF16)
    for s in range(T // tk):
        vt_ref[s] = vtb[:, s * tk:(s + 1) * tk]

    u = cg * xc
    row_u = lax.broadcasted_iota(jnp.int32, (T, CONV_W), 0)
    p1 = u_prev[7:8, :]
    p2 = u_prev[6:7, :]
    u1 = jnp.where(row_u == 0, p1, pltpu.roll(u, 1, axis=0))
    u2 = jnp.where(row_u == 0, p2, jnp.where(row_u == 1, p1, pltpu.roll(u, 2, axis=0)))
    zc = convw_ref[0:1, :] * u2 + convw_ref[1:2, :] * u1 + convw_ref[2:3, :] * u
    conv_ref[...] = bg * zc

    if emit_carry:
        cumrow_out[...] = cum_row
        cumt_out[...] = cum_t
        u_out[...] = u
    c_row[0:1, :] = cum_row[T - 1:T, :]
    c_col[...] = jnp.broadcast_to(cum_t[:, T - 1:T], (PART_ROWS, LANES))
    u_prev[...] = u[T - 8:T, :]


def _proj_call(xs, g, wrow, wt, bfr, bfc, convw, selk, selq, cin_row, cin_col, uin,
               *, t_rows, n_valid, emit_carry):
    bn, ln, _ = xs.shape
    T = t_rows
    tk = min(ATT_T, T)
    nt = ln // T
    const2 = lambda b, t: (0, 0)
    in_specs = [
        pl.BlockSpec((None, T, D_MODEL), lambda b, t: (b, t, 0)),
        pl.BlockSpec(g.shape, const2),
        pl.BlockSpec(wrow.shape, const2),
        pl.BlockSpec(wt.shape, const2),
        pl.BlockSpec(bfr.shape, const2),
        pl.BlockSpec(bfc.shape, const2),
        pl.BlockSpec(convw.shape, const2),
        pl.BlockSpec(selk.shape, const2),
        pl.BlockSpec(selq.shape, const2),
        pl.BlockSpec(cin_row.shape, const2),
        pl.BlockSpec(cin_col.shape, const2),
        pl.BlockSpec(uin.shape, const2),
    ]
    out_shape = [
        jax.ShapeDtypeStruct((bn, ln, AUG_W), BF16),
        jax.ShapeDtypeStruct((bn, AUG_W, ln), BF16),
        jax.ShapeDtypeStruct((bn, ln // tk, ATTN_W, tk), BF16),
        jax.ShapeDtypeStruct((bn, ln, CONV_W), F32),
    ]
    out_specs = [
        pl.BlockSpec((None, T, AUG_W), lambda b, t: (b, t, 0)),
        pl.BlockSpec((None, AUG_W, T), lambda b, t: (b, 0, t)),
        pl.BlockSpec((None, T // tk, ATTN_W, tk), lambda b, t: (b, t, 0, 0)),
        pl.BlockSpec((None, T, CONV_W), lambda b, t: (b, t, 0)),
    ]
    if emit_carry:
        out_shape += [
            jax.ShapeDtypeStruct((bn, ln, LANES), F32),
            jax.ShapeDtypeStruct((bn, PART_ROWS, ln), F32),
            jax.ShapeDtypeStruct((bn, ln, CONV_W), F32),
        ]
        out_specs += [
            pl.BlockSpec((None, T, LANES), lambda b, t: (b, t, 0)),
            pl.BlockSpec((None, PART_ROWS, T), lambda b, t: (b, 0, t)),
            pl.BlockSpec((None, T, CONV_W), lambda b, t: (b, t, 0)),
        ]
    kern = functools.partial(_proj_kernel, t_rows=T, n_valid=n_valid, emit_carry=emit_carry, tk=tk)
    return pl.pallas_call(
        kern,
        grid=(bn, nt),
        in_specs=in_specs,
        out_specs=out_specs,
        out_shape=out_shape,
        scratch_shapes=[pltpu.VMEM((8, LANES), F32), pltpu.VMEM((PART_ROWS, LANES), F32),
                        pltpu.VMEM((8, CONV_W), F32)],
        compiler_params=pltpu.CompilerParams(dimension_semantics=("arbitrary", "arbitrary"),
                                             vmem_limit_bytes=VMEM_LIMIT),
        name="proj_meta" if emit_carry else "proj",
    )(xs, g, wrow, wt, bfr, bfc, convw, selk, selq, cin_row, cin_col, uin)


def _attn_kernel(q_ref, k_ref, v_ref, km_ref, vm_ref, g_ref, o_ref, s_s, m_s, l_s, acc_s):
    i = pl.program_id(1)
    T = ATT_T
    causal = (lax.broadcasted_iota(jnp.int32, (T, T), 0) <= lax.broadcasted_iota(jnp.int32, (T, T), 1))

    def scores(h, kblk, slot, masked):
        qa = q_ref[h * HEAD_SLOT:(h + 1) * HEAD_SLOT, :]
        s = jnp.dot(kblk, qa, preferred_element_type=F32)
        if masked:
            s = jnp.where(causal, s, NEG_BIG)
        s_s[slot, h] = s

    def consume(h, vblk, slot):
        s = s_s[slot, h]
        m_old = m_s[h, 0:1, :]
        m_new = jnp.maximum(m_old, jnp.max(s, axis=0, keepdims=True))
        alpha = jnp.exp2(m_old - m_new)
        pm = jnp.exp2(s - m_new)
        l_s[h, 0:1, :] = alpha * l_s[h, 0:1, :] + jnp.sum(pm, axis=0, keepdims=True)
        acc_s[h] = alpha * acc_s[h] + jnp.dot(vblk, pm.astype(BF16), preferred_element_type=F32)
        m_s[h, 0:1, :] = m_new

    def step(k_next, next_slot, masked, v_cur, cur_slot):
        ahead = 2
        for h in range(ahead):
            scores(h, k_next(h), next_slot, masked)
        for h in range(N_HEADS):
            consume(h, v_cur(h), cur_slot)
            if h + ahead < N_HEADS:
                scores(h + ahead, k_next(h + ahead), next_slot, masked)

    def k_of(j):
        row = pl.ds(pl.multiple_of(j * T, T), T)
        return lambda h: k_ref[row, h * HEAD_SLOT:(h + 1) * HEAD_SLOT]

    def v_of(j):
        return lambda h: v_ref[j, h * HEAD_DIM:(h + 1) * HEAD_DIM, :]

    k_meta = lambda h: km_ref[:, h * HEAD_SLOT:(h + 1) * HEAD_SLOT]
    v_meta = lambda h: vm_ref[h * HEAD_DIM:(h + 1) * HEAD_DIM, :]

    m_s[...] = jnp.full(m_s.shape, NEG_BIG, F32)
    l_s[...] = jnp.zeros(l_s.shape, F32)
    acc_s[...] = jnp.zeros(acc_s.shape, F32)
    for h in range(N_HEADS):
        scores(h, k_meta(h), 0, False)

    def v_at(t):
        return lambda h: jnp.where(t == 0, v_meta(h), v_ref[jnp.maximum(t - 1, 0), h * HEAD_DIM:(h + 1) * HEAD_DIM, :])

    def pair_body(p, c):
        t0 = 2 * p
        step(k_of(t0), 1, False, v_at(t0), 0)
        step(k_of(t0 + 1), 0, False, v_at(t0 + 1), 1)
        return c

    lax.fori_loop(0, i // 2, pair_body, 0)

    @pl.when(i % 2 == 1)
    def _():
        step(k_of(i - 1), 1, False, v_at(i - 1), 0)
        step(k_of(i), 0, True, v_at(i), 1)
        for h in range(N_HEADS):
            consume(h, v_of(i)(h), 0)

    @pl.when(i % 2 == 0)
    def _():
        step(k_of(i), 1, True, v_at(i), 0)
        for h in range(N_HEADS):
            consume(h, v_of(i)(h), 1)

    for p in range(N_HEADS // 2):
        halves = []
        for h in (2 * p, 2 * p + 1):
            o = acc_s[h] * (1.0 / l_s[h, 0:1, :])
            halves.append(o * lax.rsqrt(jnp.mean(o * o, axis=0, keepdims=True) + EPS))
        pair = jnp.concatenate(halves, axis=0)
        o_ref[:, p * LANES:(p + 1) * LANES] = (pair.T * g_ref[:, p * LANES:(p + 1) * LANES]).astype(o_ref.dtype)


def _attn_call(qaug, kaug, vt, kmeta, vmeta, gain):
    bn, _, ln = qaug.shape
    T = ATT_T
    nq = ln // T
    return pl.pallas_call(
        _attn_kernel,
        grid=(bn, nq),
        in_specs=[
            pl.BlockSpec((None, AUG_W, T), lambda b, i: (b, 0, i)),
            pl.BlockSpec((None, ln, AUG_W), lambda b, i: (b, 0, 0)),
            pl.BlockSpec((None, ln // T, ATTN_W, T), lambda b, i: (b, 0, 0, 0)),
            pl.BlockSpec(kmeta.shape, lambda b, i: (0, 0)),
            pl.BlockSpec(vmeta.shape, lambda b, i: (0, 0)),
            pl.BlockSpec(gain.shape, lambda b, i: (0, 0)),
        ],
        out_specs=pl.BlockSpec((None, T, ATTN_W), lambda b, i: (b, i, 0)),
        out_shape=jax.ShapeDtypeStruct((bn, ln, ATTN_W), BF16),
        scratch_shapes=[pltpu.VMEM((2, N_HEADS, T, T), F32),
                        pltpu.VMEM((N_HEADS, 8, T), F32), pltpu.VMEM((N_HEADS, 8, T), F32),
                        pltpu.VMEM((N_HEADS, HEAD_DIM, T), F32)],
        compiler_params=pltpu.CompilerParams(dimension_semantics=("arbitrary", "arbitrary"),
                                             vmem_limit_bytes=VMEM_LIMIT),
        name="attention",
    )(qaug, kaug, vt, kmeta, vmeta, gain)


def _mix_kernel(attn_ref, conv_ref, x_ref, gc_ref, gsum_ref, woa_ref, woc_ref, fg_ref,
                wrh_ref, wrl_ref, br_ref, h_ref, z_ref, rinfo_ref, counts_ref, cnt):
    T = MIX_T

    @pl.when(pl.program_id(0) == 0)
    def _():
        cnt[...] = jnp.zeros_like(cnt)

    c = conv_ref[...]
    c2 = c * c
    c2h = c2.astype(BF16)
    c2l = (c2 - c2h.astype(F32)).astype(BF16)
    ss = (jnp.dot(c2h, gsum_ref[...], preferred_element_type=F32)
          + jnp.dot(c2l, gsum_ref[...], preferred_element_type=F32))
    cn = (c * lax.rsqrt(ss * (1.0 / HEAD_DIM) + EPS) * gc_ref[...]).astype(BF16)
    hadd = (jnp.dot(attn_ref[...], woa_ref[...], preferred_element_type=F32)
            + jnp.dot(cn, woc_ref[...], preferred_element_type=F32))
    h = x_ref[...] + hadd
    h_ref[...] = h
    z = h * lax.rsqrt(jnp.mean(h * h, axis=-1, keepdims=True) + EPS) * fg_ref[...]
    z_ref[...] = z
    zh = z.astype(BF16)
    zl = (z - zh.astype(F32)).astype(BF16)
    logits = (jnp.dot(zh, wrh_ref[...], preferred_element_type=F32)
              + jnp.dot(zl, wrh_ref[...], preferred_element_type=F32)
              + jnp.dot(zh, wrl_ref[...], preferred_element_type=F32)) + br_ref[...]

    lane = lax.broadcasted_iota(jnp.int32, (T, LANES), 1).astype(F32)
    big = float(LANES)
    gl = jnp.where(lane < N_GROUPS, logits, -jnp.inf)
    gmax = jnp.max(gl, axis=-1, keepdims=True)
    gsum = jnp.sum(jnp.exp(gl - gmax), axis=-1, keepdims=True)
    g_w = 1.0 / gsum
    g_idx = jnp.min(jnp.where(gl == gmax, lane, big), axis=-1, keepdims=True)
    e_lo = ROUTER_LANE0 + EXPERTS_PER_GROUP * g_idx
    emask = (lane >= e_lo) & (lane < e_lo + EXPERTS_PER_GROUP)
    el = jnp.where(emask, logits, -jnp.inf)
    emax = jnp.max(el, axis=-1, keepdims=True)
    eexp = jnp.exp(el - emax)
    probs = eexp / jnp.sum(eexp, axis=-1, keepdims=True)
    probs = jnp.where(emask, probs, -1.0)
    p1 = jnp.max(probs, axis=-1, keepdims=True)
    i1 = jnp.min(jnp.where(probs == p1, lane, big), axis=-1, keepdims=True)
    probs2 = jnp.where(lane == i1, -1.0, probs)
    p2 = jnp.max(probs2, axis=-1, keepdims=True)
    i2 = jnp.min(jnp.where(probs2 == p2, lane, big), axis=-1, keepdims=True)
    psum = p1 + p2
    w1 = g_w * (p1 / psum)
    w2 = g_w * (p2 / psum)

    hit = ((lane == i1) | (lane == i2)).astype(F32)
    tri_r = lax.broadcasted_iota(jnp.int32, (T, T), 0)
    tri_c = lax.broadcasted_iota(jnp.int32, (T, T), 1)
    strict_l = (tri_c < tri_r).astype(BF16)
    before = jnp.dot(strict_l, hit.astype(BF16), preferred_element_type=F32) + cnt[0:1, :]
    r1 = jnp.sum(jnp.where(lane == i1, before, 0.0), axis=-1, keepdims=True)
    r2 = jnp.sum(jnp.where(lane == i2, before, 0.0), axis=-1, keepdims=True)
    cnt[0:1, :] = cnt[0:1, :] + jnp.sum(hit, axis=0, keepdims=True)
    counts_ref[...] = cnt[...]

    e1 = i1 - ROUTER_LANE0
    e2 = i2 - ROUTER_LANE0
    rinfo = jnp.where(lane == 0, e1, jnp.where(lane == 1, e2, jnp.where(lane == 2, w1, jnp.where(
        lane == 3, w2, jnp.where(lane == 4, r1, jnp.where(lane == 5, r2, 0.0))))))
    rinfo_ref[...] = rinfo


def _mix_call(attn, conv, x, gc, gsum, woa, woc, fg, wrh, wrl, br):
    n = x.shape[0]
    T = MIX_T
    const = lambda i: (0, 0)
    row = lambda w: pl.BlockSpec((T, w), lambda i: (i, 0))
    return pl.pallas_call(
        _mix_kernel,
        grid=(n // T,),
        in_specs=[row(ATTN_W), row(CONV_W), row(D_MODEL)] + [
            pl.BlockSpec(a.shape, const) for a in (gc, gsum, woa, woc, fg, wrh, wrl, br)],
        out_specs=[row(D_MODEL), row(D_MODEL), row(LANES), pl.BlockSpec((8, LANES), const)],
        out_shape=[jax.ShapeDtypeStruct((n, D_MODEL), F32), jax.ShapeDtypeStruct((n, D_MODEL), F32),
                   jax.ShapeDtypeStruct((n, LANES), F32), jax.ShapeDtypeStruct((8, LANES), F32)],
        scratch_shapes=[pltpu.VMEM((8, LANES), F32)],
        compiler_params=pltpu.CompilerParams(dimension_semantics=("arbitrary",),
                                             vmem_limit_bytes=VMEM_LIMIT),
        name="mix",
    )(attn, conv, x, gc, gsum, woa, woc, fg, wrh, wrl, br)


def _dispatch_kernel(dest_ref, z_ref, xbuf_in, xbuf_ref, sem):
    del xbuf_in
    T = DISPATCH_T

    def row_copy(r, slot):
        d = dest_ref[0, 2 * r + slot]
        return pltpu.make_async_copy(z_ref.at[pl.ds(r, 1)], xbuf_ref.at[pl.ds(d, 1)], sem)

    def start(r, c):
        row_copy(r, 0).start()
        row_copy(r, 1).start()
        return c

    def wait(r, c):
        row_copy(r, 0).wait()
        row_copy(r, 1).wait()
        return c

    lax.fori_loop(0, T, start, 0, unroll=DMA_UNROLL)
    lax.fori_loop(0, T, wait, 0, unroll=DMA_UNROLL)


def _dispatch_call(dest, z, xbuf0):
    n = z.shape[0]
    T = DISPATCH_T
    return pl.pallas_call(
        _dispatch_kernel,
        grid=(n // T,),
        in_specs=[
            pl.BlockSpec((None, 1, 2 * T), lambda i: (i, 0, 0), memory_space=pltpu.SMEM),
            pl.BlockSpec((T, D_MODEL), lambda i: (i, 0)),
            pl.BlockSpec(memory_space=pl.ANY),
        ],
        out_specs=pl.BlockSpec(memory_space=pl.ANY),
        out_shape=jax.ShapeDtypeStruct(xbuf0.shape, xbuf0.dtype),
        scratch_shapes=[pltpu.SemaphoreType.DMA(())],
        input_output_aliases={2: 0},
        compiler_params=pltpu.CompilerParams(dimension_semantics=("arbitrary",)),
        name="dispatch",
    )(dest, z, xbuf0)


def _expert_kernel(blk_e_ref, blk_valid_ref, x_ref, wg_ref, wu_ref, wd_ref, y_ref, wgu_s, wd_s):
    blk = pl.program_id(0)
    prev_e = blk_e_ref[jnp.maximum(blk - 1, 0)]

    @pl.when((blk == 0) | (blk_e_ref[blk] != prev_e))
    def _():
        wgu_s[:, 0:D_EXPERT] = wg_ref[...].astype(BF16)
        wgu_s[:, D_EXPERT:2 * D_EXPERT] = wu_ref[...].astype(BF16)
        wd_s[...] = wd_ref[...].astype(BF16)

    @pl.when(blk_valid_ref[blk] != 0)
    def _():
        xb = x_ref[...].astype(BF16)
        gu = jnp.dot(xb, wgu_s[...], preferred_element_type=F32)
        g = gu[:, 0:D_EXPERT]
        u = gu[:, D_EXPERT:2 * D_EXPERT]
        a = (g * jax.nn.sigmoid(g)) * u
        y_ref[...] = jnp.dot(a.astype(BF16), wd_s[...], preferred_element_type=F32)

    @pl.when(blk_valid_ref[blk] == 0)
    def _():
        y_ref[...] = jnp.zeros_like(y_ref)


def _expert_call(blk_e, blk_valid, xbuf, w_gate, w_up, w_down):
    rows = xbuf.shape[0]
    bm = EXPERT_BM
    grid_spec = pltpu.PrefetchScalarGridSpec(
        num_scalar_prefetch=2,
        grid=(rows // bm,),
        in_specs=[
            pl.BlockSpec((bm, D_MODEL), lambda i, be, bv: (i, 0)),
            pl.BlockSpec((None, D_MODEL, D_EXPERT), lambda i, be, bv: (be[i], 0, 0)),
            pl.BlockSpec((None, D_MODEL, D_EXPERT), lambda i, be, bv: (be[i], 0, 0)),
            pl.BlockSpec((None, D_EXPERT, D_MODEL), lambda i, be, bv: (be[i], 0, 0)),
        ],
        out_specs=pl.BlockSpec((bm, D_MODEL), lambda i, be, bv: (i, 0)),
        scratch_shapes=[pltpu.VMEM((D_MODEL, 2 * D_EXPERT), BF16), pltpu.VMEM((D_EXPERT, D_MODEL), BF16)],
    )
    return pl.pallas_call(
        _expert_kernel,
        grid_spec=grid_spec,
        out_shape=jax.ShapeDtypeStruct((rows, D_MODEL), F32),
        compiler_params=pltpu.CompilerParams(dimension_semantics=("arbitrary",),
                                             vmem_limit_bytes=VMEM_LIMIT),
        name="experts",
    )(blk_e, blk_valid, xbuf, w_gate, w_up, w_down)


def _combine_kernel(dest_ref, h_ref, rinfo_ref, fn_ref, ybuf_ref, o_ref, ybuf_v, sem):
    T = COMBINE_T

    def row_copy(r, slot):
        d = dest_ref[0, 2 * r + slot]
        return pltpu.make_async_copy(ybuf_ref.at[pl.ds(d, 1)], ybuf_v.at[slot, pl.ds(r, 1)], sem)

    def start(r, c):
        row_copy(r, 0).start()
        row_copy(r, 1).start()
        return c

    def wait(r, c):
        row_copy(r, 0).wait()
        row_copy(r, 1).wait()
        return c

    lax.fori_loop(0, T, start, 0, unroll=DMA_UNROLL)
    lax.fori_loop(0, T, wait, 0, unroll=DMA_UNROLL)
    rinfo = rinfo_ref[...]
    w1 = rinfo[:, 2:3]
    w2 = rinfo[:, 3:4]
    h = h_ref[...] + (ybuf_v[0] * w1 + ybuf_v[1] * w2)
    o_ref[...] = h * lax.rsqrt(jnp.mean(h * h, axis=-1, keepdims=True) + EPS) * fn_ref[...]


def _combine_call(dest, h, rinfo, fn, ybuf):
    n = h.shape[0]
    T = COMBINE_T
    return pl.pallas_call(
        _combine_kernel,
        grid=(n // T,),
        in_specs=[
            pl.BlockSpec((None, 1, 2 * T), lambda i: (i, 0, 0), memory_space=pltpu.SMEM),
            pl.BlockSpec((T, D_MODEL), lambda i: (i, 0)),
            pl.BlockSpec((T, LANES), lambda i: (i, 0)),
            pl.BlockSpec(fn.shape, lambda i: (0, 0)),
            pl.BlockSpec(memory_space=pl.ANY),
        ],
        out_specs=pl.BlockSpec((T, D_MODEL), lambda i: (i, 0)),
        out_shape=jax.ShapeDtypeStruct((n, D_MODEL), F32),
        scratch_shapes=[pltpu.VMEM((2, T, D_MODEL), F32), pltpu.SemaphoreType.DMA(())],
        compiler_params=pltpu.CompilerParams(dimension_semantics=("arbitrary",),
                                             vmem_limit_bytes=VMEM_LIMIT),
        name="combine",
    )(dest, h, rinfo, fn, ybuf)


def kernel(x, meta_tokens, attn_norm, w_in, b_forget, conv_w, mix_norm, w_out, ffn_norm,
           w_router_group, b_router_group, w_router_expert, b_router_expert,
           w_gate, w_up, w_down, final_norm):
    bsz, seq, d = x.shape
    assert d == D_MODEL and seq % PROJ_T == 0 and meta_tokens.shape[0] == N_META
    assert attn_norm.shape[0] == 1, "single-layer block"
    n_tok = bsz * seq

    wi = w_in[0]
    o = 0
    w_q = wi[:, o:o + ATTN_W]; o += ATTN_W
    w_k = wi[:, o:o + ATTN_W]; o += ATTN_W
    w_v = wi[:, o:o + ATTN_W]; o += ATTN_W
    w_f = wi[:, o:o + N_HEADS]; o += N_HEADS
    w_xc = wi[:, o:o + CONV_W]; o += CONV_W
    w_bg = wi[:, o:o + CONV_W]; o += CONV_W
    w_cg = wi[:, o:o + CONV_W]
    wrow = jnp.concatenate([w_k, w_xc, w_bg, w_cg, w_f, jnp.zeros((D_MODEL, LANES - N_HEADS), F32)],
                           axis=1).astype(BF16)
    wt = jnp.concatenate([w_q.T * (HEAD_DIM ** -0.5), w_v.T, w_f.T,
                          jnp.zeros((PART_ROWS - N_HEADS, D_MODEL), F32)], axis=0).astype(BF16)
    g_attn = attn_norm[0].reshape(1, D_MODEL)
    bfr = jnp.zeros((1, LANES), F32).at[0, :N_HEADS].set(b_forget[0])
    bfc = jnp.zeros((PART_ROWS, LANES), F32).at[:N_HEADS, :].set(b_forget[0][:, None])
    convw = jnp.zeros((8, CONV_W), F32).at[:3].set(conv_w[0])
    selk, selq = _selection_matrices()

    meta_pad = jnp.zeros((1, META_PAD, D_MODEL), F32).at[0, :N_META].set(meta_tokens)
    zrow = jnp.zeros((8, LANES), F32)
    zcol = jnp.zeros((PART_ROWS, LANES), F32)
    zu = jnp.zeros((8, CONV_W), F32)
    kmeta, _, vmeta, _, cumrow_m, cumt_m, u_m = _proj_call(
        meta_pad, g_attn, wrow, wt, bfr, bfc, convw, selk, selq, zrow, zcol, zu,
        t_rows=META_PAD, n_valid=N_META, emit_carry=True)
    cin_row = jnp.zeros((8, LANES), F32).at[0].set(cumrow_m[0, N_META - 1])
    cin_col = jnp.broadcast_to(cumt_m[0, :, N_META - 1:N_META], (PART_ROWS, LANES))
    uin = u_m[0, N_META - 8:N_META]

    kaug, qaug, vt, conv = _proj_call(
        x, g_attn, wrow, wt, bfr, bfc, convw, selk, selq, cin_row, cin_col, uin,
        t_rows=PROJ_T, n_valid=PROJ_T, emit_carry=False)

    mixg = mix_norm[0]
    attn = _attn_call(qaug, kaug, vt, kmeta[0], vmeta[0, 0], mixg[:ATTN_W].reshape(1, ATTN_W))

    gidx = np.arange(CONV_W) // HEAD_DIM
    gsum = jnp.asarray(gidx[:, None] == gidx[None, :], BF16)
    wo = w_out[0].astype(BF16)
    wr = jnp.zeros((D_MODEL, LANES), F32)
    wr = wr.at[:, :N_GROUPS].set(w_router_group[0]).at[:, ROUTER_LANE0:ROUTER_LANE0 + N_EXPERTS].set(
        w_router_expert[0])
    wrh = wr.astype(BF16)
    wrl = (wr - wrh.astype(F32)).astype(BF16)
    br = jnp.zeros((1, LANES), F32)
    br = br.at[0, :N_GROUPS].set(b_router_group[0]).at[0, ROUTER_LANE0:ROUTER_LANE0 + N_EXPERTS].set(
        b_router_expert[0])
    h, z, rinfo, counts = _mix_call(
        attn.reshape(n_tok, ATTN_W), conv.reshape(n_tok, CONV_W), x.reshape(n_tok, D_MODEL),
        mixg[ATTN_W:].reshape(1, CONV_W), gsum, wo[:ATTN_W], wo[ATTN_W:], ffn_norm[0].reshape(1, D_MODEL),
        wrh, wrl, br)

    cnt = counts[0, ROUTER_LANE0:ROUTER_LANE0 + N_EXPERTS].astype(jnp.int32)
    padded = (cnt + EXPERT_BM - 1) // EXPERT_BM * EXPERT_BM
    pad_end = jnp.cumsum(padded)
    pad_start = pad_end - padded
    eid = rinfo[:, 0:2].astype(jnp.int32)
    rank = rinfo[:, 4:6].astype(jnp.int32)
    dest = (pad_start[eid] + rank).reshape(n_tok // DISPATCH_T, 1, 2 * DISPATCH_T)
    n_blk = (2 * n_tok) // EXPERT_BM + N_EXPERTS
    blk_row = jnp.arange(n_blk, dtype=jnp.int32) * EXPERT_BM
    blk_e = jnp.minimum(jnp.sum(blk_row[:, None] >= pad_end[None, :], axis=1), N_EXPERTS - 1).astype(jnp.int32)
    blk_valid = (blk_row < pad_end[-1]).astype(jnp.int32)

    xbuf = _dispatch_call(dest, z, jnp.zeros((n_blk * EXPERT_BM, D_MODEL), F32))
    ybuf = _expert_call(blk_e, blk_valid, xbuf, w_gate[0], w_up[0], w_down[0])
    out = _combine_call(dest, h, rinfo, final_norm.reshape(1, D_MODEL), ybuf)
    return out.reshape(bsz, seq, D_MODEL)
```

```python
import functools

import numpy as np
import jax
import jax.numpy as jnp
from jax import lax
from jax.experimental import pallas as pl
from jax.experimental.pallas import tpu as pltpu

F32 = jnp.float32
BF16 = jnp.bfloat16

D_MODEL = 1024
HEAD_DIM = 64
N_HEADS = 8
ATTN_W = N_HEADS * HEAD_DIM
CONV_W = 512
N_META = 16
N_GROUPS = 4
EXPERTS_PER_GROUP = 8
N_EXPERTS = N_GROUPS * EXPERTS_PER_GROUP
D_EXPERT = 512
EPS = 1e-6
NEG_BIG = -1e30
LOG2E = 1.4426950408889634

LANES = 128
HEAD_SLOT = 2 * HEAD_DIM
V_SLOT = HEAD_DIM + 16
V_AUG = N_HEADS * V_SLOT
AUG_W = N_HEADS * HEAD_SLOT
N_PARTS = 3
PART_ROWS = 16
ONES_LANE = 8
ROUTER_LANE0 = N_GROUPS

PROJ_T = 512
ATT_T = 256
META_PAD = ATT_T
MIX_T = 256
DISPATCH_T = 256
EXPERT_BM = 256
COMBINE_T = 256
DMA_UNROLL = 8
VMEM_LIMIT = 48 * 1024 * 1024


def _split3(x):
    hi = x.astype(BF16)
    r1 = x - hi.astype(F32)
    mid = r1.astype(BF16)
    r2 = r1 - mid.astype(F32)
    return hi, mid, r2.astype(BF16)


def _log_sigmoid(x):
    return jnp.minimum(x, 0.0) - jnp.log1p(jnp.exp(-jnp.abs(x)))


def _head_offset(h):
    return HEAD_DIM if h % 2 == 0 else 0


def _selection_matrices():
    selk = np.zeros((N_PARTS * LANES, AUG_W), np.float32)
    selq = np.zeros((AUG_W, LANES), np.float32)
    for h in range(N_HEADS):
        base = h * HEAD_SLOT + _head_offset(h)
        for j in range(N_PARTS):
            selk[ONES_LANE, base + j] = 1.0
            selk[j * LANES + h, base + N_PARTS + j] = 1.0
            selq[base + j, j * PART_ROWS + h] = 1.0
            selq[base + N_PARTS + j, ONES_LANE] = 1.0
    return jnp.asarray(selk, BF16), jnp.asarray(selq, BF16)


def _proj_kernel(x_ref, g_ref, wrow_ref, wt_ref, bfr_ref, bfc_ref, convw_ref, selk_ref, selq_ref,
                 cin_row_ref, cin_col_ref, uin_ref,
                 kaug_ref, qaug_ref, vt_ref, conv_ref, *rest, t_rows, n_valid, emit_carry, tk):
    if emit_carry:
        cumrow_out, cumt_out, u_out, c_row, c_col, u_prev = rest
    else:
        c_row, c_col, u_prev = rest

    @pl.when(pl.program_id(1) == 0)
    def _():
        c_row[...] = cin_row_ref[...]
        c_col[...] = cin_col_ref[...]
        u_prev[...] = uin_ref[...]

    T = t_rows
    x = x_ref[...]
    ms = jnp.mean(x * x, axis=-1, keepdims=True)
    z = x * lax.rsqrt(ms + EPS) * g_ref[...]
    zb = z.astype(BF16)
    r = jnp.dot(zb, wrow_ref[...], preferred_element_type=F32)
    tt = lax.dot_general(wt_ref[...], zb, (((1,), (1,)), ((), ())),
                         preferred_element_type=F32)
    k = r[:, 0:ATTN_W]
    xc = r[:, ATTN_W:ATTN_W + CONV_W]
    bg = r[:, ATTN_W + CONV_W:ATTN_W + 2 * CONV_W]
    cg = r[:, ATTN_W + 2 * CONV_W:ATTN_W + 3 * CONV_W]
    fr = r[:, ATTN_W + 3 * CONV_W:ATTN_W + 3 * CONV_W + LANES]
    qt = tt[0:ATTN_W]
    vt = tt[ATTN_W:2 * ATTN_W]
    ft = tt[2 * ATTN_W:2 * ATTN_W + PART_ROWS]

    lane_r = lax.broadcasted_iota(jnp.int32, (T, LANES), 1)
    row_r = lax.broadcasted_iota(jnp.int32, (T, LANES), 0)
    lfr = jnp.where(lane_r < N_HEADS, _log_sigmoid(fr + bfr_ref[...]), 0.0)
    lfr = lfr + jnp.where(row_r == 0, c_row[0:1, :], 0.0)
    tri_r = lax.broadcasted_iota(jnp.int32, (T, T), 0)
    tri_c = lax.broadcasted_iota(jnp.int32, (T, T), 1)
    tri_l = (tri_c <= tri_r).astype(BF16)
    tri_u = (tri_r <= tri_c).astype(BF16)
    c3 = jnp.dot(tri_l, jnp.concatenate(_split3(lfr), axis=1), preferred_element_type=F32)
    cum_row = c3[:, 0:LANES] + c3[:, LANES:2 * LANES] + c3[:, 2 * LANES:3 * LANES]

    kb = cum_row * (-LOG2E)
    if n_valid < T:
        kb = kb + jnp.where(row_r >= n_valid, NEG_BIG, 0.0)
    kb = jnp.where(lane_r == ONES_LANE, 1.0, kb)
    e = jnp.dot(jnp.concatenate(_split3(kb), axis=1), selk_ref[...],
                preferred_element_type=F32)
    for p in range(N_HEADS // 2):
        kp = k[:, p * LANES:(p + 1) * LANES].astype(BF16)
        e0 = e[:, 2 * p * LANES:(2 * p + 1) * LANES].astype(BF16)
        e1 = e[:, (2 * p + 1) * LANES:(2 * p + 2) * LANES].astype(BF16)
        kaug_ref[:, 2 * p * LANES:(2 * p + 1) * LANES] = jnp.where(lane_r < HEAD_DIM, kp, e0)
        kaug_ref[:, (2 * p + 1) * LANES:(2 * p + 2) * LANES] = jnp.where(lane_r >= HEAD_DIM, kp, e1)

    lane_c = lax.broadcasted_iota(jnp.int32, (PART_ROWS, T), 1)
    row_c = lax.broadcasted_iota(jnp.int32, (PART_ROWS, T), 0)
    lfc = jnp.where(row_c < N_HEADS, _log_sigmoid(ft + bfc_ref[:, 0:1]), 0.0)
    lfc = lfc + jnp.where(lane_c == 0, c_col[:, 0:1], 0.0)
    c3c = jnp.dot(jnp.concatenate(_split3(lfc), axis=0), tri_u, preferred_element_type=F32)
    cum_t = c3c[0:PART_ROWS] + c3c[PART_ROWS:2 * PART_ROWS] + c3c[2 * PART_ROWS:3 * PART_ROWS]
    cq = jnp.where(row_c == ONES_LANE, 1.0, cum_t * LOG2E)
    pm_t = jnp.concatenate(_split3(cq) + (jnp.zeros((LANES - N_PARTS * PART_ROWS, T), BF16),), axis=0)
    et = jnp.dot(selq_ref[...], pm_t, preferred_element_type=F32)
    for h in range(N_HEADS):
        qh = (qt[h * HEAD_DIM:(h + 1) * HEAD_DIM] * LOG2E).astype(BF16)
        lo = h * HEAD_SLOT
        if h % 2 == 0:
            qaug_ref[lo:lo + HEAD_DIM, :] = qh
            qaug_ref[lo + HEAD_DIM:lo + HEAD_SLOT, :] = et[lo + HEAD_DIM:lo + HEAD_SLOT].astype(BF16)
        else:
            qaug_ref[lo:lo + HEAD_DIM, :] = et[lo:lo + HEAD_DIM].astype(BF16)
            qaug_ref[lo + HEAD_DIM:lo + HEAD_SLOT, :] = qh

    vtb = vt.astype(BF16)
    ones_rows = jnp.ones((V_SLOT - HEAD_DIM, tk), BF16)
    for s in range(T // tk):
        for h in range(N_HEADS):
            vt_ref[s, h * V_SLOT:h * V_SLOT + HEAD_DIM, :] = vtb[h * HEAD_DIM:(h + 1) * HEAD_DIM, s * tk:(s + 1) * tk]
            vt_ref[s, h * V_SLOT + HEAD_DIM:(h + 1) * V_SLOT, :] = ones_rows

    u = cg * xc
    row_u = lax.broadcasted_iota(jnp.int32, (T, CONV_W), 0)
    p1 = u_prev[7:8, :]
    p2 = u_prev[6:7, :]
    u1 = jnp.where(row_u == 0, p1, pltpu.roll(u, 1, axis=0))
    u2 = jnp.where(row_u == 0, p2, jnp.where(row_u == 1, p1, pltpu.roll(u, 2, axis=0)))
    zc = convw_ref[0:1, :] * u2 + convw_ref[1:2, :] * u1 + convw_ref[2:3, :] * u
    conv_ref[...] = bg * zc

    if emit_carry:
        cumrow_out[...] = cum_row
        cumt_out[...] = cum_t
        u_out[...] = u
    c_row[0:1, :] = cum_row[T - 1:T, :]
    c_col[...] = jnp.broadcast_to(cum_t[:, T - 1:T], (PART_ROWS, LANES))
    u_prev[...] = u[T - 8:T, :]


def _proj_call(xs, g, wrow, wt, bfr, bfc, convw, selk, selq, cin_row, cin_col, uin,
               *, t_rows, n_valid, emit_carry):
    bn, ln, _ = xs.shape
    T = t_rows
    tk = min(ATT_T, T)
    nt = ln // T
    const2 = lambda b, t: (0, 0)
    in_specs = [
        pl.BlockSpec((None, T, D_MODEL), lambda b, t: (b, t, 0)),
        pl.BlockSpec(g.shape, const2),
        pl.BlockSpec(wrow.shape, const2),
        pl.BlockSpec(wt.shape, const2),
        pl.BlockSpec(bfr.shape, const2),
        pl.BlockSpec(bfc.shape, const2),
        pl.BlockSpec(convw.shape, const2),
        pl.BlockSpec(selk.shape, const2),
        pl.BlockSpec(selq.shape, const2),
        pl.BlockSpec(cin_row.shape, const2),
        pl.BlockSpec(cin_col.shape, const2),
        pl.BlockSpec(uin.shape, const2),
    ]
    out_shape = [
        jax.ShapeDtypeStruct((bn, ln, AUG_W), BF16),
        jax.ShapeDtypeStruct((bn, AUG_W, ln), BF16),
        jax.ShapeDtypeStruct((bn, ln // tk, V_AUG, tk), BF16),
        jax.ShapeDtypeStruct((bn, ln, CONV_W), F32),
    ]
    out_specs = [
        pl.BlockSpec((None, T, AUG_W), lambda b, t: (b, t, 0)),
        pl.BlockSpec((None, AUG_W, T), lambda b, t: (b, 0, t)),
        pl.BlockSpec((None, T // tk, V_AUG, tk), lambda b, t: (b, t, 0, 0)),
        pl.BlockSpec((None, T, CONV_W), lambda b, t: (b, t, 0)),
    ]
    if emit_carry:
        out_shape += [
            jax.ShapeDtypeStruct((bn, ln, LANES), F32),
            jax.ShapeDtypeStruct((bn, PART_ROWS, ln), F32),
            jax.ShapeDtypeStruct((bn, ln, CONV_W), F32),
        ]
        out_specs += [
            pl.BlockSpec((None, T, LANES), lambda b, t: (b, t, 0)),
            pl.BlockSpec((None, PART_ROWS, T), lambda b, t: (b, 0, t)),
            pl.BlockSpec((None, T, CONV_W), lambda b, t: (b, t, 0)),
        ]
    kern = functools.partial(_proj_kernel, t_rows=T, n_valid=n_valid, emit_carry=emit_carry, tk=tk)
    return pl.pallas_call(
        kern,
        grid=(bn, nt),
        in_specs=in_specs,
        out_specs=out_specs,
        out_shape=out_shape,
        scratch_shapes=[pltpu.VMEM((8, LANES), F32), pltpu.VMEM((PART_ROWS, LANES), F32),
                        pltpu.VMEM((8, CONV_W), F32)],
        compiler_params=pltpu.CompilerParams(dimension_semantics=("arbitrary", "arbitrary"),
                                             vmem_limit_bytes=VMEM_LIMIT),
        name="proj_meta" if emit_carry else "proj",
    )(xs, g, wrow, wt, bfr, bfc, convw, selk, selq, cin_row, cin_col, uin)


def _attn_kernel(q_ref, k_ref, v_ref, km_ref, vm_ref, g_ref, o_ref, s_s, m_s, acc_s):
    i = pl.program_id(1)
    T = ATT_T
    causal = (lax.broadcasted_iota(jnp.int32, (T, T), 0) <= lax.broadcasted_iota(jnp.int32, (T, T), 1))

    def scores(h, kblk, slot, masked):
        qa = q_ref[h * HEAD_SLOT:(h + 1) * HEAD_SLOT, :]
        s = jnp.dot(kblk, qa, preferred_element_type=F32)
        if masked:
            s = jnp.where(causal, s, NEG_BIG)
        s_s[slot, h] = s

    def consume(h, vblk, slot):
        s = s_s[slot, h]
        m_old = m_s[h, 0:1, :]
        m_new = jnp.maximum(m_old, jnp.max(s, axis=0, keepdims=True))
        alpha = jnp.exp2(m_old - m_new)
        pm = jnp.exp2(s - m_new).astype(BF16)
        acc_s[h] = alpha * acc_s[h] + jnp.dot(vblk, pm, preferred_element_type=F32)
        m_s[h, 0:1, :] = m_new

    def step(k_next, next_slot, masked, v_cur, cur_slot):
        ahead = 2
        for h in range(ahead):
            scores(h, k_next(h), next_slot, masked)
        for h in range(N_HEADS):
            consume(h, v_cur(h), cur_slot)
            if h + ahead < N_HEADS:
                scores(h + ahead, k_next(h + ahead), next_slot, masked)

    def k_of(j):
        row = pl.ds(pl.multiple_of(j * T, T), T)
        return lambda h: k_ref[row, h * HEAD_SLOT:(h + 1) * HEAD_SLOT]

    def v_of(j):
        return lambda h: v_ref[j, h * V_SLOT:(h + 1) * V_SLOT, :]

    k_meta = lambda h: km_ref[:, h * HEAD_SLOT:(h + 1) * HEAD_SLOT]
    v_meta = lambda h: vm_ref[h * V_SLOT:(h + 1) * V_SLOT, :]

    m_s[...] = jnp.full(m_s.shape, NEG_BIG, F32)
    acc_s[...] = jnp.zeros(acc_s.shape, F32)
    for h in range(N_HEADS):
        scores(h, k_meta(h), 0, False)

    def v_at(t):
        return lambda h: jnp.where(t == 0, v_meta(h), v_ref[jnp.maximum(t - 1, 0), h * V_SLOT:(h + 1) * V_SLOT, :])

    def pair_body(p, c):
        t0 = 2 * p
        step(k_of(t0), 1, False, v_at(t0), 0)
        step(k_of(t0 + 1), 0, False, v_at(t0 + 1), 1)
        return c

    lax.fori_loop(0, i // 2, pair_body, 0)

    @pl.when(i % 2 == 1)
    def _():
        step(k_of(i - 1), 1, False, v_at(i - 1), 0)
        step(k_of(i), 0, True, v_at(i), 1)
        for h in range(N_HEADS):
            consume(h, v_of(i)(h), 0)

    @pl.when(i % 2 == 0)
    def _():
        step(k_of(i), 1, True, v_at(i), 0)
        for h in range(N_HEADS):
            consume(h, v_of(i)(h), 1)

    for p in range(N_HEADS // 2):
        halves = []
        for h in (2 * p, 2 * p + 1):
            o = acc_s[h, 0:HEAD_DIM, :] * (1.0 / acc_s[h, HEAD_DIM:HEAD_DIM + 1, :])
            halves.append(o * lax.rsqrt(jnp.mean(o * o, axis=0, keepdims=True) + EPS))
        pair = jnp.concatenate(halves, axis=0)
        o_ref[:, p * LANES:(p + 1) * LANES] = (pair.T * g_ref[:, p * LANES:(p + 1) * LANES]).astype(o_ref.dtype)


def _attn_call(qaug, kaug, vt, kmeta, vmeta, gain):
    bn, _, ln = qaug.shape
    T = ATT_T
    nq = ln // T
    return pl.pallas_call(
        _attn_kernel,
        grid=(bn, nq),
        in_specs=[
            pl.BlockSpec((None, AUG_W, T), lambda b, i: (b, 0, i)),
            pl.BlockSpec((None, ln, AUG_W), lambda b, i: (b, 0, 0)),
            pl.BlockSpec((None, ln // T, V_AUG, T), lambda b, i: (b, 0, 0, 0)),
            pl.BlockSpec(kmeta.shape, lambda b, i: (0, 0)),
            pl.BlockSpec(vmeta.shape, lambda b, i: (0, 0)),
            pl.BlockSpec(gain.shape, lambda b, i: (0, 0)),
        ],
        out_specs=pl.BlockSpec((None, T, ATTN_W), lambda b, i: (b, i, 0)),
        out_shape=jax.ShapeDtypeStruct((bn, ln, ATTN_W), BF16),
        scratch_shapes=[pltpu.VMEM((2, N_HEADS, T, T), F32),
                        pltpu.VMEM((N_HEADS, 8, T), F32),
                        pltpu.VMEM((N_HEADS, V_SLOT, T), F32)],
        compiler_params=pltpu.CompilerParams(dimension_semantics=("arbitrary", "arbitrary"),
                                             vmem_limit_bytes=VMEM_LIMIT),
        name="attention",
    )(qaug, kaug, vt, kmeta, vmeta, gain)


def _mix_kernel(attn_ref, conv_ref, x_ref, gc_ref, gsum_ref, woa_ref, woc_ref, fg_ref,
                wrh_ref, wrl_ref, br_ref, h_ref, z_ref, rinfo_ref, counts_ref, cnt):
    T = MIX_T

    @pl.when(pl.program_id(0) == 0)
    def _():
        cnt[...] = jnp.zeros_like(cnt)

    c = conv_ref[...]
    c2 = c * c
    c2h = c2.astype(BF16)
    c2l = (c2 - c2h.astype(F32)).astype(BF16)
    ss = (jnp.dot(c2h, gsum_ref[...], preferred_element_type=F32)
          + jnp.dot(c2l, gsum_ref[...], preferred_element_type=F32))
    cn = (c * lax.rsqrt(ss * (1.0 / HEAD_DIM) + EPS) * gc_ref[...]).astype(BF16)
    hadd = (jnp.dot(attn_ref[...], woa_ref[...], preferred_element_type=F32)
            + jnp.dot(cn, woc_ref[...], preferred_element_type=F32))
    h = x_ref[...] + hadd
    h_ref[...] = h
    z = h * lax.rsqrt(jnp.mean(h * h, axis=-1, keepdims=True) + EPS) * fg_ref[...]
    z_ref[...] = z
    zh = z.astype(BF16)
    zl = (z - zh.astype(F32)).astype(BF16)
    logits = (jnp.dot(zh, wrh_ref[...], preferred_element_type=F32)
              + jnp.dot(zl, wrh_ref[...], preferred_element_type=F32)
              + jnp.dot(zh, wrl_ref[...], preferred_element_type=F32)) + br_ref[...]

    lane = lax.broadcasted_iota(jnp.int32, (T, LANES), 1).astype(F32)
    big = float(LANES)
    gl = jnp.where(lane < N_GROUPS, logits, -jnp.inf)
    gmax = jnp.max(gl, axis=-1, keepdims=True)
    gsum = jnp.sum(jnp.exp(gl - gmax), axis=-1, keepdims=True)
    g_w = 1.0 / gsum
    g_idx = jnp.min(jnp.where(gl == gmax, lane, big), axis=-1, keepdims=True)
    e_lo = ROUTER_LANE0 + EXPERTS_PER_GROUP * g_idx
    emask = (lane >= e_lo) & (lane < e_lo + EXPERTS_PER_GROUP)
    el = jnp.where(emask, logits, -jnp.inf)
    emax = jnp.max(el, axis=-1, keepdims=True)
    eexp = jnp.exp(el - emax)
    probs = eexp / jnp.sum(eexp, axis=-1, keepdims=True)
    probs = jnp.where(emask, probs, -1.0)
    p1 = jnp.max(probs, axis=-1, keepdims=True)
    i1 = jnp.min(jnp.where(probs == p1, lane, big), axis=-1, keepdims=True)
    probs2 = jnp.where(lane == i1, -1.0, probs)
    p2 = jnp.max(probs2, axis=-1, keepdims=True)
    i2 = jnp.min(jnp.where(probs2 == p2, lane, big), axis=-1, keepdims=True)
    psum = p1 + p2
    w1 = g_w * (p1 / psum)
    w2 = g_w * (p2 / psum)

    hit = ((lane == i1) | (lane == i2)).astype(F32)
    tri_r = lax.broadcasted_iota(jnp.int32, (T, T), 0)
    tri_c = lax.broadcasted_iota(jnp.int32, (T, T), 1)
    strict_l = (tri_c < tri_r).astype(BF16)
    before = jnp.dot(strict_l, hit.astype(BF16), preferred_element_type=F32) + cnt[0:1, :]
    r1 = jnp.sum(jnp.where(lane == i1, before, 0.0), axis=-1, keepdims=True)
    r2 = jnp.sum(jnp.where(lane == i2, before, 0.0), axis=-1, keepdims=True)
    cnt[0:1, :] = cnt[0:1, :] + jnp.sum(hit, axis=0, keepdims=True)
    counts_ref[...] = cnt[...]

    e1 = i1 - ROUTER_LANE0
    e2 = i2 - ROUTER_LANE0
    rinfo = jnp.where(lane == 0, e1, jnp.where(lane == 1, e2, jnp.where(lane == 2, w1, jnp.where(
        lane == 3, w2, jnp.where(lane == 4, r1, jnp.where(lane == 5, r2, 0.0))))))
    rinfo_ref[...] = rinfo


def _mix_call(attn, conv, x, gc, gsum, woa, woc, fg, wrh, wrl, br):
    n = x.shape[0]
    T = MIX_T
    const = lambda i: (0, 0)
    row = lambda w: pl.BlockSpec((T, w), lambda i: (i, 0))
    return pl.pallas_call(
        _mix_kernel,
        grid=(n // T,),
        in_specs=[row(ATTN_W), row(CONV_W), row(D_MODEL)] + [
            pl.BlockSpec(a.shape, const) for a in (gc, gsum, woa, woc, fg, wrh, wrl, br)],
        out_specs=[row(D_MODEL), row(D_MODEL), row(LANES), pl.BlockSpec((8, LANES), const)],
        out_shape=[jax.ShapeDtypeStruct((n, D_MODEL), F32), jax.ShapeDtypeStruct((n, D_MODEL), F32),
                   jax.ShapeDtypeStruct((n, LANES), F32), jax.ShapeDtypeStruct((8, LANES), F32)],
        scratch_shapes=[pltpu.VMEM((8, LANES), F32)],
        compiler_params=pltpu.CompilerParams(dimension_semantics=("arbitrary",),
                                             vmem_limit_bytes=VMEM_LIMIT),
        name="mix",
    )(attn, conv, x, gc, gsum, woa, woc, fg, wrh, wrl, br)


def _dispatch_kernel(pad_end_ref, cnt_ref, dest_ref, z_ref, xbuf_ref, zero_v, sem, zsem):
    T = DISPATCH_T

    @pl.when(pl.program_id(0) == 0)
    def _():
        zero_v[...] = jnp.zeros_like(zero_v)

        def zero_copy(e):
            first = pl.multiple_of(jnp.maximum(pad_end_ref[e] - EXPERT_BM, 0), EXPERT_BM)
            return pltpu.make_async_copy(zero_v, xbuf_ref.at[pl.ds(first, EXPERT_BM)], zsem)

        for e in range(N_EXPERTS):
            @pl.when(cnt_ref[e] > 0)
            def _():
                zero_copy(e).start()
        for e in range(N_EXPERTS):
            @pl.when(cnt_ref[e] > 0)
            def _():
                zero_copy(e).wait()

        def tail_copy(b):
            return pltpu.make_async_copy(
                zero_v, xbuf_ref.at[pl.ds(pl.multiple_of(b * EXPERT_BM, EXPERT_BM), EXPERT_BM)], zsem)

        n_used = pad_end_ref[N_EXPERTS - 1] // EXPERT_BM
        n_all = xbuf_ref.shape[0] // EXPERT_BM
        lax.fori_loop(n_used, n_all, lambda b, c: (tail_copy(b).start(), c)[1], 0)
        lax.fori_loop(n_used, n_all, lambda b, c: (tail_copy(b).wait(), c)[1], 0)

    def row_copy(r, slot):
        d = dest_ref[0, 2 * r + slot]
        return pltpu.make_async_copy(z_ref.at[pl.ds(r, 1)], xbuf_ref.at[pl.ds(d, 1)], sem)

    def start(r, c):
        row_copy(r, 0).start()
        row_copy(r, 1).start()
        return c

    def wait(r, c):
        row_copy(r, 0).wait()
        row_copy(r, 1).wait()
        return c

    lax.fori_loop(0, T, start, 0, unroll=DMA_UNROLL)
    lax.fori_loop(0, T, wait, 0, unroll=DMA_UNROLL)


def _dispatch_call(pad_end, cnt, dest, z, n_rows):
    n = z.shape[0]
    T = DISPATCH_T
    grid_spec = pltpu.PrefetchScalarGridSpec(
        num_scalar_prefetch=2,
        grid=(n // T,),
        in_specs=[
            pl.BlockSpec((None, 1, 2 * T), lambda i, pe, ct: (i, 0, 0), memory_space=pltpu.SMEM),
            pl.BlockSpec((T, D_MODEL), lambda i, pe, ct: (i, 0)),
        ],
        out_specs=pl.BlockSpec(memory_space=pl.ANY),
        scratch_shapes=[pltpu.VMEM((EXPERT_BM, D_MODEL), F32), pltpu.SemaphoreType.DMA(()),
                        pltpu.SemaphoreType.DMA(())],
    )
    return pl.pallas_call(
        _dispatch_kernel,
        grid_spec=grid_spec,
        out_shape=jax.ShapeDtypeStruct((n_rows, D_MODEL), F32),
        compiler_params=pltpu.CompilerParams(dimension_semantics=("arbitrary",)),
        name="dispatch",
    )(pad_end, cnt, dest, z)


def _expert_kernel(blk_e_ref, nvb_ref, x_ref, wg_ref, wu_ref, wd_ref, y_ref, wgu_s, wd_s):
    blk = pl.program_id(0)
    valid = blk < nvb_ref[0]
    prev_e = blk_e_ref[jnp.maximum(blk - 1, 0)]

    @pl.when(valid & ((blk == 0) | (blk_e_ref[blk] != prev_e)))
    def _():
        wgu_s[:, 0:D_EXPERT] = wg_ref[...].astype(BF16)
        wgu_s[:, D_EXPERT:2 * D_EXPERT] = wu_ref[...].astype(BF16)
        wd_s[...] = wd_ref[...].astype(BF16)

    @pl.when(valid)
    def _():
        xb = x_ref[...].astype(BF16)
        gu = jnp.dot(xb, wgu_s[...], preferred_element_type=F32)
        g = gu[:, 0:D_EXPERT]
        u = gu[:, D_EXPERT:2 * D_EXPERT]
        a = (g * jax.nn.sigmoid(g)) * u
        y_ref[...] = jnp.dot(a.astype(BF16), wd_s[...], preferred_element_type=F32)


def _expert_call(blk_e, nvb, xbuf, w_gate, w_up, w_down):
    rows = xbuf.shape[0]
    bm = EXPERT_BM
    used = lambda i, be, nv: (jnp.minimum(i, nv[0] - 1), 0)
    grid_spec = pltpu.PrefetchScalarGridSpec(
        num_scalar_prefetch=2,
        grid=(rows // bm,),
        in_specs=[
            pl.BlockSpec((bm, D_MODEL), used),
            pl.BlockSpec((None, D_MODEL, D_EXPERT), lambda i, be, nv: (be[i], 0, 0)),
            pl.BlockSpec((None, D_MODEL, D_EXPERT), lambda i, be, nv: (be[i], 0, 0)),
            pl.BlockSpec((None, D_EXPERT, D_MODEL), lambda i, be, nv: (be[i], 0, 0)),
        ],
        out_specs=pl.BlockSpec((bm, D_MODEL), used),
        scratch_shapes=[pltpu.VMEM((D_MODEL, 2 * D_EXPERT), BF16), pltpu.VMEM((D_EXPERT, D_MODEL), BF16)],
    )
    return pl.pallas_call(
        _expert_kernel,
        grid_spec=grid_spec,
        out_shape=jax.ShapeDtypeStruct((rows, D_MODEL), F32),
        input_output_aliases={2: 0},
        compiler_params=pltpu.CompilerParams(dimension_semantics=("arbitrary",),
                                             vmem_limit_bytes=VMEM_LIMIT),
        name="experts",
    )(blk_e, nvb, xbuf, w_gate, w_up, w_down)


def _combine_kernel(dest_ref, dest_next_ref, h_ref, rinfo_ref, fn_ref, ybuf_ref, o_ref, ybuf_v, sem):
    T = COMBINE_T
    i = pl.program_id(0)
    cur = i % 2

    def row_copy(dref, buf, r, k):
        d = dref[0, 2 * r + k]
        return pltpu.make_async_copy(ybuf_ref.at[pl.ds(d, 1)], ybuf_v.at[buf, k, pl.ds(r, 1)], sem.at[buf])

    def issue(dref, buf):
        def start(r, c):
            row_copy(dref, buf, r, 0).start()
            row_copy(dref, buf, r, 1).start()
            return c
        lax.fori_loop(0, T, start, 0, unroll=DMA_UNROLL)

    @pl.when(i == 0)
    def _():
        issue(dest_ref, 0)

    @pl.when(i + 1 < pl.num_programs(0))
    def _():
        issue(dest_next_ref, 1 - cur)

    def wait(r, c):
        row_copy(dest_ref, cur, r, 0).wait()
        row_copy(dest_ref, cur, r, 1).wait()
        return c

    lax.fori_loop(0, T, wait, 0, unroll=DMA_UNROLL)
    rinfo = rinfo_ref[...]
    w1 = rinfo[:, 2:3]
    w2 = rinfo[:, 3:4]
    h = h_ref[...] + (ybuf_v[cur, 0] * w1 + ybuf_v[cur, 1] * w2)
    o_ref[...] = h * lax.rsqrt(jnp.mean(h * h, axis=-1, keepdims=True) + EPS) * fn_ref[...]


def _combine_call(dest, h, rinfo, fn, ybuf):
    n = h.shape[0]
    T = COMBINE_T
    nt = n // T
    return pl.pallas_call(
        _combine_kernel,
        grid=(nt,),
        in_specs=[
            pl.BlockSpec((None, 1, 2 * T), lambda i: (i, 0, 0), memory_space=pltpu.SMEM),
            pl.BlockSpec((None, 1, 2 * T), lambda i: (jnp.minimum(i + 1, nt - 1), 0, 0), memory_space=pltpu.SMEM),
            pl.BlockSpec((T, D_MODEL), lambda i: (i, 0)),
            pl.BlockSpec((T, LANES), lambda i: (i, 0)),
            pl.BlockSpec(fn.shape, lambda i: (0, 0)),
            pl.BlockSpec(memory_space=pl.ANY),
        ],
        out_specs=pl.BlockSpec((T, D_MODEL), lambda i: (i, 0)),
        out_shape=jax.ShapeDtypeStruct((n, D_MODEL), F32),
        scratch_shapes=[pltpu.VMEM((2, 2, T, D_MODEL), F32), pltpu.SemaphoreType.DMA((2,))],
        compiler_params=pltpu.CompilerParams(dimension_semantics=("arbitrary",),
                                             vmem_limit_bytes=VMEM_LIMIT),
        name="combine",
    )(dest, dest, h, rinfo, fn, ybuf)


def kernel(x, meta_tokens, attn_norm, w_in, b_forget, conv_w, mix_norm, w_out, ffn_norm,
           w_router_group, b_router_group, w_router_expert, b_router_expert,
           w_gate, w_up, w_down, final_norm):
    bsz, seq, d = x.shape
    assert d == D_MODEL and seq % PROJ_T == 0 and meta_tokens.shape[0] == N_META
    assert attn_norm.shape[0] == 1, "single-layer block"
    n_tok = bsz * seq

    wi = w_in[0]
    o = 0
    w_q = wi[:, o:o + ATTN_W]; o += ATTN_W
    w_k = wi[:, o:o + ATTN_W]; o += ATTN_W
    w_v = wi[:, o:o + ATTN_W]; o += ATTN_W
    w_f = wi[:, o:o + N_HEADS]; o += N_HEADS
    w_xc = wi[:, o:o + CONV_W]; o += CONV_W
    w_bg = wi[:, o:o + CONV_W]; o += CONV_W
    w_cg = wi[:, o:o + CONV_W]
    wrow = jnp.concatenate([w_k, w_xc, w_bg, w_cg, w_f, jnp.zeros((D_MODEL, LANES - N_HEADS), F32)],
                           axis=1).astype(BF16)
    wt = jnp.concatenate([w_q.T * (HEAD_DIM ** -0.5), w_v.T, w_f.T,
                          jnp.zeros((PART_ROWS - N_HEADS, D_MODEL), F32)], axis=0).astype(BF16)
    g_attn = attn_norm[0].reshape(1, D_MODEL)
    bfr = jnp.zeros((1, LANES), F32).at[0, :N_HEADS].set(b_forget[0])
    bfc = jnp.zeros((PART_ROWS, LANES), F32).at[:N_HEADS, :].set(b_forget[0][:, None])
    convw = jnp.zeros((8, CONV_W), F32).at[:3].set(conv_w[0])
    selk, selq = _selection_matrices()

    meta_pad = jnp.zeros((1, META_PAD, D_MODEL), F32).at[0, :N_META].set(meta_tokens)
    zrow = jnp.zeros((8, LANES), F32)
    zcol = jnp.zeros((PART_ROWS, LANES), F32)
    zu = jnp.zeros((8, CONV_W), F32)
    kmeta, _, vmeta, _, cumrow_m, cumt_m, u_m = _proj_call(
        meta_pad, g_attn, wrow, wt, bfr, bfc, convw, selk, selq, zrow, zcol, zu,
        t_rows=META_PAD, n_valid=N_META, emit_carry=True)
    cin_row = jnp.zeros((8, LANES), F32).at[0].set(cumrow_m[0, N_META - 1])
    cin_col = jnp.broadcast_to(cumt_m[0, :, N_META - 1:N_META], (PART_ROWS, LANES))
    uin = u_m[0, N_META - 8:N_META]

    kaug, qaug, vt, conv = _proj_call(
        x, g_attn, wrow, wt, bfr, bfc, convw, selk, selq, cin_row, cin_col, uin,
        t_rows=PROJ_T, n_valid=PROJ_T, emit_carry=False)

    mixg = mix_norm[0]
    attn = _attn_call(qaug, kaug, vt, kmeta[0], vmeta[0, 0], mixg[:ATTN_W].reshape(1, ATTN_W))

    gidx = np.arange(CONV_W) // HEAD_DIM
    gsum = jnp.asarray(gidx[:, None] == gidx[None, :], BF16)
    wo = w_out[0].astype(BF16)
    wr = jnp.zeros((D_MODEL, LANES), F32)
    wr = wr.at[:, :N_GROUPS].set(w_router_group[0]).at[:, ROUTER_LANE0:ROUTER_LANE0 + N_EXPERTS].set(
        w_router_expert[0])
    wrh = wr.astype(BF16)
    wrl = (wr - wrh.astype(F32)).astype(BF16)
    br = jnp.zeros((1, LANES), F32)
    br = br.at[0, :N_GROUPS].set(b_router_group[0]).at[0, ROUTER_LANE0:ROUTER_LANE0 + N_EXPERTS].set(
        b_router_expert[0])
    h, z, rinfo, counts = _mix_call(
        attn.reshape(n_tok, ATTN_W), conv.reshape(n_tok, CONV_W), x.reshape(n_tok, D_MODEL),
        mixg[ATTN_W:].reshape(1, CONV_W), gsum, wo[:ATTN_W], wo[ATTN_W:], ffn_norm[0].reshape(1, D_MODEL),
        wrh, wrl, br)

    cnt = counts[0, ROUTER_LANE0:ROUTER_LANE0 + N_EXPERTS].astype(jnp.int32)
    padded = (cnt + EXPERT_BM - 1) // EXPERT_BM * EXPERT_BM
    pad_end = jnp.cumsum(padded)
    pad_start = pad_end - padded
    eid = rinfo[:, 0:2].astype(jnp.int32)
    rank = rinfo[:, 4:6].astype(jnp.int32)
    expert_ids = jnp.arange(N_EXPERTS, dtype=jnp.int32)
    dest = rank + jnp.sum(jnp.where(eid[:, :, None] == expert_ids, pad_start, 0), axis=-1)
    dest = dest.reshape(n_tok // DISPATCH_T, 1, 2 * DISPATCH_T)
    n_blk = (2 * n_tok) // EXPERT_BM + N_EXPERTS
    nvb = (pad_end[-1] // EXPERT_BM).reshape(1)
    blk_row = jnp.minimum(jnp.arange(n_blk, dtype=jnp.int32), nvb - 1) * EXPERT_BM
    blk_e = jnp.sum(blk_row[:, None] >= pad_end[None, :], axis=1).astype(jnp.int32)

    xbuf = _dispatch_call(pad_end, cnt, dest, z, n_blk * EXPERT_BM)
    ybuf = _expert_call(blk_e, nvb, xbuf, w_gate[0], w_up[0], w_down[0])
    out = _combine_call(dest, h, rinfo, final_norm.reshape(1, D_MODEL), ybuf)
    return out.reshape(bsz, seq, D_MODEL)
```

```python
import functools

import numpy as np
import jax
import jax.numpy as jnp
from jax import lax
from jax.experimental import pallas as pl
from jax.experimental.pallas import tpu as pltpu

F32 = jnp.float32
BF16 = jnp.bfloat16

D_MODEL = 1024
HEAD_DIM = 64
N_HEADS = 8
ATTN_W = N_HEADS * HEAD_DIM
CONV_W = 512
N_META = 16
N_GROUPS = 4
EXPERTS_PER_GROUP = 8
N_EXPERTS = N_GROUPS * EXPERTS_PER_GROUP
D_EXPERT = 512
EPS = 1e-6
NEG_BIG = -1e30
LOG2E = 1.4426950408889634

LANES = 128
HEAD_SLOT = 2 * HEAD_DIM
V_SLOT = HEAD_DIM + 16
V_AUG = N_HEADS * V_SLOT
AUG_W = N_HEADS * HEAD_SLOT
N_PARTS = 3
PART_ROWS = 16
ONES_LANE = 8
ROUTER_LANE0 = N_GROUPS

PROJ_T = 512
ATT_T = 256
META_PAD = ATT_T
MIX_T = 512
DISPATCH_T = 256
EXPERT_BM = 256
COMBINE_T = 256
DMA_UNROLL = 8
VMEM_LIMIT = 48 * 1024 * 1024


def _split3(x):
    hi = x.astype(BF16)
    r1 = x - hi.astype(F32)
    mid = r1.astype(BF16)
    r2 = r1 - mid.astype(F32)
    return hi, mid, r2.astype(BF16)


def _log_sigmoid(x):
    return jnp.minimum(x, 0.0) - jnp.log1p(jnp.exp(-jnp.abs(x)))


def _head_offset(h):
    return HEAD_DIM if h % 2 == 0 else 0


def _bias_slot(j, h):
    return N_PARTS + N_HEADS * j + h


def _selection_matrix():
    selk = np.zeros((N_PARTS * LANES, 2 * LANES), np.float32)
    for base in (_head_offset(0), LANES + _head_offset(1)):
        for j in range(N_PARTS):
            selk[ONES_LANE, base + j] = 1.0
            for h in range(N_HEADS):
                selk[j * LANES + h, base + _bias_slot(j, h)] = 1.0
    return jnp.asarray(selk, BF16)


def _proj_kernel(x_ref, g_ref, wrow_ref, wt_ref, bfr_ref, bfc_ref, convw_ref, selk_ref,
                 cin_row_ref, cin_col_ref, uin_ref,
                 kaug_ref, qaug_ref, vt_ref, conv_ref, *rest, t_rows, n_valid, emit_carry, tk):
    if emit_carry:
        cumrow_out, cumt_out, u_out, c_row, c_col, u_prev = rest
    else:
        c_row, c_col, u_prev = rest

    @pl.when(pl.program_id(1) == 0)
    def _():
        c_row[...] = cin_row_ref[...]
        c_col[...] = cin_col_ref[...]
        u_prev[...] = uin_ref[...]

    T = t_rows
    x = x_ref[...]
    ms = jnp.mean(x * x, axis=-1, keepdims=True)
    z = x * lax.rsqrt(ms + EPS) * g_ref[...]
    zb = z.astype(BF16)
    r = jnp.dot(zb, wrow_ref[...], preferred_element_type=F32)
    tt = lax.dot_general(wt_ref[...], zb, (((1,), (1,)), ((), ())),
                         preferred_element_type=F32)
    k = r[:, 0:ATTN_W]
    xc = r[:, ATTN_W:ATTN_W + CONV_W]
    bg = r[:, ATTN_W + CONV_W:ATTN_W + 2 * CONV_W]
    cg = r[:, ATTN_W + 2 * CONV_W:ATTN_W + 3 * CONV_W]
    fr = r[:, ATTN_W + 3 * CONV_W:ATTN_W + 3 * CONV_W + LANES]
    qt = tt[0:ATTN_W]
    vt = tt[ATTN_W:2 * ATTN_W]
    ft = tt[2 * ATTN_W:2 * ATTN_W + PART_ROWS]

    lane_r = lax.broadcasted_iota(jnp.int32, (T, LANES), 1)
    row_r = lax.broadcasted_iota(jnp.int32, (T, LANES), 0)
    lfr = jnp.where(lane_r < N_HEADS, _log_sigmoid(fr + bfr_ref[...]), 0.0)
    lfr = lfr + jnp.where(row_r == 0, c_row[0:1, :], 0.0)
    tri_r = lax.broadcasted_iota(jnp.int32, (T, T), 0)
    tri_c = lax.broadcasted_iota(jnp.int32, (T, T), 1)
    tri_l = (tri_c <= tri_r).astype(BF16)
    tri_u = (tri_r <= tri_c).astype(BF16)
    c3 = jnp.dot(tri_l, jnp.concatenate(_split3(lfr), axis=1), preferred_element_type=F32)
    cum_row = c3[:, 0:LANES] + c3[:, LANES:2 * LANES] + c3[:, 2 * LANES:3 * LANES]

    kb = cum_row * (-LOG2E)
    if n_valid < T:
        kb = kb + jnp.where(row_r >= n_valid, NEG_BIG, 0.0)
    kb = jnp.where(lane_r == ONES_LANE, 1.0, kb)
    e = jnp.dot(jnp.concatenate(_split3(kb), axis=1), selk_ref[...],
                preferred_element_type=F32)
    e_even = e[:, 0:LANES].astype(BF16)
    e_odd = e[:, LANES:2 * LANES].astype(BF16)
    for p in range(N_HEADS // 2):
        kp = k[:, p * LANES:(p + 1) * LANES].astype(BF16)
        kaug_ref[:, 2 * p * LANES:(2 * p + 1) * LANES] = jnp.where(lane_r < HEAD_DIM, kp, e_even)
        kaug_ref[:, (2 * p + 1) * LANES:(2 * p + 2) * LANES] = jnp.where(lane_r >= HEAD_DIM, kp, e_odd)

    lane_c = lax.broadcasted_iota(jnp.int32, (PART_ROWS, T), 1)
    row_c = lax.broadcasted_iota(jnp.int32, (PART_ROWS, T), 0)
    lfc = jnp.where(row_c < N_HEADS, _log_sigmoid(ft + bfc_ref[:, 0:1]), 0.0)
    lfc = lfc + jnp.where(lane_c == 0, c_col[:, 0:1], 0.0)
    c3c = jnp.dot(jnp.concatenate(_split3(lfc), axis=0), tri_u, preferred_element_type=F32)
    cum_t = c3c[0:PART_ROWS] + c3c[PART_ROWS:2 * PART_ROWS] + c3c[2 * PART_ROWS:3 * PART_ROWS]
    cq_parts = [p.astype(F32) for p in _split3(cum_t * LOG2E)]
    bias_rows = HEAD_DIM // 2
    row_b = lax.broadcasted_iota(jnp.int32, (bias_rows, T), 0)
    zero_rows = jnp.zeros((HEAD_DIM - bias_rows, T), BF16)
    for h in range(N_HEADS):
        qh = (qt[h * HEAD_DIM:(h + 1) * HEAD_DIM] * LOG2E).astype(BF16)
        own = (row_b == _bias_slot(0, h)) | (row_b == _bias_slot(1, h)) | (row_b == _bias_slot(2, h))
        bias = jnp.where(own, 1.0, 0.0)
        for j in range(N_PARTS):
            bias = jnp.where(row_b == j, cq_parts[j][h:h + 1, :], bias)
        lo = h * HEAD_SLOT + _head_offset(h)
        qaug_ref[lo:lo + bias_rows, :] = bias.astype(BF16)
        qaug_ref[lo + bias_rows:lo + HEAD_DIM, :] = zero_rows
        lo_q = h * HEAD_SLOT + (HEAD_DIM - _head_offset(h))
        qaug_ref[lo_q:lo_q + HEAD_DIM, :] = qh

    vtb = vt.astype(BF16)
    ones_rows = jnp.ones((V_SLOT - HEAD_DIM, tk), BF16)
    for s in range(T // tk):
        for h in range(N_HEADS):
            vt_ref[s, h * V_SLOT:h * V_SLOT + HEAD_DIM, :] = vtb[h * HEAD_DIM:(h + 1) * HEAD_DIM, s * tk:(s + 1) * tk]
            vt_ref[s, h * V_SLOT + HEAD_DIM:(h + 1) * V_SLOT, :] = ones_rows

    u = cg * xc
    row_u = lax.broadcasted_iota(jnp.int32, (T, CONV_W), 0)
    p1 = u_prev[7:8, :]
    p2 = u_prev[6:7, :]
    u1 = jnp.where(row_u == 0, p1, pltpu.roll(u, 1, axis=0))
    u2 = jnp.where(row_u == 0, p2, jnp.where(row_u == 1, p1, pltpu.roll(u, 2, axis=0)))
    zc = convw_ref[0:1, :] * u2 + convw_ref[1:2, :] * u1 + convw_ref[2:3, :] * u
    conv_ref[...] = bg * zc

    if emit_carry:
        cumrow_out[...] = cum_row
        cumt_out[...] = cum_t
        u_out[...] = u
    c_row[0:1, :] = cum_row[T - 1:T, :]
    c_col[...] = jnp.broadcast_to(cum_t[:, T - 1:T], (PART_ROWS, LANES))
    u_prev[...] = u[T - 8:T, :]


def _proj_call(xs, g, wrow, wt, bfr, bfc, convw, selk, cin_row, cin_col, uin,
               *, t_rows, n_valid, emit_carry):
    bn, ln, _ = xs.shape
    T = t_rows
    tk = min(ATT_T, T)
    nt = ln // T
    const2 = lambda b, t: (0, 0)
    in_specs = [
        pl.BlockSpec((None, T, D_MODEL), lambda b, t: (b, t, 0)),
        pl.BlockSpec(g.shape, const2),
        pl.BlockSpec(wrow.shape, const2),
        pl.BlockSpec(wt.shape, const2),
        pl.BlockSpec(bfr.shape, const2),
        pl.BlockSpec(bfc.shape, const2),
        pl.BlockSpec(convw.shape, const2),
        pl.BlockSpec(selk.shape, const2),
        pl.BlockSpec(cin_row.shape, const2),
        pl.BlockSpec(cin_col.shape, const2),
        pl.BlockSpec(uin.shape, const2),
    ]
    out_shape = [
        jax.ShapeDtypeStruct((bn, ln, AUG_W), BF16),
        jax.ShapeDtypeStruct((bn, AUG_W, ln), BF16),
        jax.ShapeDtypeStruct((bn, ln // tk, V_AUG, tk), BF16),
        jax.ShapeDtypeStruct((bn, ln, CONV_W), F32),
    ]
    out_specs = [
        pl.BlockSpec((None, T, AUG_W), lambda b, t: (b, t, 0)),
        pl.BlockSpec((None, AUG_W, T), lambda b, t: (b, 0, t)),
        pl.BlockSpec((None, T // tk, V_AUG, tk), lambda b, t: (b, t, 0, 0)),
        pl.BlockSpec((None, T, CONV_W), lambda b, t: (b, t, 0)),
    ]
    if emit_carry:
        out_shape += [
            jax.ShapeDtypeStruct((bn, ln, LANES), F32),
            jax.ShapeDtypeStruct((bn, PART_ROWS, ln), F32),
            jax.ShapeDtypeStruct((bn, ln, CONV_W), F32),
        ]
        out_specs += [
            pl.BlockSpec((None, T, LANES), lambda b, t: (b, t, 0)),
            pl.BlockSpec((None, PART_ROWS, T), lambda b, t: (b, 0, t)),
            pl.BlockSpec((None, T, CONV_W), lambda b, t: (b, t, 0)),
        ]
    kern = functools.partial(_proj_kernel, t_rows=T, n_valid=n_valid, emit_carry=emit_carry, tk=tk)
    return pl.pallas_call(
        kern,
        grid=(bn, nt),
        in_specs=in_specs,
        out_specs=out_specs,
        out_shape=out_shape,
        scratch_shapes=[pltpu.VMEM((8, LANES), F32), pltpu.VMEM((PART_ROWS, LANES), F32),
                        pltpu.VMEM((8, CONV_W), F32)],
        compiler_params=pltpu.CompilerParams(dimension_semantics=("arbitrary", "arbitrary"),
                                             vmem_limit_bytes=VMEM_LIMIT),
        name="proj_meta" if emit_carry else "proj",
    )(xs, g, wrow, wt, bfr, bfc, convw, selk, cin_row, cin_col, uin)


def _attn_kernel(q_ref, k_ref, v_ref, km_ref, vm_ref, g_ref, o_ref, s_s, m_s, acc_s):
    i = pl.program_id(1)
    T = ATT_T
    causal = (lax.broadcasted_iota(jnp.int32, (T, T), 0) <= lax.broadcasted_iota(jnp.int32, (T, T), 1))

    def scores(h, kblk, slot, masked):
        qa = q_ref[h * HEAD_SLOT:(h + 1) * HEAD_SLOT, :]
        s = jnp.dot(kblk, qa, preferred_element_type=F32)
        if masked:
            s = jnp.where(causal, s, NEG_BIG)
        s_s[slot, h] = s

    def consume(h, vblk, slot):
        s = s_s[slot, h]
        m_old = m_s[h, 0:1, :]
        m_new = jnp.maximum(m_old, jnp.max(s, axis=0, keepdims=True))
        alpha = jnp.exp2(m_old - m_new)
        pm = jnp.exp2(s - m_new).astype(BF16)
        acc_s[h] = alpha * acc_s[h] + jnp.dot(vblk, pm, preferred_element_type=F32)
        m_s[h, 0:1, :] = m_new

    def step(k_next, next_slot, masked, v_cur, cur_slot):
        ahead = 2
        for h in range(ahead):
            scores(h, k_next(h), next_slot, masked)
        for h in range(N_HEADS):
            consume(h, v_cur(h), cur_slot)
            if h + ahead < N_HEADS:
                scores(h + ahead, k_next(h + ahead), next_slot, masked)

    def k_of(j):
        row = pl.ds(pl.multiple_of(j * T, T), T)
        return lambda h: k_ref[row, h * HEAD_SLOT:(h + 1) * HEAD_SLOT]

    def v_of(j):
        return lambda h: v_ref[j, h * V_SLOT:(h + 1) * V_SLOT, :]

    k_meta = lambda h: km_ref[:, h * HEAD_SLOT:(h + 1) * HEAD_SLOT]
    v_meta = lambda h: vm_ref[h * V_SLOT:(h + 1) * V_SLOT, :]

    m_s[...] = jnp.full(m_s.shape, NEG_BIG, F32)
    acc_s[...] = jnp.zeros(acc_s.shape, F32)
    for h in range(N_HEADS):
        scores(h, k_meta(h), 0, False)

    def v_at(t):
        return lambda h: jnp.where(t == 0, v_meta(h), v_ref[jnp.maximum(t - 1, 0), h * V_SLOT:(h + 1) * V_SLOT, :])

    def pair_body(p, c):
        t0 = 2 * p
        step(k_of(t0), 1, False, v_at(t0), 0)
        step(k_of(t0 + 1), 0, False, v_at(t0 + 1), 1)
        return c

    lax.fori_loop(0, i // 2, pair_body, 0)

    @pl.when(i % 2 == 1)
    def _():
        step(k_of(i - 1), 1, False, v_at(i - 1), 0)
        step(k_of(i), 0, True, v_at(i), 1)
        for h in range(N_HEADS):
            consume(h, v_of(i)(h), 0)

    @pl.when(i % 2 == 0)
    def _():
        step(k_of(i), 1, True, v_at(i), 0)
        for h in range(N_HEADS):
            consume(h, v_of(i)(h), 1)

    for p in range(N_HEADS // 2):
        halves = []
        for h in (2 * p, 2 * p + 1):
            o = acc_s[h, 0:HEAD_DIM, :] * (1.0 / acc_s[h, HEAD_DIM:HEAD_DIM + 1, :])
            halves.append(o * lax.rsqrt(jnp.mean(o * o, axis=0, keepdims=True) + EPS))
        pair = jnp.concatenate(halves, axis=0)
        o_ref[:, p * LANES:(p + 1) * LANES] = (pair.T * g_ref[:, p * LANES:(p + 1) * LANES]).astype(o_ref.dtype)


def _attn_call(qaug, kaug, vt, kmeta, vmeta, gain):
    bn, _, ln = qaug.shape
    T = ATT_T
    nq = ln // T
    return pl.pallas_call(
        _attn_kernel,
        grid=(bn, nq),
        in_specs=[
            pl.BlockSpec((None, AUG_W, T), lambda b, i: (b, 0, i)),
            pl.BlockSpec((None, ln, AUG_W), lambda b, i: (b, 0, 0)),
            pl.BlockSpec((None, ln // T, V_AUG, T), lambda b, i: (b, 0, 0, 0)),
            pl.BlockSpec(kmeta.shape, lambda b, i: (0, 0)),
            pl.BlockSpec(vmeta.shape, lambda b, i: (0, 0)),
            pl.BlockSpec(gain.shape, lambda b, i: (0, 0)),
        ],
        out_specs=pl.BlockSpec((None, T, ATTN_W), lambda b, i: (b, i, 0)),
        out_shape=jax.ShapeDtypeStruct((bn, ln, ATTN_W), BF16),
        scratch_shapes=[pltpu.VMEM((2, N_HEADS, T, T), F32),
                        pltpu.VMEM((N_HEADS, 8, T), F32),
                        pltpu.VMEM((N_HEADS, V_SLOT, T), F32)],
        compiler_params=pltpu.CompilerParams(dimension_semantics=("arbitrary", "arbitrary"),
                                             vmem_limit_bytes=VMEM_LIMIT),
        name="attention",
    )(qaug, kaug, vt, kmeta, vmeta, gain)


def _mix_kernel(attn_ref, conv_ref, x_ref, gc_ref, gsum_ref, woa_ref, woc_ref, fg_ref,
                wrh_ref, br_ref, h_ref, z_ref, rinfo_ref, counts_ref, cnt):
    T = MIX_T

    @pl.when(pl.program_id(0) == 0)
    def _():
        cnt[...] = jnp.zeros_like(cnt)

    c = conv_ref[...]
    c2 = c * c
    c2h = c2.astype(BF16)
    c2l = (c2 - c2h.astype(F32)).astype(BF16)
    ss = (jnp.dot(c2h, gsum_ref[...], preferred_element_type=F32)
          + jnp.dot(c2l, gsum_ref[...], preferred_element_type=F32))
    cn = (c * lax.rsqrt(ss * (1.0 / HEAD_DIM) + EPS) * gc_ref[...]).astype(BF16)
    hadd = (jnp.dot(attn_ref[...], woa_ref[...], preferred_element_type=F32)
            + jnp.dot(cn, woc_ref[...], preferred_element_type=F32))
    h = x_ref[...] + hadd
    h_ref[...] = h
    z = h * lax.rsqrt(jnp.mean(h * h, axis=-1, keepdims=True) + EPS) * fg_ref[...]
    z_ref[...] = z
    zh = z.astype(BF16)
    zl = (z - zh.astype(F32)).astype(BF16)
    both = jnp.dot(zh, wrh_ref[...], preferred_element_type=F32)
    logits = (both[:, 0:LANES] + both[:, LANES:2 * LANES]
              + jnp.dot(zl, wrh_ref[:, 0:LANES], preferred_element_type=F32)) + br_ref[...]

    lane = lax.broadcasted_iota(jnp.int32, (T, LANES), 1).astype(F32)
    big = float(LANES)
    gl = jnp.where(lane < N_GROUPS, logits, -jnp.inf)
    gmax = jnp.max(gl, axis=-1, keepdims=True)
    gsum = jnp.sum(jnp.exp(gl - gmax), axis=-1, keepdims=True)
    g_w = 1.0 / gsum
    g_idx = jnp.min(jnp.where(gl == gmax, lane, big), axis=-1, keepdims=True)
    e_lo = ROUTER_LANE0 + EXPERTS_PER_GROUP * g_idx
    emask = (lane >= e_lo) & (lane < e_lo + EXPERTS_PER_GROUP)
    el = jnp.where(emask, logits, -jnp.inf)
    emax = jnp.max(el, axis=-1, keepdims=True)
    eexp = jnp.exp(el - emax)
    probs = eexp / jnp.sum(eexp, axis=-1, keepdims=True)
    probs = jnp.where(emask, probs, -1.0)
    p1 = jnp.max(probs, axis=-1, keepdims=True)
    i1 = jnp.min(jnp.where(probs == p1, lane, big), axis=-1, keepdims=True)
    probs2 = jnp.where(lane == i1, -1.0, probs)
    p2 = jnp.max(probs2, axis=-1, keepdims=True)
    i2 = jnp.min(jnp.where(probs2 == p2, lane, big), axis=-1, keepdims=True)
    psum = p1 + p2
    w1 = g_w * (p1 / psum)
    w2 = g_w * (p2 / psum)

    hit = ((lane == i1) | (lane == i2)).astype(F32)
    tri_r = lax.broadcasted_iota(jnp.int32, (T, T), 0)
    tri_c = lax.broadcasted_iota(jnp.int32, (T, T), 1)
    strict_l = (tri_c < tri_r).astype(BF16)
    before = jnp.dot(strict_l, hit.astype(BF16), preferred_element_type=F32) + cnt[0:1, :]
    r1 = jnp.sum(jnp.where(lane == i1, before, 0.0), axis=-1, keepdims=True)
    r2 = jnp.sum(jnp.where(lane == i2, before, 0.0), axis=-1, keepdims=True)
    cnt[0:1, :] = cnt[0:1, :] + jnp.sum(hit, axis=0, keepdims=True)
    counts_ref[...] = cnt[...]

    e1 = i1 - ROUTER_LANE0
    e2 = i2 - ROUTER_LANE0
    rinfo = jnp.where(lane == 0, e1, jnp.where(lane == 1, e2, jnp.where(lane == 2, w1, jnp.where(
        lane == 3, w2, jnp.where(lane == 4, r1, jnp.where(lane == 5, r2, 0.0))))))
    rinfo_ref[...] = rinfo


def _mix_call(attn, conv, x, gc, gsum, woa, woc, fg, wrh, br):
    n = x.shape[0]
    T = MIX_T
    const = lambda i: (0, 0)
    row = lambda w: pl.BlockSpec((T, w), lambda i: (i, 0))
    return pl.pallas_call(
        _mix_kernel,
        grid=(n // T,),
        in_specs=[row(ATTN_W), row(CONV_W), row(D_MODEL)] + [
            pl.BlockSpec(a.shape, const) for a in (gc, gsum, woa, woc, fg, wrh, br)],
        out_specs=[row(D_MODEL), row(D_MODEL), row(LANES), pl.BlockSpec((8, LANES), const)],
        out_shape=[jax.ShapeDtypeStruct((n, D_MODEL), F32), jax.ShapeDtypeStruct((n, D_MODEL), F32),
                   jax.ShapeDtypeStruct((n, LANES), F32), jax.ShapeDtypeStruct((8, LANES), F32)],
        scratch_shapes=[pltpu.VMEM((8, LANES), F32)],
        compiler_params=pltpu.CompilerParams(dimension_semantics=("arbitrary",),
                                             vmem_limit_bytes=VMEM_LIMIT),
        name="mix",
    )(attn, conv, x, gc, gsum, woa, woc, fg, wrh, br)


def _dispatch_kernel(pad_end_ref, cnt_ref, dest_ref, z_ref, xbuf_ref, zero_v, sem, zsem):
    T = DISPATCH_T

    @pl.when(pl.program_id(0) == 0)
    def _():
        zero_v[...] = jnp.zeros_like(zero_v)

        def zero_copy(e):
            first = pl.multiple_of(jnp.maximum(pad_end_ref[e] - EXPERT_BM, 0), EXPERT_BM)
            return pltpu.make_async_copy(zero_v, xbuf_ref.at[pl.ds(first, EXPERT_BM)], zsem)

        for e in range(N_EXPERTS):
            @pl.when(cnt_ref[e] > 0)
            def _():
                zero_copy(e).start()
        for e in range(N_EXPERTS):
            @pl.when(cnt_ref[e] > 0)
            def _():
                zero_copy(e).wait()

        def tail_copy(b):
            return pltpu.make_async_copy(
                zero_v, xbuf_ref.at[pl.ds(pl.multiple_of(b * EXPERT_BM, EXPERT_BM), EXPERT_BM)], zsem)

        n_used = pad_end_ref[N_EXPERTS - 1] // EXPERT_BM
        n_all = xbuf_ref.shape[0] // EXPERT_BM
        lax.fori_loop(n_used, n_all, lambda b, c: (tail_copy(b).start(), c)[1], 0)
        lax.fori_loop(n_used, n_all, lambda b, c: (tail_copy(b).wait(), c)[1], 0)

    def row_copy(r, slot):
        d = dest_ref[0, 2 * r + slot]
        return pltpu.make_async_copy(z_ref.at[pl.ds(r, 1)], xbuf_ref.at[pl.ds(d, 1)], sem)

    def start(r, c):
        row_copy(r, 0).start(priority=0)
        row_copy(r, 1).start(priority=1)
        return c

    def wait(r, c):
        row_copy(r, 0).wait()
        row_copy(r, 1).wait()
        return c

    lax.fori_loop(0, T, start, 0, unroll=DMA_UNROLL)
    lax.fori_loop(0, T, wait, 0, unroll=DMA_UNROLL)


def _dispatch_call(pad_end, cnt, dest, z, n_rows):
    n = z.shape[0]
    T = DISPATCH_T
    grid_spec = pltpu.PrefetchScalarGridSpec(
        num_scalar_prefetch=2,
        grid=(n // T,),
        in_specs=[
            pl.BlockSpec((None, 1, 2 * T), lambda i, pe, ct: (i, 0, 0), memory_space=pltpu.SMEM),
            pl.BlockSpec((T, D_MODEL), lambda i, pe, ct: (i, 0)),
        ],
        out_specs=pl.BlockSpec(memory_space=pl.ANY),
        scratch_shapes=[pltpu.VMEM((EXPERT_BM, D_MODEL), F32), pltpu.SemaphoreType.DMA(()),
                        pltpu.SemaphoreType.DMA(())],
    )
    return pl.pallas_call(
        _dispatch_kernel,
        grid_spec=grid_spec,
        out_shape=jax.ShapeDtypeStruct((n_rows, D_MODEL), F32),
        compiler_params=pltpu.CompilerParams(dimension_semantics=("arbitrary",)),
        name="dispatch",
    )(pad_end, cnt, dest, z)


def _expert_kernel(blk_e_ref, nvb_ref, x_ref, wg_ref, wu_ref, wd_ref, y_ref, wgu_s, wd_s):
    blk = pl.program_id(0)
    valid = blk < nvb_ref[0]
    prev_e = blk_e_ref[jnp.maximum(blk - 1, 0)]

    @pl.when(valid & ((blk == 0) | (blk_e_ref[blk] != prev_e)))
    def _():
        wgu_s[:, 0:D_EXPERT] = wg_ref[...].astype(BF16)
        wgu_s[:, D_EXPERT:2 * D_EXPERT] = wu_ref[...].astype(BF16)
        wd_s[...] = wd_ref[...].astype(BF16)

    @pl.when(valid)
    def _():
        xb = x_ref[...].astype(BF16)
        gu = jnp.dot(xb, wgu_s[...], preferred_element_type=F32)
        g = gu[:, 0:D_EXPERT]
        u = gu[:, D_EXPERT:2 * D_EXPERT]
        a = (g * jax.nn.sigmoid(g)) * u
        y_ref[...] = jnp.dot(a.astype(BF16), wd_s[...], preferred_element_type=F32)


def _expert_call(blk_e, nvb, xbuf, w_gate, w_up, w_down):
    rows = xbuf.shape[0]
    bm = EXPERT_BM
    used = lambda i, be, nv: (jnp.minimum(i, nv[0] - 1), 0)
    grid_spec = pltpu.PrefetchScalarGridSpec(
        num_scalar_prefetch=2,
        grid=(rows // bm,),
        in_specs=[
            pl.BlockSpec((bm, D_MODEL), used),
            pl.BlockSpec((None, D_MODEL, D_EXPERT), lambda i, be, nv: (be[i], 0, 0)),
            pl.BlockSpec((None, D_MODEL, D_EXPERT), lambda i, be, nv: (be[i], 0, 0)),
            pl.BlockSpec((None, D_EXPERT, D_MODEL), lambda i, be, nv: (be[i], 0, 0)),
        ],
        out_specs=pl.BlockSpec((bm, D_MODEL), used),
        scratch_shapes=[pltpu.VMEM((D_MODEL, 2 * D_EXPERT), BF16), pltpu.VMEM((D_EXPERT, D_MODEL), BF16)],
    )
    return pl.pallas_call(
        _expert_kernel,
        grid_spec=grid_spec,
        out_shape=jax.ShapeDtypeStruct((rows, D_MODEL), F32),
        input_output_aliases={2: 0},
        compiler_params=pltpu.CompilerParams(dimension_semantics=("arbitrary",),
                                             vmem_limit_bytes=VMEM_LIMIT),
        name="experts",
    )(blk_e, nvb, xbuf, w_gate, w_up, w_down)


def _combine_kernel(dest_ref, dest_next_ref, h_ref, rinfo_ref, fn_ref, ybuf_ref, o_ref, ybuf_v, sem):
    T = COMBINE_T
    i = pl.program_id(0)
    cur = i % 2

    def row_copy(dref, buf, r, k):
        d = dref[0, 2 * r + k]
        return pltpu.make_async_copy(ybuf_ref.at[pl.ds(d, 1)], ybuf_v.at[buf, k, pl.ds(r, 1)], sem.at[buf])

    def issue(dref, buf):
        def start(r, c):
            row_copy(dref, buf, r, 0).start(priority=0)
            row_copy(dref, buf, r, 1).start(priority=1)
            return c
        lax.fori_loop(0, T, start, 0, unroll=DMA_UNROLL)

    @pl.when(i == 0)
    def _():
        issue(dest_ref, 0)

    @pl.when(i + 1 < pl.num_programs(0))
    def _():
        issue(dest_next_ref, 1 - cur)

    def wait(r, c):
        row_copy(dest_ref, cur, r, 0).wait()
        row_copy(dest_ref, cur, r, 1).wait()
        return c

    lax.fori_loop(0, T, wait, 0, unroll=DMA_UNROLL)
    rinfo = rinfo_ref[...]
    w1 = rinfo[:, 2:3]
    w2 = rinfo[:, 3:4]
    h = h_ref[...] + (ybuf_v[cur, 0] * w1 + ybuf_v[cur, 1] * w2)
    o_ref[...] = h * lax.rsqrt(jnp.mean(h * h, axis=-1, keepdims=True) + EPS) * fn_ref[...]


def _combine_call(dest, h, rinfo, fn, ybuf):
    n = h.shape[0]
    T = COMBINE_T
    nt = n // T
    return pl.pallas_call(
        _combine_kernel,
        grid=(nt,),
        in_specs=[
            pl.BlockSpec((None, 1, 2 * T), lambda i: (i, 0, 0), memory_space=pltpu.SMEM),
            pl.BlockSpec((None, 1, 2 * T), lambda i: (jnp.minimum(i + 1, nt - 1), 0, 0), memory_space=pltpu.SMEM),
            pl.BlockSpec((T, D_MODEL), lambda i: (i, 0)),
            pl.BlockSpec((T, LANES), lambda i: (i, 0)),
            pl.BlockSpec(fn.shape, lambda i: (0, 0)),
            pl.BlockSpec(memory_space=pl.ANY),
        ],
        out_specs=pl.BlockSpec((T, D_MODEL), lambda i: (i, 0)),
        out_shape=jax.ShapeDtypeStruct((n, D_MODEL), F32),
        scratch_shapes=[pltpu.VMEM((2, 2, T, D_MODEL), F32), pltpu.SemaphoreType.DMA((2,))],
        compiler_params=pltpu.CompilerParams(dimension_semantics=("arbitrary",),
                                             vmem_limit_bytes=VMEM_LIMIT),
        name="combine",
    )(dest, dest, h, rinfo, fn, ybuf)


def kernel(x, meta_tokens, attn_norm, w_in, b_forget, conv_w, mix_norm, w_out, ffn_norm,
           w_router_group, b_router_group, w_router_expert, b_router_expert,
           w_gate, w_up, w_down, final_norm):
    bsz, seq, d = x.shape
    assert d == D_MODEL and seq % PROJ_T == 0 and meta_tokens.shape[0] == N_META
    assert attn_norm.shape[0] == 1, "single-layer block"
    n_tok = bsz * seq

    wi = w_in[0]
    o = 0
    w_q = wi[:, o:o + ATTN_W]; o += ATTN_W
    w_k = wi[:, o:o + ATTN_W]; o += ATTN_W
    w_v = wi[:, o:o + ATTN_W]; o += ATTN_W
    w_f = wi[:, o:o + N_HEADS]; o += N_HEADS
    w_xc = wi[:, o:o + CONV_W]; o += CONV_W
    w_bg = wi[:, o:o + CONV_W]; o += CONV_W
    w_cg = wi[:, o:o + CONV_W]
    wrow = jnp.concatenate([w_k, w_xc, w_bg, w_cg, w_f, jnp.zeros((D_MODEL, LANES - N_HEADS), F32)],
                           axis=1).astype(BF16)
    wt = jnp.concatenate([w_q.T * (HEAD_DIM ** -0.5), w_v.T, w_f.T,
                          jnp.zeros((PART_ROWS - N_HEADS, D_MODEL), F32)], axis=0).astype(BF16)
    g_attn = attn_norm[0].reshape(1, D_MODEL)
    bfr = jnp.zeros((1, LANES), F32).at[0, :N_HEADS].set(b_forget[0])
    bfc = jnp.zeros((PART_ROWS, LANES), F32).at[:N_HEADS, :].set(b_forget[0][:, None])
    convw = jnp.zeros((8, CONV_W), F32).at[:3].set(conv_w[0])
    selk = _selection_matrix()

    meta_pad = jnp.zeros((1, META_PAD, D_MODEL), F32).at[0, :N_META].set(meta_tokens)
    zrow = jnp.zeros((8, LANES), F32)
    zcol = jnp.zeros((PART_ROWS, LANES), F32)
    zu = jnp.zeros((8, CONV_W), F32)
    kmeta, _, vmeta, _, cumrow_m, cumt_m, u_m = _proj_call(
        meta_pad, g_attn, wrow, wt, bfr, bfc, convw, selk, zrow, zcol, zu,
        t_rows=META_PAD, n_valid=N_META, emit_carry=True)
    cin_row = jnp.zeros((8, LANES), F32).at[0].set(cumrow_m[0, N_META - 1])
    cin_col = jnp.broadcast_to(cumt_m[0, :, N_META - 1:N_META], (PART_ROWS, LANES))
    uin = u_m[0, N_META - 8:N_META]

    kaug, qaug, vt, conv = _proj_call(
        x, g_attn, wrow, wt, bfr, bfc, convw, selk, cin_row, cin_col, uin,
        t_rows=PROJ_T, n_valid=PROJ_T, emit_carry=False)

    mixg = mix_norm[0]
    attn = _attn_call(qaug, kaug, vt, kmeta[0], vmeta[0, 0], mixg[:ATTN_W].reshape(1, ATTN_W))

    gidx = np.arange(CONV_W) // HEAD_DIM
    gsum = jnp.asarray(gidx[:, None] == gidx[None, :], BF16)
    wo = w_out[0].astype(BF16)
    wr = jnp.zeros((D_MODEL, LANES), F32)
    wr = wr.at[:, :N_GROUPS].set(w_router_group[0]).at[:, ROUTER_LANE0:ROUTER_LANE0 + N_EXPERTS].set(
        w_router_expert[0])
    wr_hi = wr.astype(BF16)
    wrh = jnp.concatenate([wr_hi, (wr - wr_hi.astype(F32)).astype(BF16)], axis=1)
    br = jnp.zeros((1, LANES), F32)
    br = br.at[0, :N_GROUPS].set(b_router_group[0]).at[0, ROUTER_LANE0:ROUTER_LANE0 + N_EXPERTS].set(
        b_router_expert[0])
    h, z, rinfo, counts = _mix_call(
        attn.reshape(n_tok, ATTN_W), conv.reshape(n_tok, CONV_W), x.reshape(n_tok, D_MODEL),
        mixg[ATTN_W:].reshape(1, CONV_W), gsum, wo[:ATTN_W], wo[ATTN_W:], ffn_norm[0].reshape(1, D_MODEL),
        wrh, br)

    cnt = counts[0, ROUTER_LANE0:ROUTER_LANE0 + N_EXPERTS].astype(jnp.int32)
    padded = (cnt + EXPERT_BM - 1) // EXPERT_BM * EXPERT_BM
    pad_end = jnp.cumsum(padded)
    pad_start = pad_end - padded
    eid = rinfo[:, 0:2].astype(jnp.int32)
    rank = rinfo[:, 4:6].astype(jnp.int32)
    expert_ids = jnp.arange(N_EXPERTS, dtype=jnp.int32)
    dest = rank + jnp.sum(jnp.where(eid[:, :, None] == expert_ids, pad_start, 0), axis=-1)
    dest = dest.reshape(n_tok // DISPATCH_T, 1, 2 * DISPATCH_T)
    n_blk = (2 * n_tok) // EXPERT_BM + N_EXPERTS
    nvb = (pad_end[-1] // EXPERT_BM).reshape(1)
    blk_row = jnp.minimum(jnp.arange(n_blk, dtype=jnp.int32), nvb - 1) * EXPERT_BM
    blk_e = jnp.sum(blk_row[:, None] >= pad_end[None, :], axis=1).astype(jnp.int32)

    xbuf = _dispatch_call(pad_end, cnt, dest, z, n_blk * EXPERT_BM)
    ybuf = _expert_call(blk_e, nvb, xbuf, w_gate[0], w_up[0], w_down[0])
    out = _combine_call(dest, h, rinfo, final_norm.reshape(1, D_MODEL), ybuf)
    return out.reshape(bsz, seq, D_MODEL)
```

```python
import functools

import numpy as np
import jax
import jax.numpy as jnp
from jax import lax
from jax.experimental import pallas as pl
from jax.experimental.pallas import tpu as pltpu

F32 = jnp.float32
BF16 = jnp.bfloat16

D_MODEL = 1024
HEAD_DIM = 64
N_HEADS = 8
ATTN_W = N_HEADS * HEAD_DIM
CONV_W = 512
N_META = 16
N_GROUPS = 4
EXPERTS_PER_GROUP = 8
N_EXPERTS = N_GROUPS * EXPERTS_PER_GROUP
D_EXPERT = 512
EPS = 1e-6
NEG_BIG = -1e30
LOG2E = 1.4426950408889634

LANES = 128
HEAD_SLOT = 2 * HEAD_DIM
V_SLOT = HEAD_DIM + 16
V_AUG = N_HEADS * V_SLOT
AUG_W = N_HEADS * HEAD_SLOT
N_PARTS = 3
PART_ROWS = 16
ONES_LANE = 8
ROUTER_LANE0 = N_GROUPS

PROJ_T = 512
ATT_T = 256
META_PAD = ATT_T
MIX_T = 512
DISPATCH_T = 512
EXPERT_BM = 512
COMBINE_T = 512
DMA_UNROLL = 8
VMEM_LIMIT = 48 * 1024 * 1024


def _split3(x):
    hi = x.astype(BF16)
    r1 = x - hi.astype(F32)
    mid = r1.astype(BF16)
    r2 = r1 - mid.astype(F32)
    return hi, mid, r2.astype(BF16)


def _log_sigmoid(x):
    return jnp.minimum(x, 0.0) - jnp.log1p(jnp.exp(-jnp.abs(x)))


def _head_offset(h):
    return HEAD_DIM if h % 2 == 0 else 0


def _bias_slot(j, h):
    return N_PARTS + N_HEADS * j + h


def _selection_matrix():
    selk = np.zeros((N_PARTS * LANES, 2 * LANES), np.float32)
    for base in (_head_offset(0), LANES + _head_offset(1)):
        for j in range(N_PARTS):
            selk[ONES_LANE, base + j] = 1.0
            for h in range(N_HEADS):
                selk[j * LANES + h, base + _bias_slot(j, h)] = 1.0
    return jnp.asarray(selk, BF16)


def _proj_kernel(x_ref, g_ref, wrow_ref, wt_ref, bfr_ref, bfc_ref, convw_ref, selk_ref,
                 cin_row_ref, cin_col_ref, uin_ref,
                 kaug_ref, qaug_ref, vt_ref, conv_ref, *rest, t_rows, n_valid, emit_carry, tk):
    if emit_carry:
        cumrow_out, cumt_out, u_out, c_row, c_col, u_prev = rest
    else:
        c_row, c_col, u_prev = rest

    @pl.when(pl.program_id(1) == 0)
    def _():
        c_row[...] = cin_row_ref[...]
        c_col[...] = cin_col_ref[...]
        u_prev[...] = uin_ref[...]

    T = t_rows
    x = x_ref[...]
    ms = jnp.mean(x * x, axis=-1, keepdims=True)
    z = x * lax.rsqrt(ms + EPS) * g_ref[...]
    zb = z.astype(BF16)
    r = jnp.dot(zb, wrow_ref[...], preferred_element_type=F32)
    tt = lax.dot_general(wt_ref[...], zb, (((1,), (1,)), ((), ())),
                         preferred_element_type=F32)
    k = r[:, 0:ATTN_W]
    xc = r[:, ATTN_W:ATTN_W + CONV_W]
    bg = r[:, ATTN_W + CONV_W:ATTN_W + 2 * CONV_W]
    cg = r[:, ATTN_W + 2 * CONV_W:ATTN_W + 3 * CONV_W]
    fr = r[:, ATTN_W + 3 * CONV_W:ATTN_W + 3 * CONV_W + LANES]
    qt = tt[0:ATTN_W]
    vt = tt[ATTN_W:2 * ATTN_W]
    ft = tt[2 * ATTN_W:2 * ATTN_W + PART_ROWS]

    lane_r = lax.broadcasted_iota(jnp.int32, (T, LANES), 1)
    row_r = lax.broadcasted_iota(jnp.int32, (T, LANES), 0)
    lfr = jnp.where(lane_r < N_HEADS, _log_sigmoid(fr + bfr_ref[...]), 0.0)
    lfr = lfr + jnp.where(row_r == 0, c_row[0:1, :], 0.0)
    tri_r = lax.broadcasted_iota(jnp.int32, (T, T), 0)
    tri_c = lax.broadcasted_iota(jnp.int32, (T, T), 1)
    tri_l = (tri_c <= tri_r).astype(BF16)
    tri_u = (tri_r <= tri_c).astype(BF16)
    c3 = jnp.dot(tri_l, jnp.concatenate(_split3(lfr), axis=1), preferred_element_type=F32)
    cum_row = c3[:, 0:LANES] + c3[:, LANES:2 * LANES] + c3[:, 2 * LANES:3 * LANES]

    kb = cum_row * (-LOG2E)
    if n_valid < T:
        kb = kb + jnp.where(row_r >= n_valid, NEG_BIG, 0.0)
    kb = jnp.where(lane_r == ONES_LANE, 1.0, kb)
    e = jnp.dot(jnp.concatenate(_split3(kb), axis=1), selk_ref[...],
                preferred_element_type=F32)
    e_even = e[:, 0:LANES].astype(BF16)
    e_odd = e[:, LANES:2 * LANES].astype(BF16)
    for p in range(N_HEADS // 2):
        kp = k[:, p * LANES:(p + 1) * LANES].astype(BF16)
        kaug_ref[:, 2 * p * LANES:(2 * p + 1) * LANES] = jnp.where(lane_r < HEAD_DIM, kp, e_even)
        kaug_ref[:, (2 * p + 1) * LANES:(2 * p + 2) * LANES] = jnp.where(lane_r >= HEAD_DIM, kp, e_odd)

    lane_c = lax.broadcasted_iota(jnp.int32, (PART_ROWS, T), 1)
    row_c = lax.broadcasted_iota(jnp.int32, (PART_ROWS, T), 0)
    lfc = jnp.where(row_c < N_HEADS, _log_sigmoid(ft + bfc_ref[:, 0:1]), 0.0)
    lfc = lfc + jnp.where(lane_c == 0, c_col[:, 0:1], 0.0)
    c3c = jnp.dot(jnp.concatenate(_split3(lfc), axis=0), tri_u, preferred_element_type=F32)
    cum_t = c3c[0:PART_ROWS] + c3c[PART_ROWS:2 * PART_ROWS] + c3c[2 * PART_ROWS:3 * PART_ROWS]
    cq_parts = [p.astype(F32) for p in _split3(cum_t * LOG2E)]
    bias_rows = HEAD_DIM // 2
    row_b = lax.broadcasted_iota(jnp.int32, (bias_rows, T), 0)
    zero_rows = jnp.zeros((HEAD_DIM - bias_rows, T), BF16)
    for h in range(N_HEADS):
        qh = (qt[h * HEAD_DIM:(h + 1) * HEAD_DIM] * LOG2E).astype(BF16)
        own = (row_b == _bias_slot(0, h)) | (row_b == _bias_slot(1, h)) | (row_b == _bias_slot(2, h))
        bias = jnp.where(own, 1.0, 0.0)
        for j in range(N_PARTS):
            bias = jnp.where(row_b == j, cq_parts[j][h:h + 1, :], bias)
        lo = h * HEAD_SLOT + _head_offset(h)
        qaug_ref[lo:lo + bias_rows, :] = bias.astype(BF16)
        qaug_ref[lo + bias_rows:lo + HEAD_DIM, :] = zero_rows
        lo_q = h * HEAD_SLOT + (HEAD_DIM - _head_offset(h))
        qaug_ref[lo_q:lo_q + HEAD_DIM, :] = qh

    vtb = vt.astype(BF16)
    ones_rows = jnp.ones((V_SLOT - HEAD_DIM, tk), BF16)
    for s in range(T // tk):
        for h in range(N_HEADS):
            vt_ref[s, h * V_SLOT:h * V_SLOT + HEAD_DIM, :] = vtb[h * HEAD_DIM:(h + 1) * HEAD_DIM, s * tk:(s + 1) * tk]
            vt_ref[s, h * V_SLOT + HEAD_DIM:(h + 1) * V_SLOT, :] = ones_rows

    u = cg * xc
    row_u = lax.broadcasted_iota(jnp.int32, (T, CONV_W), 0)
    p1 = u_prev[7:8, :]
    p2 = u_prev[6:7, :]
    u1 = jnp.where(row_u == 0, p1, pltpu.roll(u, 1, axis=0))
    u2 = jnp.where(row_u == 0, p2, jnp.where(row_u == 1, p1, pltpu.roll(u, 2, axis=0)))
    zc = convw_ref[0:1, :] * u2 + convw_ref[1:2, :] * u1 + convw_ref[2:3, :] * u
    conv_ref[...] = bg * zc

    if emit_carry:
        cumrow_out[...] = cum_row
        cumt_out[...] = cum_t
        u_out[...] = u
    c_row[0:1, :] = cum_row[T - 1:T, :]
    c_col[...] = jnp.broadcast_to(cum_t[:, T - 1:T], (PART_ROWS, LANES))
    u_prev[...] = u[T - 8:T, :]


def _proj_call(xs, g, wrow, wt, bfr, bfc, convw, selk, cin_row, cin_col, uin,
               *, t_rows, n_valid, emit_carry):
    bn, ln, _ = xs.shape
    T = t_rows
    tk = min(ATT_T, T)
    nt = ln // T
    const2 = lambda b, t: (0, 0)
    in_specs = [
        pl.BlockSpec((None, T, D_MODEL), lambda b, t: (b, t, 0)),
        pl.BlockSpec(g.shape, const2),
        pl.BlockSpec(wrow.shape, const2),
        pl.BlockSpec(wt.shape, const2),
        pl.BlockSpec(bfr.shape, const2),
        pl.BlockSpec(bfc.shape, const2),
        pl.BlockSpec(convw.shape, const2),
        pl.BlockSpec(selk.shape, const2),
        pl.BlockSpec(cin_row.shape, const2),
        pl.BlockSpec(cin_col.shape, const2),
        pl.BlockSpec(uin.shape, const2),
    ]
    out_shape = [
        jax.ShapeDtypeStruct((bn, ln, AUG_W), BF16),
        jax.ShapeDtypeStruct((bn, AUG_W, ln), BF16),
        jax.ShapeDtypeStruct((bn, ln // tk, V_AUG, tk), BF16),
        jax.ShapeDtypeStruct((bn, ln, CONV_W), F32),
    ]
    out_specs = [
        pl.BlockSpec((None, T, AUG_W), lambda b, t: (b, t, 0)),
        pl.BlockSpec((None, AUG_W, T), lambda b, t: (b, 0, t)),
        pl.BlockSpec((None, T // tk, V_AUG, tk), lambda b, t: (b, t, 0, 0)),
        pl.BlockSpec((None, T, CONV_W), lambda b, t: (b, t, 0)),
    ]
    if emit_carry:
        out_shape += [
            jax.ShapeDtypeStruct((bn, ln, LANES), F32),
            jax.ShapeDtypeStruct((bn, PART_ROWS, ln), F32),
            jax.ShapeDtypeStruct((bn, ln, CONV_W), F32),
        ]
        out_specs += [
            pl.BlockSpec((None, T, LANES), lambda b, t: (b, t, 0)),
            pl.BlockSpec((None, PART_ROWS, T), lambda b, t: (b, 0, t)),
            pl.BlockSpec((None, T, CONV_W), lambda b, t: (b, t, 0)),
        ]
    kern = functools.partial(_proj_kernel, t_rows=T, n_valid=n_valid, emit_carry=emit_carry, tk=tk)
    return pl.pallas_call(
        kern,
        grid=(bn, nt),
        in_specs=in_specs,
        out_specs=out_specs,
        out_shape=out_shape,
        scratch_shapes=[pltpu.VMEM((8, LANES), F32), pltpu.VMEM((PART_ROWS, LANES), F32),
                        pltpu.VMEM((8, CONV_W), F32)],
        compiler_params=pltpu.CompilerParams(dimension_semantics=("arbitrary", "arbitrary"),
                                             vmem_limit_bytes=VMEM_LIMIT),
        name="proj_meta" if emit_carry else "proj",
    )(xs, g, wrow, wt, bfr, bfc, convw, selk, cin_row, cin_col, uin)


def _attn_kernel(q_ref, k_ref, v_ref, km_ref, vm_ref, g_ref, o_ref, s_s, m_s, acc_s):
    i = pl.program_id(1)
    T = ATT_T
    causal = (lax.broadcasted_iota(jnp.int32, (T, T), 0) <= lax.broadcasted_iota(jnp.int32, (T, T), 1))

    def scores(h, kblk, slot, masked):
        qa = q_ref[h * HEAD_SLOT:(h + 1) * HEAD_SLOT, :]
        s = jnp.dot(kblk, qa, preferred_element_type=F32)
        if masked:
            s = jnp.where(causal, s, NEG_BIG)
        s_s[slot, h] = s

    def consume(h, vblk, slot):
        s = s_s[slot, h]
        m_old = m_s[h, 0:1, :]
        m_new = jnp.maximum(m_old, jnp.max(s, axis=0, keepdims=True))
        alpha = jnp.exp2(m_old - m_new)
        pm = jnp.exp2(s - m_new).astype(BF16)
        acc_s[h] = alpha * acc_s[h] + jnp.dot(vblk, pm, preferred_element_type=F32)
        m_s[h, 0:1, :] = m_new

    def step(k_next, next_slot, masked, v_cur, cur_slot):
        ahead = 2
        for h in range(ahead):
            scores(h, k_next(h), next_slot, masked)
        for h in range(N_HEADS):
            consume(h, v_cur(h), cur_slot)
            if h + ahead < N_HEADS:
                scores(h + ahead, k_next(h + ahead), next_slot, masked)

    def k_of(j):
        row = pl.ds(pl.multiple_of(j * T, T), T)
        return lambda h: k_ref[row, h * HEAD_SLOT:(h + 1) * HEAD_SLOT]

    def v_of(j):
        return lambda h: v_ref[j, h * V_SLOT:(h + 1) * V_SLOT, :]

    k_meta = lambda h: km_ref[:, h * HEAD_SLOT:(h + 1) * HEAD_SLOT]
    v_meta = lambda h: vm_ref[h * V_SLOT:(h + 1) * V_SLOT, :]

    m_s[...] = jnp.full(m_s.shape, NEG_BIG, F32)
    acc_s[...] = jnp.zeros(acc_s.shape, F32)
    for h in range(N_HEADS):
        scores(h, k_meta(h), 0, False)

    def v_at(t):
        return lambda h: jnp.where(t == 0, v_meta(h), v_ref[jnp.maximum(t - 1, 0), h * V_SLOT:(h + 1) * V_SLOT, :])

    def pair_body(p, c):
        t0 = 2 * p
        step(k_of(t0), 1, False, v_at(t0), 0)
        step(k_of(t0 + 1), 0, False, v_at(t0 + 1), 1)
        return c

    lax.fori_loop(0, i // 2, pair_body, 0)

    @pl.when(i % 2 == 1)
    def _():
        step(k_of(i - 1), 1, False, v_at(i - 1), 0)
        step(k_of(i), 0, True, v_at(i), 1)
        for h in range(N_HEADS):
            consume(h, v_of(i)(h), 0)

    @pl.when(i % 2 == 0)
    def _():
        step(k_of(i), 1, True, v_at(i), 0)
        for h in range(N_HEADS):
            consume(h, v_of(i)(h), 1)

    for p in range(N_HEADS // 2):
        halves = []
        for h in (2 * p, 2 * p + 1):
            o = acc_s[h, 0:HEAD_DIM, :] * (1.0 / acc_s[h, HEAD_DIM:HEAD_DIM + 1, :])
            halves.append(o * lax.rsqrt(jnp.mean(o * o, axis=0, keepdims=True) + EPS))
        pair = jnp.concatenate(halves, axis=0)
        o_ref[:, p * LANES:(p + 1) * LANES] = (pair.T * g_ref[:, p * LANES:(p + 1) * LANES]).astype(o_ref.dtype)


def _attn_call(qaug, kaug, vt, kmeta, vmeta, gain):
    bn, _, ln = qaug.shape
    T = ATT_T
    nq = ln // T
    return pl.pallas_call(
        _attn_kernel,
        grid=(bn, nq),
        in_specs=[
            pl.BlockSpec((None, AUG_W, T), lambda b, i: (b, 0, i)),
            pl.BlockSpec((None, ln, AUG_W), lambda b, i: (b, 0, 0)),
            pl.BlockSpec((None, ln // T, V_AUG, T), lambda b, i: (b, 0, 0, 0)),
            pl.BlockSpec(kmeta.shape, lambda b, i: (0, 0)),
            pl.BlockSpec(vmeta.shape, lambda b, i: (0, 0)),
            pl.BlockSpec(gain.shape, lambda b, i: (0, 0)),
        ],
        out_specs=pl.BlockSpec((None, T, ATTN_W), lambda b, i: (b, i, 0)),
        out_shape=jax.ShapeDtypeStruct((bn, ln, ATTN_W), BF16),
        scratch_shapes=[pltpu.VMEM((2, N_HEADS, T, T), F32),
                        pltpu.VMEM((N_HEADS, 8, T), F32),
                        pltpu.VMEM((N_HEADS, V_SLOT, T), F32)],
        compiler_params=pltpu.CompilerParams(dimension_semantics=("arbitrary", "arbitrary"),
                                             vmem_limit_bytes=VMEM_LIMIT),
        name="attention",
    )(qaug, kaug, vt, kmeta, vmeta, gain)


def _mix_kernel(attn_ref, conv_ref, x_ref, gc_ref, gsum_ref, woa_ref, woc_ref, fg_ref,
                wrh_ref, br_ref, h_ref, z_ref, rinfo_ref, counts_ref, cnt):
    T = MIX_T

    @pl.when(pl.program_id(0) == 0)
    def _():
        cnt[...] = jnp.zeros_like(cnt)

    c = conv_ref[...]
    c2 = c * c
    c2h = c2.astype(BF16)
    c2l = (c2 - c2h.astype(F32)).astype(BF16)
    ss = (jnp.dot(c2h, gsum_ref[...], preferred_element_type=F32)
          + jnp.dot(c2l, gsum_ref[...], preferred_element_type=F32))
    cn = (c * lax.rsqrt(ss * (1.0 / HEAD_DIM) + EPS) * gc_ref[...]).astype(BF16)
    hadd = (jnp.dot(attn_ref[...], woa_ref[...], preferred_element_type=F32)
            + jnp.dot(cn, woc_ref[...], preferred_element_type=F32))
    h = x_ref[...] + hadd
    h_ref[...] = h
    z = h * lax.rsqrt(jnp.mean(h * h, axis=-1, keepdims=True) + EPS) * fg_ref[...]
    z_ref[...] = z
    zh = z.astype(BF16)
    zl = (z - zh.astype(F32)).astype(BF16)
    both = jnp.dot(zh, wrh_ref[...], preferred_element_type=F32)
    logits = (both[:, 0:LANES] + both[:, LANES:2 * LANES]
              + jnp.dot(zl, wrh_ref[:, 0:LANES], preferred_element_type=F32)) + br_ref[...]

    lane = lax.broadcasted_iota(jnp.int32, (T, LANES), 1).astype(F32)
    big = float(LANES)
    gl = jnp.where(lane < N_GROUPS, logits, -jnp.inf)
    gmax = jnp.max(gl, axis=-1, keepdims=True)
    gsum = jnp.sum(jnp.exp(gl - gmax), axis=-1, keepdims=True)
    g_w = 1.0 / gsum
    g_idx = jnp.min(jnp.where(gl == gmax, lane, big), axis=-1, keepdims=True)
    e_lo = ROUTER_LANE0 + EXPERTS_PER_GROUP * g_idx
    emask = (lane >= e_lo) & (lane < e_lo + EXPERTS_PER_GROUP)
    el = jnp.where(emask, logits, -jnp.inf)
    emax = jnp.max(el, axis=-1, keepdims=True)
    eexp = jnp.exp(el - emax)
    probs = eexp / jnp.sum(eexp, axis=-1, keepdims=True)
    probs = jnp.where(emask, probs, -1.0)
    p1 = jnp.max(probs, axis=-1, keepdims=True)
    i1 = jnp.min(jnp.where(probs == p1, lane, big), axis=-1, keepdims=True)
    probs2 = jnp.where(lane == i1, -1.0, probs)
    p2 = jnp.max(probs2, axis=-1, keepdims=True)
    i2 = jnp.min(jnp.where(probs2 == p2, lane, big), axis=-1, keepdims=True)
    psum = p1 + p2
    w1 = g_w * (p1 / psum)
    w2 = g_w * (p2 / psum)

    hit = ((lane == i1) | (lane == i2)).astype(F32)
    tri_r = lax.broadcasted_iota(jnp.int32, (T, T), 0)
    tri_c = lax.broadcasted_iota(jnp.int32, (T, T), 1)
    strict_l = (tri_c < tri_r).astype(BF16)
    before = jnp.dot(strict_l, hit.astype(BF16), preferred_element_type=F32) + cnt[0:1, :]
    r1 = jnp.sum(jnp.where(lane == i1, before, 0.0), axis=-1, keepdims=True)
    r2 = jnp.sum(jnp.where(lane == i2, before, 0.0), axis=-1, keepdims=True)
    cnt[0:1, :] = cnt[0:1, :] + jnp.sum(hit, axis=0, keepdims=True)
    counts_ref[...] = cnt[...]

    e1 = i1 - ROUTER_LANE0
    e2 = i2 - ROUTER_LANE0
    rinfo = jnp.where(lane == 0, e1, jnp.where(lane == 1, e2, jnp.where(lane == 2, w1, jnp.where(
        lane == 3, w2, jnp.where(lane == 4, r1, jnp.where(lane == 5, r2, 0.0))))))
    rinfo_ref[...] = rinfo


def _mix_call(attn, conv, x, gc, gsum, woa, woc, fg, wrh, br):
    n = x.shape[0]
    T = MIX_T
    const = lambda i: (0, 0)
    row = lambda w: pl.BlockSpec((T, w), lambda i: (i, 0))
    return pl.pallas_call(
        _mix_kernel,
        grid=(n // T,),
        in_specs=[row(ATTN_W), row(CONV_W), row(D_MODEL)] + [
            pl.BlockSpec(a.shape, const) for a in (gc, gsum, woa, woc, fg, wrh, br)],
        out_specs=[row(D_MODEL), row(D_MODEL), row(LANES), pl.BlockSpec((8, LANES), const)],
        out_shape=[jax.ShapeDtypeStruct((n, D_MODEL), F32), jax.ShapeDtypeStruct((n, D_MODEL), F32),
                   jax.ShapeDtypeStruct((n, LANES), F32), jax.ShapeDtypeStruct((8, LANES), F32)],
        scratch_shapes=[pltpu.VMEM((8, LANES), F32)],
        compiler_params=pltpu.CompilerParams(dimension_semantics=("arbitrary",),
                                             vmem_limit_bytes=VMEM_LIMIT),
        name="mix",
    )(attn, conv, x, gc, gsum, woa, woc, fg, wrh, br)


def _dispatch_kernel(pad_end_ref, cnt_ref, dest_ref, z_ref, xbuf_ref, zero_v, sem, zsem):
    T = DISPATCH_T

    @pl.when(pl.program_id(0) == 0)
    def _():
        zero_v[...] = jnp.zeros_like(zero_v)

        def zero_copy(e):
            first = pl.multiple_of(jnp.maximum(pad_end_ref[e] - EXPERT_BM, 0), EXPERT_BM)
            return pltpu.make_async_copy(zero_v, xbuf_ref.at[pl.ds(first, EXPERT_BM)], zsem)

        for e in range(N_EXPERTS):
            @pl.when(cnt_ref[e] > 0)
            def _():
                zero_copy(e).start()
        for e in range(N_EXPERTS):
            @pl.when(cnt_ref[e] > 0)
            def _():
                zero_copy(e).wait()

        def tail_copy(b):
            return pltpu.make_async_copy(
                zero_v, xbuf_ref.at[pl.ds(pl.multiple_of(b * EXPERT_BM, EXPERT_BM), EXPERT_BM)], zsem)

        n_used = pad_end_ref[N_EXPERTS - 1] // EXPERT_BM
        n_all = xbuf_ref.shape[0] // EXPERT_BM
        lax.fori_loop(n_used, n_all, lambda b, c: (tail_copy(b).start(), c)[1], 0)
        lax.fori_loop(n_used, n_all, lambda b, c: (tail_copy(b).wait(), c)[1], 0)

    def row_copy(r, slot):
        d = dest_ref[0, 2 * r + slot]
        return pltpu.make_async_copy(z_ref.at[pl.ds(r, 1)], xbuf_ref.at[pl.ds(d, 1)], sem)

    def start(r, c):
        row_copy(r, 0).start(priority=0)
        row_copy(r, 1).start(priority=1)
        return c

    def wait(r, c):
        row_copy(r, 0).wait()
        row_copy(r, 1).wait()
        return c

    lax.fori_loop(0, T, start, 0, unroll=DMA_UNROLL)
    lax.fori_loop(0, T, wait, 0, unroll=DMA_UNROLL)


def _dispatch_call(pad_end, cnt, dest, z, n_rows):
    n = z.shape[0]
    T = DISPATCH_T
    grid_spec = pltpu.PrefetchScalarGridSpec(
        num_scalar_prefetch=2,
        grid=(n // T,),
        in_specs=[
            pl.BlockSpec((None, 1, 2 * T), lambda i, pe, ct: (i, 0, 0), memory_space=pltpu.SMEM),
            pl.BlockSpec((T, D_MODEL), lambda i, pe, ct: (i, 0)),
        ],
        out_specs=pl.BlockSpec(memory_space=pl.ANY),
        scratch_shapes=[pltpu.VMEM((EXPERT_BM, D_MODEL), F32), pltpu.SemaphoreType.DMA(()),
                        pltpu.SemaphoreType.DMA(())],
    )
    return pl.pallas_call(
        _dispatch_kernel,
        grid_spec=grid_spec,
        out_shape=jax.ShapeDtypeStruct((n_rows, D_MODEL), F32),
        compiler_params=pltpu.CompilerParams(dimension_semantics=("arbitrary",)),
        name="dispatch",
    )(pad_end, cnt, dest, z)


def _expert_kernel(blk_e_ref, nvb_ref, x_ref, wg_ref, wu_ref, wd_ref, y_ref, wgu_s, wd_s):
    blk = pl.program_id(0)
    valid = blk < nvb_ref[0]
    prev_e = blk_e_ref[jnp.maximum(blk - 1, 0)]

    @pl.when(valid & ((blk == 0) | (blk_e_ref[blk] != prev_e)))
    def _():
        wgu_s[:, 0:D_EXPERT] = wg_ref[...].astype(BF16)
        wgu_s[:, D_EXPERT:2 * D_EXPERT] = wu_ref[...].astype(BF16)
        wd_s[...] = wd_ref[...].astype(BF16)

    @pl.when(valid)
    def _():
        xb = x_ref[...].astype(BF16)
        gu = jnp.dot(xb, wgu_s[...], preferred_element_type=F32)
        g = gu[:, 0:D_EXPERT]
        u = gu[:, D_EXPERT:2 * D_EXPERT]
        a = (g * jax.nn.sigmoid(g)) * u
        y_ref[...] = jnp.dot(a.astype(BF16), wd_s[...], preferred_element_type=F32)


def _expert_call(blk_e, nvb, xbuf, w_gate, w_up, w_down):
    rows = xbuf.shape[0]
    bm = EXPERT_BM
    used = lambda i, be, nv: (jnp.minimum(i, nv[0] - 1), 0)
    grid_spec = pltpu.PrefetchScalarGridSpec(
        num_scalar_prefetch=2,
        grid=(rows // bm,),
        in_specs=[
            pl.BlockSpec((bm, D_MODEL), used),
            pl.BlockSpec((None, D_MODEL, D_EXPERT), lambda i, be, nv: (be[i], 0, 0)),
            pl.BlockSpec((None, D_MODEL, D_EXPERT), lambda i, be, nv: (be[i], 0, 0)),
            pl.BlockSpec((None, D_EXPERT, D_MODEL), lambda i, be, nv: (be[i], 0, 0)),
        ],
        out_specs=pl.BlockSpec((bm, D_MODEL), used),
        scratch_shapes=[pltpu.VMEM((D_MODEL, 2 * D_EXPERT), BF16), pltpu.VMEM((D_EXPERT, D_MODEL), BF16)],
    )
    return pl.pallas_call(
        _expert_kernel,
        grid_spec=grid_spec,
        out_shape=jax.ShapeDtypeStruct((rows, D_MODEL), F32),
        input_output_aliases={2: 0},
        compiler_params=pltpu.CompilerParams(dimension_semantics=("arbitrary",),
                                             vmem_limit_bytes=VMEM_LIMIT),
        name="experts",
    )(blk_e, nvb, xbuf, w_gate, w_up, w_down)


def _combine_kernel(dest_ref, dest_next_ref, h_ref, rinfo_ref, fn_ref, ybuf_ref, o_ref, ybuf_v, sem):
    T = COMBINE_T
    i = pl.program_id(0)
    cur = i % 2

    def row_copy(dref, buf, r, k):
        d = dref[0, 2 * r + k]
        return pltpu.make_async_copy(ybuf_ref.at[pl.ds(d, 1)], ybuf_v.at[buf, k, pl.ds(r, 1)], sem.at[buf])

    def issue(dref, buf):
        def start(r, c):
            row_copy(dref, buf, r, 0).start(priority=0)
            row_copy(dref, buf, r, 1).start(priority=1)
            return c
        lax.fori_loop(0, T, start, 0, unroll=DMA_UNROLL)

    @pl.when(i == 0)
    def _():
        issue(dest_ref, 0)

    @pl.when(i + 1 < pl.num_programs(0))
    def _():
        issue(dest_next_ref, 1 - cur)

    def wait(r, c):
        row_copy(dest_ref, cur, r, 0).wait()
        row_copy(dest_ref, cur, r, 1).wait()
        return c

    lax.fori_loop(0, T, wait, 0, unroll=DMA_UNROLL)
    rinfo = rinfo_ref[...]
    w1 = rinfo[:, 2:3]
    w2 = rinfo[:, 3:4]
    h = h_ref[...] + (ybuf_v[cur, 0] * w1 + ybuf_v[cur, 1] * w2)
    o_ref[...] = h * lax.rsqrt(jnp.mean(h * h, axis=-1, keepdims=True) + EPS) * fn_ref[...]


def _combine_call(dest, h, rinfo, fn, ybuf):
    n = h.shape[0]
    T = COMBINE_T
    nt = n // T
    return pl.pallas_call(
        _combine_kernel,
        grid=(nt,),
        in_specs=[
            pl.BlockSpec((None, 1, 2 * T), lambda i: (i, 0, 0), memory_space=pltpu.SMEM),
            pl.BlockSpec((None, 1, 2 * T), lambda i: (jnp.minimum(i + 1, nt - 1), 0, 0), memory_space=pltpu.SMEM),
            pl.BlockSpec((T, D_MODEL), lambda i: (i, 0)),
            pl.BlockSpec((T, LANES), lambda i: (i, 0)),
            pl.BlockSpec(fn.shape, lambda i: (0, 0)),
            pl.BlockSpec(memory_space=pl.ANY),
        ],
        out_specs=pl.BlockSpec((T, D_MODEL), lambda i: (i, 0)),
        out_shape=jax.ShapeDtypeStruct((n, D_MODEL), F32),
        scratch_shapes=[pltpu.VMEM((2, 2, T, D_MODEL), F32), pltpu.SemaphoreType.DMA((2,))],
        compiler_params=pltpu.CompilerParams(dimension_semantics=("arbitrary",),
                                             vmem_limit_bytes=VMEM_LIMIT),
        name="combine",
    )(dest, dest, h, rinfo, fn, ybuf)


def kernel(x, meta_tokens, attn_norm, w_in, b_forget, conv_w, mix_norm, w_out, ffn_norm,
           w_router_group, b_router_group, w_router_expert, b_router_expert,
           w_gate, w_up, w_down, final_norm):
    bsz, seq, d = x.shape
    assert d == D_MODEL and seq % PROJ_T == 0 and meta_tokens.shape[0] == N_META
    assert attn_norm.shape[0] == 1, "single-layer block"
    n_tok = bsz * seq

    wi = w_in[0]
    o = 0
    w_q = wi[:, o:o + ATTN_W]; o += ATTN_W
    w_k = wi[:, o:o + ATTN_W]; o += ATTN_W
    w_v = wi[:, o:o + ATTN_W]; o += ATTN_W
    w_f = wi[:, o:o + N_HEADS]; o += N_HEADS
    w_xc = wi[:, o:o + CONV_W]; o += CONV_W
    w_bg = wi[:, o:o + CONV_W]; o += CONV_W
    w_cg = wi[:, o:o + CONV_W]
    wrow = jnp.concatenate([w_k, w_xc, w_bg, w_cg, w_f, jnp.zeros((D_MODEL, LANES - N_HEADS), F32)],
                           axis=1).astype(BF16)
    wt = jnp.concatenate([w_q.T * (HEAD_DIM ** -0.5), w_v.T, w_f.T,
                          jnp.zeros((PART_ROWS - N_HEADS, D_MODEL), F32)], axis=0).astype(BF16)
    g_attn = attn_norm[0].reshape(1, D_MODEL)
    bfr = jnp.zeros((1, LANES), F32).at[0, :N_HEADS].set(b_forget[0])
    bfc = jnp.zeros((PART_ROWS, LANES), F32).at[:N_HEADS, :].set(b_forget[0][:, None])
    convw = jnp.zeros((8, CONV_W), F32).at[:3].set(conv_w[0])
    selk = _selection_matrix()

    meta_pad = jnp.zeros((1, META_PAD, D_MODEL), F32).at[0, :N_META].set(meta_tokens)
    zrow = jnp.zeros((8, LANES), F32)
    zcol = jnp.zeros((PART_ROWS, LANES), F32)
    zu = jnp.zeros((8, CONV_W), F32)
    kmeta, _, vmeta, _, cumrow_m, cumt_m, u_m = _proj_call(
        meta_pad, g_attn, wrow, wt, bfr, bfc, convw, selk, zrow, zcol, zu,
        t_rows=META_PAD, n_valid=N_META, emit_carry=True)
    cin_row = jnp.zeros((8, LANES), F32).at[0].set(cumrow_m[0, N_META - 1])
    cin_col = jnp.broadcast_to(cumt_m[0, :, N_META - 1:N_META], (PART_ROWS, LANES))
    uin = u_m[0, N_META - 8:N_META]

    kaug, qaug, vt, conv = _proj_call(
        x, g_attn, wrow, wt, bfr, bfc, convw, selk, cin_row, cin_col, uin,
        t_rows=PROJ_T, n_valid=PROJ_T, emit_carry=False)

    mixg = mix_norm[0]
    attn = _attn_call(qaug, kaug, vt, kmeta[0], vmeta[0, 0], mixg[:ATTN_W].reshape(1, ATTN_W))

    gidx = np.arange(CONV_W) // HEAD_DIM
    gsum = jnp.asarray(gidx[:, None] == gidx[None, :], BF16)
    wo = w_out[0].astype(BF16)
    wr = jnp.zeros((D_MODEL, LANES), F32)
    wr = wr.at[:, :N_GROUPS].set(w_router_group[0]).at[:, ROUTER_LANE0:ROUTER_LANE0 + N_EXPERTS].set(
        w_router_expert[0])
    wr_hi = wr.astype(BF16)
    wrh = jnp.concatenate([wr_hi, (wr - wr_hi.astype(F32)).astype(BF16)], axis=1)
    br = jnp.zeros((1, LANES), F32)
    br = br.at[0, :N_GROUPS].set(b_router_group[0]).at[0, ROUTER_LANE0:ROUTER_LANE0 + N_EXPERTS].set(
        b_router_expert[0])
    h, z, rinfo, counts = _mix_call(
        attn.reshape(n_tok, ATTN_W), conv.reshape(n_tok, CONV_W), x.reshape(n_tok, D_MODEL),
        mixg[ATTN_W:].reshape(1, CONV_W), gsum, wo[:ATTN_W], wo[ATTN_W:], ffn_norm[0].reshape(1, D_MODEL),
        wrh, br)

    cnt = counts[0, ROUTER_LANE0:ROUTER_LANE0 + N_EXPERTS].astype(jnp.int32)
    padded = (cnt + EXPERT_BM - 1) // EXPERT_BM * EXPERT_BM
    pad_end = jnp.cumsum(padded)
    pad_start = pad_end - padded
    eid = rinfo[:, 0:2].astype(jnp.int32)
    rank = rinfo[:, 4:6].astype(jnp.int32)
    expert_ids = jnp.arange(N_EXPERTS, dtype=jnp.int32)
    dest = rank + jnp.sum(jnp.where(eid[:, :, None] == expert_ids, pad_start, 0), axis=-1)
    dest = dest.reshape(n_tok // DISPATCH_T, 1, 2 * DISPATCH_T)
    n_blk = (2 * n_tok) // EXPERT_BM + N_EXPERTS
    nvb = (pad_end[-1] // EXPERT_BM).reshape(1)
    blk_row = jnp.minimum(jnp.arange(n_blk, dtype=jnp.int32), nvb - 1) * EXPERT_BM
    blk_e = jnp.sum(blk_row[:, None] >= pad_end[None, :], axis=1).astype(jnp.int32)

    xbuf = _dispatch_call(pad_end, cnt, dest, z, n_blk * EXPERT_BM)
    ybuf = _expert_call(blk_e, nvb, xbuf, w_gate[0], w_up[0], w_down[0])
    out = _combine_call(dest, h, rinfo, final_norm.reshape(1, D_MODEL), ybuf)
    return out.reshape(bsz, seq, D_MODEL)
```

```python
import functools

import numpy as np
import jax
import jax.numpy as jnp
from jax import lax
from jax.experimental import pallas as pl
from jax.experimental.pallas import tpu as pltpu

F32 = jnp.float32
BF16 = jnp.bfloat16

D_MODEL = 1024
HEAD_DIM = 64
N_HEADS = 8
ATTN_W = N_HEADS * HEAD_DIM
CONV_W = 512
N_META = 16
N_GROUPS = 4
EXPERTS_PER_GROUP = 8
N_EXPERTS = N_GROUPS * EXPERTS_PER_GROUP
D_EXPERT = 512
EPS = 1e-6
NEG_BIG = -1e30
LOG2E = 1.4426950408889634

LANES = 128
ROW_SUB = D_MODEL // LANES
HEAD_SLOT = 2 * HEAD_DIM
V_SLOT = HEAD_DIM + 16
V_AUG = N_HEADS * V_SLOT
AUG_W = N_HEADS * HEAD_SLOT
N_PARTS = 3
PART_ROWS = 16
ONES_LANE = 8
ROUTER_LANE0 = N_GROUPS

PROJ_T = 512
ATT_T = 256
META_PAD = ATT_T
MIX_T = 512
DISPATCH_T = 512
EXPERT_BM = 512
COMBINE_T = 512
DMA_UNROLL = 8
VMEM_LIMIT = 48 * 1024 * 1024


def _split3(x):
    hi = x.astype(BF16)
    r1 = x - hi.astype(F32)
    mid = r1.astype(BF16)
    r2 = r1 - mid.astype(F32)
    return hi, mid, r2.astype(BF16)


def _log_sigmoid(x):
    return jnp.minimum(x, 0.0) - jnp.log1p(jnp.exp(-jnp.abs(x)))


def _head_offset(h):
    return HEAD_DIM if h % 2 == 0 else 0


def _bias_slot(j, h):
    return N_PARTS + N_HEADS * j + h


def _selection_matrix():
    selk = np.zeros((N_PARTS * LANES, 2 * LANES), np.float32)
    for base in (_head_offset(0), LANES + _head_offset(1)):
        for j in range(N_PARTS):
            selk[ONES_LANE, base + j] = 1.0
            for h in range(N_HEADS):
                selk[j * LANES + h, base + _bias_slot(j, h)] = 1.0
    return jnp.asarray(selk, BF16)


def _proj_kernel(x_ref, g_ref, wrow_ref, wt_ref, bfr_ref, bfc_ref, convw_ref, selk_ref,
                 cin_row_ref, cin_col_ref, uin_ref,
                 kaug_ref, qaug_ref, vt_ref, conv_ref, *rest, t_rows, n_valid, emit_carry, tk):
    if emit_carry:
        cumrow_out, cumt_out, u_out, c_row, c_col, u_prev = rest
    else:
        c_row, c_col, u_prev = rest

    @pl.when(pl.program_id(1) == 0)
    def _():
        c_row[...] = cin_row_ref[...]
        c_col[...] = cin_col_ref[...]
        u_prev[...] = uin_ref[...]

    T = t_rows
    x = x_ref[...]
    ms = jnp.mean(x * x, axis=-1, keepdims=True)
    z = x * lax.rsqrt(ms + EPS) * g_ref[...]
    zb = z.astype(BF16)
    r = jnp.dot(zb, wrow_ref[...], preferred_element_type=F32)
    tt = lax.dot_general(wt_ref[...], zb, (((1,), (1,)), ((), ())),
                         preferred_element_type=F32)
    k = r[:, 0:ATTN_W]
    xc = r[:, ATTN_W:ATTN_W + CONV_W]
    bg = r[:, ATTN_W + CONV_W:ATTN_W + 2 * CONV_W]
    cg = r[:, ATTN_W + 2 * CONV_W:ATTN_W + 3 * CONV_W]
    fr = r[:, ATTN_W + 3 * CONV_W:ATTN_W + 3 * CONV_W + LANES]
    qt = tt[0:ATTN_W]
    vt = tt[ATTN_W:2 * ATTN_W]
    ft = tt[2 * ATTN_W:2 * ATTN_W + PART_ROWS]

    lane_r = lax.broadcasted_iota(jnp.int32, (T, LANES), 1)
    row_r = lax.broadcasted_iota(jnp.int32, (T, LANES), 0)
    lfr = jnp.where(lane_r < N_HEADS, _log_sigmoid(fr + bfr_ref[...]), 0.0)
    lfr = lfr + jnp.where(row_r == 0, c_row[0:1, :], 0.0)
    tri_r = lax.broadcasted_iota(jnp.int32, (T, T), 0)
    tri_c = lax.broadcasted_iota(jnp.int32, (T, T), 1)
    tri_l = (tri_c <= tri_r).astype(BF16)
    tri_u = (tri_r <= tri_c).astype(BF16)
    c3 = jnp.dot(tri_l, jnp.concatenate(_split3(lfr), axis=1), preferred_element_type=F32)
    cum_row = c3[:, 0:LANES] + c3[:, LANES:2 * LANES] + c3[:, 2 * LANES:3 * LANES]

    kb = cum_row * (-LOG2E)
    if n_valid < T:
        kb = kb + jnp.where(row_r >= n_valid, NEG_BIG, 0.0)
    kb = jnp.where(lane_r == ONES_LANE, 1.0, kb)
    e = jnp.dot(jnp.concatenate(_split3(kb), axis=1), selk_ref[...],
                preferred_element_type=F32)
    e_even = e[:, 0:LANES].astype(BF16)
    e_odd = e[:, LANES:2 * LANES].astype(BF16)
    for p in range(N_HEADS // 2):
        kp = k[:, p * LANES:(p + 1) * LANES].astype(BF16)
        kaug_ref[:, 2 * p * LANES:(2 * p + 1) * LANES] = jnp.where(lane_r < HEAD_DIM, kp, e_even)
        kaug_ref[:, (2 * p + 1) * LANES:(2 * p + 2) * LANES] = jnp.where(lane_r >= HEAD_DIM, kp, e_odd)

    lane_c = lax.broadcasted_iota(jnp.int32, (PART_ROWS, T), 1)
    row_c = lax.broadcasted_iota(jnp.int32, (PART_ROWS, T), 0)
    lfc = jnp.where(row_c < N_HEADS, _log_sigmoid(ft + bfc_ref[:, 0:1]), 0.0)
    lfc = lfc + jnp.where(lane_c == 0, c_col[:, 0:1], 0.0)
    c3c = jnp.dot(jnp.concatenate(_split3(lfc), axis=0), tri_u, preferred_element_type=F32)
    cum_t = c3c[0:PART_ROWS] + c3c[PART_ROWS:2 * PART_ROWS] + c3c[2 * PART_ROWS:3 * PART_ROWS]
    cq_parts = [p.astype(F32) for p in _split3(cum_t * LOG2E)]
    bias_rows = HEAD_DIM // 2
    row_b = lax.broadcasted_iota(jnp.int32, (bias_rows, T), 0)
    zero_rows = jnp.zeros((HEAD_DIM - bias_rows, T), BF16)
    for h in range(N_HEADS):
        qh = (qt[h * HEAD_DIM:(h + 1) * HEAD_DIM] * LOG2E).astype(BF16)
        own = (row_b == _bias_slot(0, h)) | (row_b == _bias_slot(1, h)) | (row_b == _bias_slot(2, h))
        bias = jnp.where(own, 1.0, 0.0)
        for j in range(N_PARTS):
            bias = jnp.where(row_b == j, cq_parts[j][h:h + 1, :], bias)
        lo = h * HEAD_SLOT + _head_offset(h)
        qaug_ref[lo:lo + bias_rows, :] = bias.astype(BF16)
        qaug_ref[lo + bias_rows:lo + HEAD_DIM, :] = zero_rows
        lo_q = h * HEAD_SLOT + (HEAD_DIM - _head_offset(h))
        qaug_ref[lo_q:lo_q + HEAD_DIM, :] = qh

    vtb = vt.astype(BF16)
    ones_rows = jnp.ones((V_SLOT - HEAD_DIM, tk), BF16)
    for s in range(T // tk):
        for h in range(N_HEADS):
            vt_ref[s, h * V_SLOT:h * V_SLOT + HEAD_DIM, :] = vtb[h * HEAD_DIM:(h + 1) * HEAD_DIM, s * tk:(s + 1) * tk]
            vt_ref[s, h * V_SLOT + HEAD_DIM:(h + 1) * V_SLOT, :] = ones_rows

    u = cg * xc
    row_u = lax.broadcasted_iota(jnp.int32, (T, CONV_W), 0)
    p1 = u_prev[7:8, :]
    p2 = u_prev[6:7, :]
    u1 = jnp.where(row_u == 0, p1, pltpu.roll(u, 1, axis=0))
    u2 = jnp.where(row_u == 0, p2, jnp.where(row_u == 1, p1, pltpu.roll(u, 2, axis=0)))
    zc = convw_ref[0:1, :] * u2 + convw_ref[1:2, :] * u1 + convw_ref[2:3, :] * u
    conv_ref[...] = bg * zc

    if emit_carry:
        cumrow_out[...] = cum_row
        cumt_out[...] = cum_t
        u_out[...] = u
    c_row[0:1, :] = cum_row[T - 1:T, :]
    c_col[...] = jnp.broadcast_to(cum_t[:, T - 1:T], (PART_ROWS, LANES))
    u_prev[...] = u[T - 8:T, :]


def _proj_call(xs, g, wrow, wt, bfr, bfc, convw, selk, cin_row, cin_col, uin,
               *, t_rows, n_valid, emit_carry):
    bn, ln, _ = xs.shape
    T = t_rows
    tk = min(ATT_T, T)
    nt = ln // T
    const2 = lambda b, t: (0, 0)
    in_specs = [
        pl.BlockSpec((None, T, D_MODEL), lambda b, t: (b, t, 0)),
        pl.BlockSpec(g.shape, const2),
        pl.BlockSpec(wrow.shape, const2),
        pl.BlockSpec(wt.shape, const2),
        pl.BlockSpec(bfr.shape, const2),
        pl.BlockSpec(bfc.shape, const2),
        pl.BlockSpec(convw.shape, const2),
        pl.BlockSpec(selk.shape, const2),
        pl.BlockSpec(cin_row.shape, const2),
        pl.BlockSpec(cin_col.shape, const2),
        pl.BlockSpec(uin.shape, const2),
    ]
    out_shape = [
        jax.ShapeDtypeStruct((bn, ln, AUG_W), BF16),
        jax.ShapeDtypeStruct((bn, AUG_W, ln), BF16),
        jax.ShapeDtypeStruct((bn, ln // tk, V_AUG, tk), BF16),
        jax.ShapeDtypeStruct((bn, ln, CONV_W), F32),
    ]
    out_specs = [
        pl.BlockSpec((None, T, AUG_W), lambda b, t: (b, t, 0)),
        pl.BlockSpec((None, AUG_W, T), lambda b, t: (b, 0, t)),
        pl.BlockSpec((None, T // tk, V_AUG, tk), lambda b, t: (b, t, 0, 0)),
        pl.BlockSpec((None, T, CONV_W), lambda b, t: (b, t, 0)),
    ]
    if emit_carry:
        out_shape += [
            jax.ShapeDtypeStruct((bn, ln, LANES), F32),
            jax.ShapeDtypeStruct((bn, PART_ROWS, ln), F32),
            jax.ShapeDtypeStruct((bn, ln, CONV_W), F32),
        ]
        out_specs += [
            pl.BlockSpec((None, T, LANES), lambda b, t: (b, t, 0)),
            pl.BlockSpec((None, PART_ROWS, T), lambda b, t: (b, 0, t)),
            pl.BlockSpec((None, T, CONV_W), lambda b, t: (b, t, 0)),
        ]
    kern = functools.partial(_proj_kernel, t_rows=T, n_valid=n_valid, emit_carry=emit_carry, tk=tk)
    return pl.pallas_call(
        kern,
        grid=(bn, nt),
        in_specs=in_specs,
        out_specs=out_specs,
        out_shape=out_shape,
        scratch_shapes=[pltpu.VMEM((8, LANES), F32), pltpu.VMEM((PART_ROWS, LANES), F32),
                        pltpu.VMEM((8, CONV_W), F32)],
        compiler_params=pltpu.CompilerParams(dimension_semantics=("arbitrary", "arbitrary"),
                                             vmem_limit_bytes=VMEM_LIMIT),
        name="proj_meta" if emit_carry else "proj",
    )(xs, g, wrow, wt, bfr, bfc, convw, selk, cin_row, cin_col, uin)


def _attn_kernel(q_ref, k_ref, v_ref, km_ref, vm_ref, g_ref, o_ref, s_s, m_s, acc_s):
    i = pl.program_id(1)
    T = ATT_T
    causal = (lax.broadcasted_iota(jnp.int32, (T, T), 0) <= lax.broadcasted_iota(jnp.int32, (T, T), 1))

    def scores(h, kblk, slot, masked):
        qa = q_ref[h * HEAD_SLOT:(h + 1) * HEAD_SLOT, :]
        s = jnp.dot(kblk, qa, preferred_element_type=F32)
        if masked:
            s = jnp.where(causal, s, NEG_BIG)
        s_s[slot, h] = s

    def consume(h, vblk, slot):
        s = s_s[slot, h]
        m_old = m_s[h, 0:1, :]
        m_new = jnp.maximum(m_old, jnp.max(s, axis=0, keepdims=True))
        alpha = jnp.exp2(m_old - m_new)
        pm = jnp.exp2(s - m_new).astype(BF16)
        acc_s[h] = alpha * acc_s[h] + jnp.dot(vblk, pm, preferred_element_type=F32)
        m_s[h, 0:1, :] = m_new

    def step(k_next, next_slot, masked, v_cur, cur_slot):
        ahead = 2
        for h in range(ahead):
            scores(h, k_next(h), next_slot, masked)
        for h in range(N_HEADS):
            consume(h, v_cur(h), cur_slot)
            if h + ahead < N_HEADS:
                scores(h + ahead, k_next(h + ahead), next_slot, masked)

    def k_of(j):
        row = pl.ds(pl.multiple_of(j * T, T), T)
        return lambda h: k_ref[row, h * HEAD_SLOT:(h + 1) * HEAD_SLOT]

    def v_of(j):
        return lambda h: v_ref[j, h * V_SLOT:(h + 1) * V_SLOT, :]

    k_meta = lambda h: km_ref[:, h * HEAD_SLOT:(h + 1) * HEAD_SLOT]
    v_meta = lambda h: vm_ref[h * V_SLOT:(h + 1) * V_SLOT, :]

    m_s[...] = jnp.full(m_s.shape, NEG_BIG, F32)
    acc_s[...] = jnp.zeros(acc_s.shape, F32)
    for h in range(N_HEADS):
        scores(h, k_meta(h), 0, False)

    def v_at(t):
        return lambda h: jnp.where(t == 0, v_meta(h), v_ref[jnp.maximum(t - 1, 0), h * V_SLOT:(h + 1) * V_SLOT, :])

    def pair_body(p, c):
        t0 = 2 * p
        step(k_of(t0), 1, False, v_at(t0), 0)
        step(k_of(t0 + 1), 0, False, v_at(t0 + 1), 1)
        return c

    lax.fori_loop(0, i // 2, pair_body, 0)

    @pl.when(i % 2 == 1)
    def _():
        step(k_of(i - 1), 1, False, v_at(i - 1), 0)
        step(k_of(i), 0, True, v_at(i), 1)
        for h in range(N_HEADS):
            consume(h, v_of(i)(h), 0)

    @pl.when(i % 2 == 0)
    def _():
        step(k_of(i), 1, True, v_at(i), 0)
        for h in range(N_HEADS):
            consume(h, v_of(i)(h), 1)

    for p in range(N_HEADS // 2):
        halves = []
        for h in (2 * p, 2 * p + 1):
            o = acc_s[h, 0:HEAD_DIM, :] * (1.0 / acc_s[h, HEAD_DIM:HEAD_DIM + 1, :])
            halves.append(o * lax.rsqrt(jnp.mean(o * o, axis=0, keepdims=True) + EPS))
        pair = jnp.concatenate(halves, axis=0)
        o_ref[:, p * LANES:(p + 1) * LANES] = (pair.T * g_ref[:, p * LANES:(p + 1) * LANES]).astype(o_ref.dtype)


def _attn_call(qaug, kaug, vt, kmeta, vmeta, gain):
    bn, _, ln = qaug.shape
    T = ATT_T
    nq = ln // T
    return pl.pallas_call(
        _attn_kernel,
        grid=(bn, nq),
        in_specs=[
            pl.BlockSpec((None, AUG_W, T), lambda b, i: (b, 0, i)),
            pl.BlockSpec((None, ln, AUG_W), lambda b, i: (b, 0, 0)),
            pl.BlockSpec((None, ln // T, V_AUG, T), lambda b, i: (b, 0, 0, 0)),
            pl.BlockSpec(kmeta.shape, lambda b, i: (0, 0)),
            pl.BlockSpec(vmeta.shape, lambda b, i: (0, 0)),
            pl.BlockSpec(gain.shape, lambda b, i: (0, 0)),
        ],
        out_specs=pl.BlockSpec((None, T, ATTN_W), lambda b, i: (b, i, 0)),
        out_shape=jax.ShapeDtypeStruct((bn, ln, ATTN_W), BF16),
        scratch_shapes=[pltpu.VMEM((2, N_HEADS, T, T), F32),
                        pltpu.VMEM((N_HEADS, 8, T), F32),
                        pltpu.VMEM((N_HEADS, V_SLOT, T), F32)],
        compiler_params=pltpu.CompilerParams(dimension_semantics=("arbitrary", "arbitrary"),
                                             vmem_limit_bytes=VMEM_LIMIT),
        name="attention",
    )(qaug, kaug, vt, kmeta, vmeta, gain)


def _mix_kernel(attn_ref, conv_ref, x_ref, gc_ref, gsum_ref, woa_ref, woc_ref, fg_ref,
                wrh_ref, br_ref, h_ref, z_ref, rinfo_ref, counts_ref, cnt):
    T = MIX_T

    @pl.when(pl.program_id(0) == 0)
    def _():
        cnt[...] = jnp.zeros_like(cnt)

    c = conv_ref[...]
    c2 = c * c
    c2h = c2.astype(BF16)
    c2l = (c2 - c2h.astype(F32)).astype(BF16)
    ss = (jnp.dot(c2h, gsum_ref[...], preferred_element_type=F32)
          + jnp.dot(c2l, gsum_ref[...], preferred_element_type=F32))
    cn = (c * lax.rsqrt(ss * (1.0 / HEAD_DIM) + EPS) * gc_ref[...]).astype(BF16)
    hadd = (jnp.dot(attn_ref[...], woa_ref[...], preferred_element_type=F32)
            + jnp.dot(cn, woc_ref[...], preferred_element_type=F32))
    h = x_ref[...] + hadd
    h_ref[...] = h
    z = h * lax.rsqrt(jnp.mean(h * h, axis=-1, keepdims=True) + EPS) * fg_ref[...]
    z_ref[...] = z.reshape(T, ROW_SUB, LANES)
    zh = z.astype(BF16)
    zl = (z - zh.astype(F32)).astype(BF16)
    both = jnp.dot(zh, wrh_ref[...], preferred_element_type=F32)
    logits = (both[:, 0:LANES] + both[:, LANES:2 * LANES]
              + jnp.dot(zl, wrh_ref[:, 0:LANES], preferred_element_type=F32)) + br_ref[...]

    lane = lax.broadcasted_iota(jnp.int32, (T, LANES), 1).astype(F32)
    big = float(LANES)
    gl = jnp.where(lane < N_GROUPS, logits, -jnp.inf)
    gmax = jnp.max(gl, axis=-1, keepdims=True)
    gsum = jnp.sum(jnp.exp(gl - gmax), axis=-1, keepdims=True)
    g_w = 1.0 / gsum
    g_idx = jnp.min(jnp.where(gl == gmax, lane, big), axis=-1, keepdims=True)
    e_lo = ROUTER_LANE0 + EXPERTS_PER_GROUP * g_idx
    emask = (lane >= e_lo) & (lane < e_lo + EXPERTS_PER_GROUP)
    el = jnp.where(emask, logits, -jnp.inf)
    emax = jnp.max(el, axis=-1, keepdims=True)
    eexp = jnp.exp(el - emax)
    probs = eexp / jnp.sum(eexp, axis=-1, keepdims=True)
    probs = jnp.where(emask, probs, -1.0)
    p1 = jnp.max(probs, axis=-1, keepdims=True)
    i1 = jnp.min(jnp.where(probs == p1, lane, big), axis=-1, keepdims=True)
    probs2 = jnp.where(lane == i1, -1.0, probs)
    p2 = jnp.max(probs2, axis=-1, keepdims=True)
    i2 = jnp.min(jnp.where(probs2 == p2, lane, big), axis=-1, keepdims=True)
    psum = p1 + p2
    w1 = g_w * (p1 / psum)
    w2 = g_w * (p2 / psum)

    hit = ((lane == i1) | (lane == i2)).astype(F32)
    tri_r = lax.broadcasted_iota(jnp.int32, (T, T), 0)
    tri_c = lax.broadcasted_iota(jnp.int32, (T, T), 1)
    strict_l = (tri_c < tri_r).astype(BF16)
    before = jnp.dot(strict_l, hit.astype(BF16), preferred_element_type=F32) + cnt[0:1, :]
    r1 = jnp.sum(jnp.where(lane == i1, before, 0.0), axis=-1, keepdims=True)
    r2 = jnp.sum(jnp.where(lane == i2, before, 0.0), axis=-1, keepdims=True)
    cnt[0:1, :] = cnt[0:1, :] + jnp.sum(hit, axis=0, keepdims=True)
    counts_ref[...] = cnt[...]

    e1 = i1 - ROUTER_LANE0
    e2 = i2 - ROUTER_LANE0
    rinfo = jnp.where(lane == 0, e1, jnp.where(lane == 1, e2, jnp.where(lane == 2, w1, jnp.where(
        lane == 3, w2, jnp.where(lane == 4, r1, jnp.where(lane == 5, r2, 0.0))))))
    rinfo_ref[...] = rinfo


def _mix_call(attn, conv, x, gc, gsum, woa, woc, fg, wrh, br):
    n = x.shape[0]
    T = MIX_T
    const = lambda i: (0, 0)
    row = lambda w: pl.BlockSpec((T, w), lambda i: (i, 0))
    return pl.pallas_call(
        _mix_kernel,
        grid=(n // T,),
        in_specs=[row(ATTN_W), row(CONV_W), row(D_MODEL)] + [
            pl.BlockSpec(a.shape, const) for a in (gc, gsum, woa, woc, fg, wrh, br)],
        out_specs=[row(D_MODEL), pl.BlockSpec((T, ROW_SUB, LANES), lambda i: (i, 0, 0)), row(LANES),
                   pl.BlockSpec((8, LANES), const)],
        out_shape=[jax.ShapeDtypeStruct((n, D_MODEL), F32), jax.ShapeDtypeStruct((n, ROW_SUB, LANES), F32),
                   jax.ShapeDtypeStruct((n, LANES), F32), jax.ShapeDtypeStruct((8, LANES), F32)],
        scratch_shapes=[pltpu.VMEM((8, LANES), F32)],
        compiler_params=pltpu.CompilerParams(dimension_semantics=("arbitrary",),
                                             vmem_limit_bytes=VMEM_LIMIT),
        name="mix",
    )(attn, conv, x, gc, gsum, woa, woc, fg, wrh, br)


def _dispatch_kernel(pad_end_ref, cnt_ref, dest_ref, z_ref, xbuf_ref, zero_v, sem, zsem):
    T = DISPATCH_T

    @pl.when(pl.program_id(0) == 0)
    def _():
        zero_v[...] = jnp.zeros_like(zero_v)

        def zero_copy(e):
            first = pl.multiple_of(jnp.maximum(pad_end_ref[e] - EXPERT_BM, 0), EXPERT_BM)
            return pltpu.make_async_copy(zero_v, xbuf_ref.at[pl.ds(first, EXPERT_BM)], zsem)

        for e in range(N_EXPERTS):
            @pl.when(cnt_ref[e] > 0)
            def _():
                zero_copy(e).start()
        for e in range(N_EXPERTS):
            @pl.when(cnt_ref[e] > 0)
            def _():
                zero_copy(e).wait()

        def tail_copy(b):
            return pltpu.make_async_copy(
                zero_v, xbuf_ref.at[pl.ds(pl.multiple_of(b * EXPERT_BM, EXPERT_BM), EXPERT_BM)], zsem)

        n_used = pad_end_ref[N_EXPERTS - 1] // EXPERT_BM
        n_all = xbuf_ref.shape[0] // EXPERT_BM
        lax.fori_loop(n_used, n_all, lambda b, c: (tail_copy(b).start(), c)[1], 0)
        lax.fori_loop(n_used, n_all, lambda b, c: (tail_copy(b).wait(), c)[1], 0)

    def row_copy(r, slot):
        d = dest_ref[0, 2 * r + slot]
        return pltpu.make_async_copy(z_ref.at[pl.ds(r, 1)], xbuf_ref.at[pl.ds(d, 1)], sem)

    def start(r, c):
        row_copy(r, 0).start(priority=0)
        row_copy(r, 1).start(priority=1)
        return c

    def wait(r, c):
        row_copy(r, 0).wait()
        row_copy(r, 1).wait()
        return c

    lax.fori_loop(0, T, start, 0, unroll=DMA_UNROLL)
    lax.fori_loop(0, T, wait, 0, unroll=DMA_UNROLL)


def _dispatch_call(pad_end, cnt, dest, z, n_rows):
    n = z.shape[0]
    T = DISPATCH_T
    grid_spec = pltpu.PrefetchScalarGridSpec(
        num_scalar_prefetch=2,
        grid=(n // T,),
        in_specs=[
            pl.BlockSpec((None, 1, 2 * T), lambda i, pe, ct: (i, 0, 0), memory_space=pltpu.SMEM),
            pl.BlockSpec((T, ROW_SUB, LANES), lambda i, pe, ct: (i, 0, 0)),
        ],
        out_specs=pl.BlockSpec(memory_space=pl.ANY),
        scratch_shapes=[pltpu.VMEM((EXPERT_BM, ROW_SUB, LANES), F32), pltpu.SemaphoreType.DMA(()),
                        pltpu.SemaphoreType.DMA(())],
    )
    return pl.pallas_call(
        _dispatch_kernel,
        grid_spec=grid_spec,
        out_shape=jax.ShapeDtypeStruct((n_rows, ROW_SUB, LANES), F32),
        compiler_params=pltpu.CompilerParams(dimension_semantics=("arbitrary",), vmem_limit_bytes=VMEM_LIMIT),
        name="dispatch",
    )(pad_end, cnt, dest, z)


def _expert_kernel(blk_e_ref, nvb_ref, x_ref, wg_ref, wu_ref, wd_ref, y_ref, wgu_s, wd_s):
    blk = pl.program_id(0)
    valid = blk < nvb_ref[0]
    prev_e = blk_e_ref[jnp.maximum(blk - 1, 0)]

    @pl.when(valid & ((blk == 0) | (blk_e_ref[blk] != prev_e)))
    def _():
        wgu_s[:, 0:D_EXPERT] = wg_ref[...].astype(BF16)
        wgu_s[:, D_EXPERT:2 * D_EXPERT] = wu_ref[...].astype(BF16)
        wd_s[...] = wd_ref[...].astype(BF16)

    @pl.when(valid)
    def _():
        xb = x_ref[...].reshape(EXPERT_BM, D_MODEL).astype(BF16)
        gu = jnp.dot(xb, wgu_s[...], preferred_element_type=F32)
        g = gu[:, 0:D_EXPERT]
        u = gu[:, D_EXPERT:2 * D_EXPERT]
        a = (g * jax.nn.sigmoid(g)) * u
        y = jnp.dot(a.astype(BF16), wd_s[...], preferred_element_type=F32)
        y_ref[...] = y.reshape(EXPERT_BM, ROW_SUB, LANES)


def _expert_call(blk_e, nvb, xbuf, w_gate, w_up, w_down):
    rows = xbuf.shape[0]
    bm = EXPERT_BM
    used = lambda i, be, nv: (jnp.minimum(i, nv[0] - 1), 0, 0)
    grid_spec = pltpu.PrefetchScalarGridSpec(
        num_scalar_prefetch=2,
        grid=(rows // bm,),
        in_specs=[
            pl.BlockSpec((bm, ROW_SUB, LANES), used),
            pl.BlockSpec((None, D_MODEL, D_EXPERT), lambda i, be, nv: (be[i], 0, 0)),
            pl.BlockSpec((None, D_MODEL, D_EXPERT), lambda i, be, nv: (be[i], 0, 0)),
            pl.BlockSpec((None, D_EXPERT, D_MODEL), lambda i, be, nv: (be[i], 0, 0)),
        ],
        out_specs=pl.BlockSpec((bm, ROW_SUB, LANES), used),
        scratch_shapes=[pltpu.VMEM((D_MODEL, 2 * D_EXPERT), BF16), pltpu.VMEM((D_EXPERT, D_MODEL), BF16)],
    )
    return pl.pallas_call(
        _expert_kernel,
        grid_spec=grid_spec,
        out_shape=jax.ShapeDtypeStruct((rows, ROW_SUB, LANES), F32),
        input_output_aliases={2: 0},
        compiler_params=pltpu.CompilerParams(dimension_semantics=("arbitrary",),
                                             vmem_limit_bytes=VMEM_LIMIT),
        name="experts",
    )(blk_e, nvb, xbuf, w_gate, w_up, w_down)


def _combine_kernel(dest_ref, dest_next_ref, h_ref, rinfo_ref, fn_ref, ybuf_ref, o_ref, ybuf_v, sem):
    T = COMBINE_T
    i = pl.program_id(0)
    cur = i % 2

    def row_copy(dref, buf, r, k):
        d = dref[0, 2 * r + k]
        return pltpu.make_async_copy(ybuf_ref.at[pl.ds(d, 1)], ybuf_v.at[buf, k, pl.ds(r, 1)], sem.at[buf])

    def issue(dref, buf):
        def start(r, c):
            row_copy(dref, buf, r, 0).start(priority=0)
            row_copy(dref, buf, r, 1).start(priority=1)
            return c
        lax.fori_loop(0, T, start, 0, unroll=DMA_UNROLL)

    @pl.when(i == 0)
    def _():
        issue(dest_ref, 0)

    @pl.when(i + 1 < pl.num_programs(0))
    def _():
        issue(dest_next_ref, 1 - cur)

    def wait(r, c):
        row_copy(dest_ref, cur, r, 0).wait()
        row_copy(dest_ref, cur, r, 1).wait()
        return c

    lax.fori_loop(0, T, wait, 0, unroll=DMA_UNROLL)
    rinfo = rinfo_ref[...]
    w1 = rinfo[:, 2:3]
    w2 = rinfo[:, 3:4]
    y1 = ybuf_v[cur, 0].reshape(T, D_MODEL)
    y2 = ybuf_v[cur, 1].reshape(T, D_MODEL)
    h = h_ref[...] + (y1 * w1 + y2 * w2)
    o_ref[...] = h * lax.rsqrt(jnp.mean(h * h, axis=-1, keepdims=True) + EPS) * fn_ref[...]


def _combine_call(dest, h, rinfo, fn, ybuf):
    n = h.shape[0]
    T = COMBINE_T
    nt = n // T
    return pl.pallas_call(
        _combine_kernel,
        grid=(nt,),
        in_specs=[
            pl.BlockSpec((None, 1, 2 * T), lambda i: (i, 0, 0), memory_space=pltpu.SMEM),
            pl.BlockSpec((None, 1, 2 * T), lambda i: (jnp.minimum(i + 1, nt - 1), 0, 0), memory_space=pltpu.SMEM),
            pl.BlockSpec((T, D_MODEL), lambda i: (i, 0)),
            pl.BlockSpec((T, LANES), lambda i: (i, 0)),
            pl.BlockSpec(fn.shape, lambda i: (0, 0)),
            pl.BlockSpec(memory_space=pl.ANY),
        ],
        out_specs=pl.BlockSpec((T, D_MODEL), lambda i: (i, 0)),
        out_shape=jax.ShapeDtypeStruct((n, D_MODEL), F32),
        scratch_shapes=[pltpu.VMEM((2, 2, T, ROW_SUB, LANES), F32), pltpu.SemaphoreType.DMA((2,))],
        compiler_params=pltpu.CompilerParams(dimension_semantics=("arbitrary",),
                                             vmem_limit_bytes=VMEM_LIMIT),
        name="combine",
    )(dest, dest, h, rinfo, fn, ybuf)


def kernel(x, meta_tokens, attn_norm, w_in, b_forget, conv_w, mix_norm, w_out, ffn_norm,
           w_router_group, b_router_group, w_router_expert, b_router_expert,
           w_gate, w_up, w_down, final_norm):
    bsz, seq, d = x.shape
    assert d == D_MODEL and seq % PROJ_T == 0 and meta_tokens.shape[0] == N_META
    assert attn_norm.shape[0] == 1, "single-layer block"
    n_tok = bsz * seq

    wi = w_in[0]
    o = 0
    w_q = wi[:, o:o + ATTN_W]; o += ATTN_W
    w_k = wi[:, o:o + ATTN_W]; o += ATTN_W
    w_v = wi[:, o:o + ATTN_W]; o += ATTN_W
    w_f = wi[:, o:o + N_HEADS]; o += N_HEADS
    w_xc = wi[:, o:o + CONV_W]; o += CONV_W
    w_bg = wi[:, o:o + CONV_W]; o += CONV_W
    w_cg = wi[:, o:o + CONV_W]
    wrow = jnp.concatenate([w_k, w_xc, w_bg, w_cg, w_f, jnp.zeros((D_MODEL, LANES - N_HEADS), F32)],
                           axis=1).astype(BF16)
    wt = jnp.concatenate([w_q.T * (HEAD_DIM ** -0.5), w_v.T, w_f.T,
                          jnp.zeros((PART_ROWS - N_HEADS, D_MODEL), F32)], axis=0).astype(BF16)
    g_attn = attn_norm[0].reshape(1, D_MODEL)
    bfr = jnp.zeros((1, LANES), F32).at[0, :N_HEADS].set(b_forget[0])
    bfc = jnp.zeros((PART_ROWS, LANES), F32).at[:N_HEADS, :].set(b_forget[0][:, None])
    convw = jnp.zeros((8, CONV_W), F32).at[:3].set(conv_w[0])
    selk = _selection_matrix()

    meta_pad = jnp.zeros((1, META_PAD, D_MODEL), F32).at[0, :N_META].set(meta_tokens)
    zrow = jnp.zeros((8, LANES), F32)
    zcol = jnp.zeros((PART_ROWS, LANES), F32)
    zu = jnp.zeros((8, CONV_W), F32)
    kmeta, _, vmeta, _, cumrow_m, cumt_m, u_m = _proj_call(
        meta_pad, g_attn, wrow, wt, bfr, bfc, convw, selk, zrow, zcol, zu,
        t_rows=META_PAD, n_valid=N_META, emit_carry=True)
    cin_row = jnp.zeros((8, LANES), F32).at[0].set(cumrow_m[0, N_META - 1])
    cin_col = jnp.broadcast_to(cumt_m[0, :, N_META - 1:N_META], (PART_ROWS, LANES))
    uin = u_m[0, N_META - 8:N_META]

    kaug, qaug, vt, conv = _proj_call(
        x, g_attn, wrow, wt, bfr, bfc, convw, selk, cin_row, cin_col, uin,
        t_rows=PROJ_T, n_valid=PROJ_T, emit_carry=False)

    mixg = mix_norm[0]
    attn = _attn_call(qaug, kaug, vt, kmeta[0], vmeta[0, 0], mixg[:ATTN_W].reshape(1, ATTN_W))

    gidx = np.arange(CONV_W) // HEAD_DIM
    gsum = jnp.asarray(gidx[:, None] == gidx[None, :], BF16)
    wo = w_out[0].astype(BF16)
    wr = jnp.zeros((D_MODEL, LANES), F32)
    wr = wr.at[:, :N_GROUPS].set(w_router_group[0]).at[:, ROUTER_LANE0:ROUTER_LANE0 + N_EXPERTS].set(
        w_router_expert[0])
    wr_hi = wr.astype(BF16)
    wrh = jnp.concatenate([wr_hi, (wr - wr_hi.astype(F32)).astype(BF16)], axis=1)
    br = jnp.zeros((1, LANES), F32)
    br = br.at[0, :N_GROUPS].set(b_router_group[0]).at[0, ROUTER_LANE0:ROUTER_LANE0 + N_EXPERTS].set(
        b_router_expert[0])
    h, z, rinfo, counts = _mix_call(
        attn.reshape(n_tok, ATTN_W), conv.reshape(n_tok, CONV_W), x.reshape(n_tok, D_MODEL),
        mixg[ATTN_W:].reshape(1, CONV_W), gsum, wo[:ATTN_W], wo[ATTN_W:], ffn_norm[0].reshape(1, D_MODEL),
        wrh, br)

    cnt = counts[0, ROUTER_LANE0:ROUTER_LANE0 + N_EXPERTS].astype(jnp.int32)
    padded = (cnt + EXPERT_BM - 1) // EXPERT_BM * EXPERT_BM
    pad_end = jnp.cumsum(padded)
    pad_start = pad_end - padded
    eid = rinfo[:, 0:2].astype(jnp.int32)
    rank = rinfo[:, 4:6].astype(jnp.int32)
    expert_ids = jnp.arange(N_EXPERTS, dtype=jnp.int32)
    dest = rank + jnp.sum(jnp.where(eid[:, :, None] == expert_ids, pad_start, 0), axis=-1)
    dest = dest.reshape(n_tok // DISPATCH_T, 1, 2 * DISPATCH_T)
    n_blk = (2 * n_tok) // EXPERT_BM + N_EXPERTS
    nvb = (pad_end[-1] // EXPERT_BM).reshape(1)
    blk_row = jnp.minimum(jnp.arange(n_blk, dtype=jnp.int32), nvb - 1) * EXPERT_BM
    blk_e = jnp.sum(blk_row[:, None] >= pad_end[None, :], axis=1).astype(jnp.int32)

    xbuf = _dispatch_call(pad_end, cnt, dest, z, n_blk * EXPERT_BM)
    ybuf = _expert_call(blk_e, nvb, xbuf, w_gate[0], w_up[0], w_down[0])
    out = _combine_call(dest, h, rinfo, final_norm.reshape(1, D_MODEL), ybuf)
    return out.reshape(bsz, seq, D_MODEL)
```

```python
import functools

import numpy as np
import jax
import jax.numpy as jnp
from jax import lax
from jax.experimental import pallas as pl
from jax.experimental.pallas import tpu as pltpu

F32 = jnp.float32
BF16 = jnp.bfloat16

D_MODEL = 1024
HEAD_DIM = 64
N_HEADS = 8
ATTN_W = N_HEADS * HEAD_DIM
CONV_W = 512
N_META = 16
N_GROUPS = 4
EXPERTS_PER_GROUP = 8
N_EXPERTS = N_GROUPS * EXPERTS_PER_GROUP
D_EXPERT = 512
EPS = 1e-6
NEG_BIG = -1e30
LOG2E = 1.4426950408889634

LANES = 128
ROW_SUB = D_MODEL // LANES
HEAD_SLOT = 2 * HEAD_DIM
V_SLOT = HEAD_DIM + 16
V_AUG = N_HEADS * V_SLOT
AUG_W = N_HEADS * HEAD_SLOT
N_PARTS = 3
PART_ROWS = 16
ONES_LANE = 8
ROUTER_LANE0 = N_GROUPS

PROJ_T = 512
ATT_T = 256
META_PAD = ATT_T
MIX_T = 512
DISPATCH_T = 512
EXPERT_BM = 512
COMBINE_T = 512
DMA_UNROLL = 8
VMEM_LIMIT = 48 * 1024 * 1024


def _split3(x):
    hi = x.astype(BF16)
    r1 = x - hi.astype(F32)
    mid = r1.astype(BF16)
    r2 = r1 - mid.astype(F32)
    return hi, mid, r2.astype(BF16)


def _log_sigmoid(x):
    return jnp.minimum(x, 0.0) - jnp.log1p(jnp.exp(-jnp.abs(x)))


def _head_offset(h):
    return HEAD_DIM if h % 2 == 0 else 0


def _bias_slot(j, h):
    return N_PARTS + N_HEADS * j + h


def _selection_matrix():
    selk = np.zeros((N_PARTS * LANES, 2 * LANES), np.float32)
    for base in (_head_offset(0), LANES + _head_offset(1)):
        for j in range(N_PARTS):
            selk[ONES_LANE, base + j] = 1.0
            for h in range(N_HEADS):
                selk[j * LANES + h, base + _bias_slot(j, h)] = 1.0
    return jnp.asarray(selk, BF16)


def _proj_kernel(x_ref, g_ref, wrow_ref, wt_ref, bfr_ref, bfc_ref, convw_ref, selk_ref,
                 cin_row_ref, cin_col_ref, uin_ref,
                 kaug_ref, qaug_ref, vt_ref, conv_ref, *rest, t_rows, n_valid, emit_carry, tk):
    if emit_carry:
        cumrow_out, cumt_out, u_out, c_row, c_col, u_prev = rest
    else:
        c_row, c_col, u_prev = rest

    @pl.when(pl.program_id(1) == 0)
    def _():
        c_row[...] = cin_row_ref[...]
        c_col[...] = cin_col_ref[...]
        u_prev[...] = uin_ref[...]

    T = t_rows
    x = x_ref[...]
    ms = jnp.mean(x * x, axis=-1, keepdims=True)
    z = x * lax.rsqrt(ms + EPS) * g_ref[...]
    zb = z.astype(BF16)
    r = jnp.dot(zb, wrow_ref[...], preferred_element_type=F32)
    tt = lax.dot_general(wt_ref[...], zb, (((1,), (1,)), ((), ())),
                         preferred_element_type=F32)
    k = r[:, 0:ATTN_W]
    xc = r[:, ATTN_W:ATTN_W + CONV_W]
    bg = r[:, ATTN_W + CONV_W:ATTN_W + 2 * CONV_W]
    cg = r[:, ATTN_W + 2 * CONV_W:ATTN_W + 3 * CONV_W]
    fr = r[:, ATTN_W + 3 * CONV_W:ATTN_W + 3 * CONV_W + LANES]
    qt = tt[0:ATTN_W]
    vt = tt[ATTN_W:2 * ATTN_W]
    ft = tt[2 * ATTN_W:2 * ATTN_W + PART_ROWS]

    lane_r = lax.broadcasted_iota(jnp.int32, (T, LANES), 1)
    row_r = lax.broadcasted_iota(jnp.int32, (T, LANES), 0)
    lfr = jnp.where(lane_r < N_HEADS, _log_sigmoid(fr + bfr_ref[...]), 0.0)
    lfr = lfr + jnp.where(row_r == 0, c_row[0:1, :], 0.0)
    tri_r = lax.broadcasted_iota(jnp.int32, (T, T), 0)
    tri_c = lax.broadcasted_iota(jnp.int32, (T, T), 1)
    tri_l = (tri_c <= tri_r).astype(BF16)
    tri_u = (tri_r <= tri_c).astype(BF16)
    c3 = jnp.dot(tri_l, jnp.concatenate(_split3(lfr), axis=1), preferred_element_type=F32)
    cum_row = c3[:, 0:LANES] + c3[:, LANES:2 * LANES] + c3[:, 2 * LANES:3 * LANES]

    kb = cum_row * (-LOG2E)
    if n_valid < T:
        kb = kb + jnp.where(row_r >= n_valid, NEG_BIG, 0.0)
    kb = jnp.where(lane_r == ONES_LANE, 1.0, kb)
    e = jnp.dot(jnp.concatenate(_split3(kb), axis=1), selk_ref[...],
                preferred_element_type=F32)
    e_even = e[:, 0:LANES].astype(BF16)
    e_odd = e[:, LANES:2 * LANES].astype(BF16)
    for p in range(N_HEADS // 2):
        kp = k[:, p * LANES:(p + 1) * LANES].astype(BF16)
        kaug_ref[:, 2 * p * LANES:(2 * p + 1) * LANES] = jnp.where(lane_r < HEAD_DIM, kp, e_even)
        kaug_ref[:, (2 * p + 1) * LANES:(2 * p + 2) * LANES] = jnp.where(lane_r >= HEAD_DIM, kp, e_odd)

    lane_c = lax.broadcasted_iota(jnp.int32, (PART_ROWS, T), 1)
    row_c = lax.broadcasted_iota(jnp.int32, (PART_ROWS, T), 0)
    lfc = jnp.where(row_c < N_HEADS, _log_sigmoid(ft + bfc_ref[:, 0:1]), 0.0)
    lfc = lfc + jnp.where(lane_c == 0, c_col[:, 0:1], 0.0)
    c3c = jnp.dot(jnp.concatenate(_split3(lfc), axis=0), tri_u, preferred_element_type=F32)
    cum_t = c3c[0:PART_ROWS] + c3c[PART_ROWS:2 * PART_ROWS] + c3c[2 * PART_ROWS:3 * PART_ROWS]
    cq_parts = [p.astype(F32) for p in _split3(cum_t * LOG2E)]
    bias_rows = HEAD_DIM // 2
    row_b = lax.broadcasted_iota(jnp.int32, (bias_rows, T), 0)
    zero_rows = jnp.zeros((HEAD_DIM - bias_rows, T), BF16)
    for h in range(N_HEADS):
        qh = (qt[h * HEAD_DIM:(h + 1) * HEAD_DIM] * LOG2E).astype(BF16)
        own = (row_b == _bias_slot(0, h)) | (row_b == _bias_slot(1, h)) | (row_b == _bias_slot(2, h))
        bias = jnp.where(own, 1.0, 0.0)
        for j in range(N_PARTS):
            bias = jnp.where(row_b == j, cq_parts[j][h:h + 1, :], bias)
        lo = h * HEAD_SLOT + _head_offset(h)
        qaug_ref[lo:lo + bias_rows, :] = bias.astype(BF16)
        qaug_ref[lo + bias_rows:lo + HEAD_DIM, :] = zero_rows
        lo_q = h * HEAD_SLOT + (HEAD_DIM - _head_offset(h))
        qaug_ref[lo_q:lo_q + HEAD_DIM, :] = qh

    vtb = vt.astype(BF16)
    ones_rows = jnp.ones((V_SLOT - HEAD_DIM, tk), BF16)
    for s in range(T // tk):
        for h in range(N_HEADS):
            vt_ref[s, h * V_SLOT:h * V_SLOT + HEAD_DIM, :] = vtb[h * HEAD_DIM:(h + 1) * HEAD_DIM, s * tk:(s + 1) * tk]
            vt_ref[s, h * V_SLOT + HEAD_DIM:(h + 1) * V_SLOT, :] = ones_rows

    u = cg * xc
    row_u = lax.broadcasted_iota(jnp.int32, (T, CONV_W), 0)
    p1 = u_prev[7:8, :]
    p2 = u_prev[6:7, :]
    u1 = jnp.where(row_u == 0, p1, pltpu.roll(u, 1, axis=0))
    u2 = jnp.where(row_u == 0, p2, jnp.where(row_u == 1, p1, pltpu.roll(u, 2, axis=0)))
    zc = convw_ref[0:1, :] * u2 + convw_ref[1:2, :] * u1 + convw_ref[2:3, :] * u
    conv_ref[...] = bg * zc

    if emit_carry:
        cumrow_out[...] = cum_row
        cumt_out[...] = cum_t
        u_out[...] = u
    c_row[0:1, :] = cum_row[T - 1:T, :]
    c_col[...] = jnp.broadcast_to(cum_t[:, T - 1:T], (PART_ROWS, LANES))
    u_prev[...] = u[T - 8:T, :]


def _proj_call(xs, g, wrow, wt, bfr, bfc, convw, selk, cin_row, cin_col, uin,
               *, t_rows, n_valid, emit_carry):
    bn, ln, _ = xs.shape
    T = t_rows
    tk = min(ATT_T, T)
    nt = ln // T
    const2 = lambda b, t: (0, 0)
    in_specs = [
        pl.BlockSpec((None, T, D_MODEL), lambda b, t: (b, t, 0)),
        pl.BlockSpec(g.shape, const2),
        pl.BlockSpec(wrow.shape, const2),
        pl.BlockSpec(wt.shape, const2),
        pl.BlockSpec(bfr.shape, const2),
        pl.BlockSpec(bfc.shape, const2),
        pl.BlockSpec(convw.shape, const2),
        pl.BlockSpec(selk.shape, const2),
        pl.BlockSpec(cin_row.shape, const2),
        pl.BlockSpec(cin_col.shape, const2),
        pl.BlockSpec(uin.shape, const2),
    ]
    out_shape = [
        jax.ShapeDtypeStruct((bn, ln, AUG_W), BF16),
        jax.ShapeDtypeStruct((bn, AUG_W, ln), BF16),
        jax.ShapeDtypeStruct((bn, ln // tk, V_AUG, tk), BF16),
        jax.ShapeDtypeStruct((bn, ln, CONV_W), F32),
    ]
    out_specs = [
        pl.BlockSpec((None, T, AUG_W), lambda b, t: (b, t, 0)),
        pl.BlockSpec((None, AUG_W, T), lambda b, t: (b, 0, t)),
        pl.BlockSpec((None, T // tk, V_AUG, tk), lambda b, t: (b, t, 0, 0)),
        pl.BlockSpec((None, T, CONV_W), lambda b, t: (b, t, 0)),
    ]
    if emit_carry:
        out_shape += [
            jax.ShapeDtypeStruct((bn, ln, LANES), F32),
            jax.ShapeDtypeStruct((bn, PART_ROWS, ln), F32),
            jax.ShapeDtypeStruct((bn, ln, CONV_W), F32),
        ]
        out_specs += [
            pl.BlockSpec((None, T, LANES), lambda b, t: (b, t, 0)),
            pl.BlockSpec((None, PART_ROWS, T), lambda b, t: (b, 0, t)),
            pl.BlockSpec((None, T, CONV_W), lambda b, t: (b, t, 0)),
        ]
    kern = functools.partial(_proj_kernel, t_rows=T, n_valid=n_valid, emit_carry=emit_carry, tk=tk)
    return pl.pallas_call(
        kern,
        grid=(bn, nt),
        in_specs=in_specs,
        out_specs=out_specs,
        out_shape=out_shape,
        scratch_shapes=[pltpu.VMEM((8, LANES), F32), pltpu.VMEM((PART_ROWS, LANES), F32),
                        pltpu.VMEM((8, CONV_W), F32)],
        compiler_params=pltpu.CompilerParams(dimension_semantics=("arbitrary", "arbitrary"),
                                             vmem_limit_bytes=VMEM_LIMIT),
        name="proj_meta" if emit_carry else "proj",
    )(xs, g, wrow, wt, bfr, bfc, convw, selk, cin_row, cin_col, uin)


def _attn_kernel(q_ref, k_ref, v_ref, km_ref, vm_ref, g_ref, o_ref, s_s, m_s, acc_s):
    i = pl.program_id(1)
    T = ATT_T
    causal = (lax.broadcasted_iota(jnp.int32, (T, T), 0) <= lax.broadcasted_iota(jnp.int32, (T, T), 1))

    def scores(h, kblk, slot, masked):
        qa = q_ref[h * HEAD_SLOT:(h + 1) * HEAD_SLOT, :]
        s = jnp.dot(kblk, qa, preferred_element_type=F32)
        if masked:
            s = jnp.where(causal, s, NEG_BIG)
        s_s[slot, h] = s

    def consume(h, vblk, slot):
        s = s_s[slot, h]
        m_old = m_s[h, 0:1, :]
        m_new = jnp.maximum(m_old, jnp.max(s, axis=0, keepdims=True))
        alpha = jnp.exp2(m_old - m_new)
        pm = jnp.exp2(s - m_new).astype(BF16)
        acc_s[h] = alpha * acc_s[h] + jnp.dot(vblk, pm, preferred_element_type=F32)
        m_s[h, 0:1, :] = m_new

    def step(k_next, next_slot, masked, v_cur, cur_slot):
        ahead = 2
        for h in range(ahead):
            scores(h, k_next(h), next_slot, masked)
        for h in range(N_HEADS):
            consume(h, v_cur(h), cur_slot)
            if h + ahead < N_HEADS:
                scores(h + ahead, k_next(h + ahead), next_slot, masked)

    def k_of(j):
        row = pl.ds(pl.multiple_of(j * T, T), T)
        return lambda h: k_ref[row, h * HEAD_SLOT:(h + 1) * HEAD_SLOT]

    def v_of(j):
        return lambda h: v_ref[j, h * V_SLOT:(h + 1) * V_SLOT, :]

    k_meta = lambda h: km_ref[:, h * HEAD_SLOT:(h + 1) * HEAD_SLOT]
    v_meta = lambda h: vm_ref[h * V_SLOT:(h + 1) * V_SLOT, :]

    m_s[...] = jnp.full(m_s.shape, NEG_BIG, F32)
    acc_s[...] = jnp.zeros(acc_s.shape, F32)
    for h in range(N_HEADS):
        scores(h, k_meta(h), 0, False)

    def v_at(t):
        return lambda h: jnp.where(t == 0, v_meta(h), v_ref[jnp.maximum(t - 1, 0), h * V_SLOT:(h + 1) * V_SLOT, :])

    def pair_body(p, c):
        t0 = 2 * p
        step(k_of(t0), 1, False, v_at(t0), 0)
        step(k_of(t0 + 1), 0, False, v_at(t0 + 1), 1)
        return c

    lax.fori_loop(0, i // 2, pair_body, 0)

    @pl.when(i % 2 == 1)
    def _():
        step(k_of(i - 1), 1, False, v_at(i - 1), 0)
        step(k_of(i), 0, True, v_at(i), 1)
        for h in range(N_HEADS):
            consume(h, v_of(i)(h), 0)

    @pl.when(i % 2 == 0)
    def _():
        step(k_of(i), 1, True, v_at(i), 0)
        for h in range(N_HEADS):
            consume(h, v_of(i)(h), 1)

    for p in range(N_HEADS // 2):
        halves = []
        for h in (2 * p, 2 * p + 1):
            o = acc_s[h, 0:HEAD_DIM, :] * (1.0 / acc_s[h, HEAD_DIM:HEAD_DIM + 1, :])
            halves.append(o * lax.rsqrt(jnp.mean(o * o, axis=0, keepdims=True) + EPS))
        pair = jnp.concatenate(halves, axis=0)
        o_ref[:, p * LANES:(p + 1) * LANES] = (pair.T * g_ref[:, p * LANES:(p + 1) * LANES]).astype(o_ref.dtype)


def _attn_call(qaug, kaug, vt, kmeta, vmeta, gain):
    bn, _, ln = qaug.shape
    T = ATT_T
    nq = ln // T
    return pl.pallas_call(
        _attn_kernel,
        grid=(bn, nq),
        in_specs=[
            pl.BlockSpec((None, AUG_W, T), lambda b, i: (b, 0, i)),
            pl.BlockSpec((None, ln, AUG_W), lambda b, i: (b, 0, 0)),
            pl.BlockSpec((None, ln // T, V_AUG, T), lambda b, i: (b, 0, 0, 0)),
            pl.BlockSpec(kmeta.shape, lambda b, i: (0, 0)),
            pl.BlockSpec(vmeta.shape, lambda b, i: (0, 0)),
            pl.BlockSpec(gain.shape, lambda b, i: (0, 0)),
        ],
        out_specs=pl.BlockSpec((None, T, ATTN_W), lambda b, i: (b, i, 0)),
        out_shape=jax.ShapeDtypeStruct((bn, ln, ATTN_W), BF16),
        scratch_shapes=[pltpu.VMEM((2, N_HEADS, T, T), F32),
                        pltpu.VMEM((N_HEADS, 8, T), F32),
                        pltpu.VMEM((N_HEADS, V_SLOT, T), F32)],
        compiler_params=pltpu.CompilerParams(dimension_semantics=("arbitrary", "arbitrary"),
                                             vmem_limit_bytes=VMEM_LIMIT),
        name="attention",
    )(qaug, kaug, vt, kmeta, vmeta, gain)


def _mix_kernel(attn_ref, conv_ref, x_ref, gc_ref, gsum_ref, woa_ref, woc_ref, fg_ref,
                wrh_ref, br_ref, h_ref, z_ref, rinfo_ref, counts_ref, cnt):
    T = MIX_T

    @pl.when(pl.program_id(0) == 0)
    def _():
        cnt[...] = jnp.zeros_like(cnt)

    c = conv_ref[...]
    c2 = c * c
    c2h = c2.astype(BF16)
    c2l = (c2 - c2h.astype(F32)).astype(BF16)
    ss = (jnp.dot(c2h, gsum_ref[...], preferred_element_type=F32)
          + jnp.dot(c2l, gsum_ref[...], preferred_element_type=F32))
    cn = (c * lax.rsqrt(ss * (1.0 / HEAD_DIM) + EPS) * gc_ref[...]).astype(BF16)
    hadd = (jnp.dot(attn_ref[...], woa_ref[...], preferred_element_type=F32)
            + jnp.dot(cn, woc_ref[...], preferred_element_type=F32))
    h = x_ref[...] + hadd
    h_ref[...] = h
    z = h * lax.rsqrt(jnp.mean(h * h, axis=-1, keepdims=True) + EPS) * fg_ref[...]
    z_ref[...] = z.reshape(T, ROW_SUB, LANES)
    zh = z.astype(BF16)
    zl = (z - zh.astype(F32)).astype(BF16)
    both = jnp.dot(zh, wrh_ref[...], preferred_element_type=F32)
    logits = (both[:, 0:LANES] + both[:, LANES:2 * LANES]
              + jnp.dot(zl, wrh_ref[:, 0:LANES], preferred_element_type=F32)) + br_ref[...]

    lane = lax.broadcasted_iota(jnp.int32, (T, LANES), 1).astype(F32)
    big = float(LANES)
    gl = jnp.where(lane < N_GROUPS, logits, -jnp.inf)
    gmax = jnp.max(gl, axis=-1, keepdims=True)
    gsum = jnp.sum(jnp.exp(gl - gmax), axis=-1, keepdims=True)
    g_w = 1.0 / gsum
    g_idx = jnp.min(jnp.where(gl == gmax, lane, big), axis=-1, keepdims=True)
    e_lo = ROUTER_LANE0 + EXPERTS_PER_GROUP * g_idx
    emask = (lane >= e_lo) & (lane < e_lo + EXPERTS_PER_GROUP)
    el = jnp.where(emask, logits, -jnp.inf)
    emax = jnp.max(el, axis=-1, keepdims=True)
    eexp = jnp.exp(el - emax)
    probs = eexp / jnp.sum(eexp, axis=-1, keepdims=True)
    probs = jnp.where(emask, probs, -1.0)
    p1 = jnp.max(probs, axis=-1, keepdims=True)
    i1 = jnp.min(jnp.where(probs == p1, lane, big), axis=-1, keepdims=True)
    probs2 = jnp.where(lane == i1, -1.0, probs)
    p2 = jnp.max(probs2, axis=-1, keepdims=True)
    i2 = jnp.min(jnp.where(probs2 == p2, lane, big), axis=-1, keepdims=True)
    psum = p1 + p2
    w1 = g_w * (p1 / psum)
    w2 = g_w * (p2 / psum)

    hit = ((lane == i1) | (lane == i2)).astype(F32)
    tri_r = lax.broadcasted_iota(jnp.int32, (T, T), 0)
    tri_c = lax.broadcasted_iota(jnp.int32, (T, T), 1)
    strict_l = (tri_c < tri_r).astype(BF16)
    before = jnp.dot(strict_l, hit.astype(BF16), preferred_element_type=F32) + cnt[0:1, :]
    r1 = jnp.sum(jnp.where(lane == i1, before, 0.0), axis=-1, keepdims=True)
    r2 = jnp.sum(jnp.where(lane == i2, before, 0.0), axis=-1, keepdims=True)
    cnt[0:1, :] = cnt[0:1, :] + jnp.sum(hit, axis=0, keepdims=True)
    counts_ref[...] = cnt[...]

    e1 = i1 - ROUTER_LANE0
    e2 = i2 - ROUTER_LANE0
    rinfo = jnp.where(lane == 0, e1, jnp.where(lane == 1, e2, jnp.where(lane == 2, w1, jnp.where(
        lane == 3, w2, jnp.where(lane == 4, r1, jnp.where(lane == 5, r2, 0.0))))))
    rinfo_ref[...] = rinfo


def _mix_call(attn, conv, x, gc, gsum, woa, woc, fg, wrh, br):
    n = x.shape[0]
    T = MIX_T
    const = lambda i: (0, 0)
    row = lambda w: pl.BlockSpec((T, w), lambda i: (i, 0))
    return pl.pallas_call(
        _mix_kernel,
        grid=(n // T,),
        in_specs=[row(ATTN_W), row(CONV_W), row(D_MODEL)] + [
            pl.BlockSpec(a.shape, const) for a in (gc, gsum, woa, woc, fg, wrh, br)],
        out_specs=[row(D_MODEL), pl.BlockSpec((T, ROW_SUB, LANES), lambda i: (i, 0, 0)), row(LANES),
                   pl.BlockSpec((8, LANES), const)],
        out_shape=[jax.ShapeDtypeStruct((n, D_MODEL), F32), jax.ShapeDtypeStruct((n, ROW_SUB, LANES), F32),
                   jax.ShapeDtypeStruct((n, LANES), F32), jax.ShapeDtypeStruct((8, LANES), F32)],
        scratch_shapes=[pltpu.VMEM((8, LANES), F32)],
        compiler_params=pltpu.CompilerParams(dimension_semantics=("arbitrary",),
                                             vmem_limit_bytes=VMEM_LIMIT),
        name="mix",
    )(attn, conv, x, gc, gsum, woa, woc, fg, wrh, br)


def _dispatch_kernel(pad_end_ref, cnt_ref, dest_ref, z_ref, xbuf_ref, zero_v, sem, zsem):
    T = DISPATCH_T

    @pl.when(pl.program_id(0) == 0)
    def _():
        zero_v[...] = jnp.zeros_like(zero_v)

        def zero_copy(e):
            first = pl.multiple_of(jnp.maximum(pad_end_ref[e] - EXPERT_BM, 0), EXPERT_BM)
            return pltpu.make_async_copy(zero_v, xbuf_ref.at[pl.ds(first, EXPERT_BM)], zsem)

        for e in range(N_EXPERTS):
            @pl.when(cnt_ref[e] > 0)
            def _():
                zero_copy(e).start()
        for e in range(N_EXPERTS):
            @pl.when(cnt_ref[e] > 0)
            def _():
                zero_copy(e).wait()

        def tail_copy(b):
            return pltpu.make_async_copy(
                zero_v, xbuf_ref.at[pl.ds(pl.multiple_of(b * EXPERT_BM, EXPERT_BM), EXPERT_BM)], zsem)

        n_used = pad_end_ref[N_EXPERTS - 1] // EXPERT_BM
        n_all = xbuf_ref.shape[0] // EXPERT_BM
        lax.fori_loop(n_used, n_all, lambda b, c: (tail_copy(b).start(), c)[1], 0)
        lax.fori_loop(n_used, n_all, lambda b, c: (tail_copy(b).wait(), c)[1], 0)

    def row_copy(r, slot):
        d = dest_ref[0, 2 * r + slot]
        return pltpu.make_async_copy(z_ref.at[pl.ds(r, 1)], xbuf_ref.at[pl.ds(d, 1)], sem)

    def start(r, c):
        row_copy(r, 0).start(priority=0)
        row_copy(r, 1).start(priority=1)
        return c

    def wait(r, c):
        row_copy(r, 0).wait()
        row_copy(r, 1).wait()
        return c

    lax.fori_loop(0, T, start, 0, unroll=DMA_UNROLL)
    lax.fori_loop(0, T, wait, 0, unroll=DMA_UNROLL)


def _dispatch_call(pad_end, cnt, dest, z, n_rows):
    n = z.shape[0]
    T = DISPATCH_T
    grid_spec = pltpu.PrefetchScalarGridSpec(
        num_scalar_prefetch=2,
        grid=(n // T,),
        in_specs=[
            pl.BlockSpec((None, 1, 2 * T), lambda i, pe, ct: (i, 0, 0), memory_space=pltpu.SMEM),
            pl.BlockSpec((T, ROW_SUB, LANES), lambda i, pe, ct: (i, 0, 0)),
        ],
        out_specs=pl.BlockSpec(memory_space=pl.ANY),
        scratch_shapes=[pltpu.VMEM((EXPERT_BM, ROW_SUB, LANES), F32), pltpu.SemaphoreType.DMA(()),
                        pltpu.SemaphoreType.DMA(())],
    )
    return pl.pallas_call(
        _dispatch_kernel,
        grid_spec=grid_spec,
        out_shape=jax.ShapeDtypeStruct((n_rows, ROW_SUB, LANES), F32),
        compiler_params=pltpu.CompilerParams(dimension_semantics=("arbitrary",), vmem_limit_bytes=VMEM_LIMIT),
        name="dispatch",
    )(pad_end, cnt, dest, z)


def _expert_kernel(blk_e_ref, nvb_ref, first_ref, slot_ref, next_ref, x_ref, wg_hbm, wu_hbm, wd_hbm, y_ref,
                   wg_v, wu_v, wd_v, wgu_s, wd_s, sem):
    blk = pl.program_id(0)
    valid = blk < nvb_ref[0]

    def fetch(e, s):
        return (pltpu.make_async_copy(wg_hbm.at[e], wg_v.at[s], sem.at[s]),
                pltpu.make_async_copy(wu_hbm.at[e], wu_v.at[s], sem.at[s]),
                pltpu.make_async_copy(wd_hbm.at[e], wd_v.at[s], sem.at[s]))

    @pl.when(blk == 0)
    def _():
        for c in fetch(blk_e_ref[0], 0):
            c.start()

    @pl.when(valid & (first_ref[blk] != 0))
    def _():
        s = slot_ref[blk]
        for c in fetch(blk_e_ref[blk], s):
            c.wait()
        wgu_s[:, 0:D_EXPERT] = wg_v[s].astype(BF16)
        wgu_s[:, D_EXPERT:2 * D_EXPERT] = wu_v[s].astype(BF16)
        wd_s[...] = wd_v[s].astype(BF16)

        @pl.when(next_ref[blk] >= 0)
        def _():
            for c in fetch(next_ref[blk], 1 - s):
                c.start()

    @pl.when(valid)
    def _():
        xb = x_ref[...].reshape(EXPERT_BM, D_MODEL).astype(BF16)
        gu = jnp.dot(xb, wgu_s[...], preferred_element_type=F32)
        g = gu[:, 0:D_EXPERT]
        u = gu[:, D_EXPERT:2 * D_EXPERT]
        a = (g * jax.nn.sigmoid(g)) * u
        y = jnp.dot(a.astype(BF16), wd_s[...], preferred_element_type=F32)
        y_ref[...] = y.reshape(EXPERT_BM, ROW_SUB, LANES)


def _expert_call(blk_e, nvb, blk_first, blk_slot, blk_next, xbuf, w_gate, w_up, w_down):
    rows = xbuf.shape[0]
    bm = EXPERT_BM
    used = lambda i, be, nv, bf, bs, bn: (jnp.minimum(i, nv[0] - 1), 0, 0)
    grid_spec = pltpu.PrefetchScalarGridSpec(
        num_scalar_prefetch=5,
        grid=(rows // bm,),
        in_specs=[
            pl.BlockSpec((bm, ROW_SUB, LANES), used),
            pl.BlockSpec(memory_space=pl.ANY),
            pl.BlockSpec(memory_space=pl.ANY),
            pl.BlockSpec(memory_space=pl.ANY),
        ],
        out_specs=pl.BlockSpec((bm, ROW_SUB, LANES), used),
        scratch_shapes=[pltpu.VMEM((2, D_MODEL, D_EXPERT), F32), pltpu.VMEM((2, D_MODEL, D_EXPERT), F32),
                        pltpu.VMEM((2, D_EXPERT, D_MODEL), F32),
                        pltpu.VMEM((D_MODEL, 2 * D_EXPERT), BF16), pltpu.VMEM((D_EXPERT, D_MODEL), BF16),
                        pltpu.SemaphoreType.DMA((2,))],
    )
    return pl.pallas_call(
        _expert_kernel,
        grid_spec=grid_spec,
        out_shape=jax.ShapeDtypeStruct((rows, ROW_SUB, LANES), F32),
        input_output_aliases={5: 0},
        compiler_params=pltpu.CompilerParams(dimension_semantics=("arbitrary",),
                                             vmem_limit_bytes=VMEM_LIMIT),
        name="experts",
    )(blk_e, nvb, blk_first, blk_slot, blk_next, xbuf, w_gate, w_up, w_down)


def _combine_kernel(dest_ref, dest_next_ref, h_ref, rinfo_ref, fn_ref, ybuf_ref, o_ref, ybuf_v, sem):
    T = COMBINE_T
    i = pl.program_id(0)
    cur = i % 2

    def row_copy(dref, buf, r, k):
        d = dref[0, 2 * r + k]
        return pltpu.make_async_copy(ybuf_ref.at[pl.ds(d, 1)], ybuf_v.at[buf, k, pl.ds(r, 1)], sem.at[buf])

    def issue(dref, buf):
        def start(r, c):
            row_copy(dref, buf, r, 0).start(priority=0)
            row_copy(dref, buf, r, 1).start(priority=1)
            return c
        lax.fori_loop(0, T, start, 0, unroll=DMA_UNROLL)

    @pl.when(i == 0)
    def _():
        issue(dest_ref, 0)

    @pl.when(i + 1 < pl.num_programs(0))
    def _():
        issue(dest_next_ref, 1 - cur)

    def wait(r, c):
        row_copy(dest_ref, cur, r, 0).wait()
        row_copy(dest_ref, cur, r, 1).wait()
        return c

    lax.fori_loop(0, T, wait, 0, unroll=DMA_UNROLL)
    rinfo = rinfo_ref[...]
    w1 = rinfo[:, 2:3]
    w2 = rinfo[:, 3:4]
    y1 = ybuf_v[cur, 0].reshape(T, D_MODEL)
    y2 = ybuf_v[cur, 1].reshape(T, D_MODEL)
    h = h_ref[...] + (y1 * w1 + y2 * w2)
    o_ref[...] = h * lax.rsqrt(jnp.mean(h * h, axis=-1, keepdims=True) + EPS) * fn_ref[...]


def _combine_call(dest, h, rinfo, fn, ybuf):
    n = h.shape[0]
    T = COMBINE_T
    nt = n // T
    return pl.pallas_call(
        _combine_kernel,
        grid=(nt,),
        in_specs=[
            pl.BlockSpec((None, 1, 2 * T), lambda i: (i, 0, 0), memory_space=pltpu.SMEM),
            pl.BlockSpec((None, 1, 2 * T), lambda i: (jnp.minimum(i + 1, nt - 1), 0, 0), memory_space=pltpu.SMEM),
            pl.BlockSpec((T, D_MODEL), lambda i: (i, 0)),
            pl.BlockSpec((T, LANES), lambda i: (i, 0)),
            pl.BlockSpec(fn.shape, lambda i: (0, 0)),
            pl.BlockSpec(memory_space=pl.ANY),
        ],
        out_specs=pl.BlockSpec((T, D_MODEL), lambda i: (i, 0)),
        out_shape=jax.ShapeDtypeStruct((n, D_MODEL), F32),
        scratch_shapes=[pltpu.VMEM((2, 2, T, ROW_SUB, LANES), F32), pltpu.SemaphoreType.DMA((2,))],
        compiler_params=pltpu.CompilerParams(dimension_semantics=("arbitrary",),
                                             vmem_limit_bytes=VMEM_LIMIT),
        name="combine",
    )(dest, dest, h, rinfo, fn, ybuf)


def kernel(x, meta_tokens, attn_norm, w_in, b_forget, conv_w, mix_norm, w_out, ffn_norm,
           w_router_group, b_router_group, w_router_expert, b_router_expert,
           w_gate, w_up, w_down, final_norm):
    bsz, seq, d = x.shape
    assert d == D_MODEL and seq % PROJ_T == 0 and meta_tokens.shape[0] == N_META
    assert attn_norm.shape[0] == 1, "single-layer block"
    n_tok = bsz * seq

    wi = w_in[0]
    o = 0
    w_q = wi[:, o:o + ATTN_W]; o += ATTN_W
    w_k = wi[:, o:o + ATTN_W]; o += ATTN_W
    w_v = wi[:, o:o + ATTN_W]; o += ATTN_W
    w_f = wi[:, o:o + N_HEADS]; o += N_HEADS
    w_xc = wi[:, o:o + CONV_W]; o += CONV_W
    w_bg = wi[:, o:o + CONV_W]; o += CONV_W
    w_cg = wi[:, o:o + CONV_W]
    wrow = jnp.concatenate([w_k, w_xc, w_bg, w_cg, w_f, jnp.zeros((D_MODEL, LANES - N_HEADS), F32)],
                           axis=1).astype(BF16)
    wt = jnp.concatenate([w_q.T * (HEAD_DIM ** -0.5), w_v.T, w_f.T,
                          jnp.zeros((PART_ROWS - N_HEADS, D_MODEL), F32)], axis=0).astype(BF16)
    g_attn = attn_norm[0].reshape(1, D_MODEL)
    bfr = jnp.zeros((1, LANES), F32).at[0, :N_HEADS].set(b_forget[0])
    bfc = jnp.zeros((PART_ROWS, LANES), F32).at[:N_HEADS, :].set(b_forget[0][:, None])
    convw = jnp.zeros((8, CONV_W), F32).at[:3].set(conv_w[0])
    selk = _selection_matrix()

    meta_pad = jnp.zeros((1, META_PAD, D_MODEL), F32).at[0, :N_META].set(meta_tokens)
    zrow = jnp.zeros((8, LANES), F32)
    zcol = jnp.zeros((PART_ROWS, LANES), F32)
    zu = jnp.zeros((8, CONV_W), F32)
    kmeta, _, vmeta, _, cumrow_m, cumt_m, u_m = _proj_call(
        meta_pad, g_attn, wrow, wt, bfr, bfc, convw, selk, zrow, zcol, zu,
        t_rows=META_PAD, n_valid=N_META, emit_carry=True)
    cin_row = jnp.zeros((8, LANES), F32).at[0].set(cumrow_m[0, N_META - 1])
    cin_col = jnp.broadcast_to(cumt_m[0, :, N_META - 1:N_META], (PART_ROWS, LANES))
    uin = u_m[0, N_META - 8:N_META]

    kaug, qaug, vt, conv = _proj_call(
        x, g_attn, wrow, wt, bfr, bfc, convw, selk, cin_row, cin_col, uin,
        t_rows=PROJ_T, n_valid=PROJ_T, emit_carry=False)

    mixg = mix_norm[0]
    attn = _attn_call(qaug, kaug, vt, kmeta[0], vmeta[0, 0], mixg[:ATTN_W].reshape(1, ATTN_W))

    gidx = np.arange(CONV_W) // HEAD_DIM
    gsum = jnp.asarray(gidx[:, None] == gidx[None, :], BF16)
    wo = w_out[0].astype(BF16)
    wr = jnp.zeros((D_MODEL, LANES), F32)
    wr = wr.at[:, :N_GROUPS].set(w_router_group[0]).at[:, ROUTER_LANE0:ROUTER_LANE0 + N_EXPERTS].set(
        w_router_expert[0])
    wr_hi = wr.astype(BF16)
    wrh = jnp.concatenate([wr_hi, (wr - wr_hi.astype(F32)).astype(BF16)], axis=1)
    br = jnp.zeros((1, LANES), F32)
    br = br.at[0, :N_GROUPS].set(b_router_group[0]).at[0, ROUTER_LANE0:ROUTER_LANE0 + N_EXPERTS].set(
        b_router_expert[0])
    h, z, rinfo, counts = _mix_call(
        attn.reshape(n_tok, ATTN_W), conv.reshape(n_tok, CONV_W), x.reshape(n_tok, D_MODEL),
        mixg[ATTN_W:].reshape(1, CONV_W), gsum, wo[:ATTN_W], wo[ATTN_W:], ffn_norm[0].reshape(1, D_MODEL),
        wrh, br)

    cnt = counts[0, ROUTER_LANE0:ROUTER_LANE0 + N_EXPERTS].astype(jnp.int32)
    padded = (cnt + EXPERT_BM - 1) // EXPERT_BM * EXPERT_BM
    pad_end = jnp.cumsum(padded)
    pad_start = pad_end - padded
    eid = rinfo[:, 0:2].astype(jnp.int32)
    rank = rinfo[:, 4:6].astype(jnp.int32)
    expert_ids = jnp.arange(N_EXPERTS, dtype=jnp.int32)
    dest = rank + jnp.sum(jnp.where(eid[:, :, None] == expert_ids, pad_start, 0), axis=-1)
    dest = dest.reshape(n_tok // DISPATCH_T, 1, 2 * DISPATCH_T)
    n_blk = (2 * n_tok) // EXPERT_BM + N_EXPERTS
    nvb = (pad_end[-1] // EXPERT_BM).reshape(1)
    blk_row = jnp.minimum(jnp.arange(n_blk, dtype=jnp.int32), nvb - 1) * EXPERT_BM
    blk_e = jnp.sum(blk_row[:, None] >= pad_end[None, :], axis=1).astype(jnp.int32)

    xbuf = _dispatch_call(pad_end, cnt, dest, z, n_blk * EXPERT_BM)
    nonempty = cnt > 0
    ordinal = jnp.cumsum(nonempty.astype(jnp.int32)) - 1
    later = (expert_ids[None, :] > expert_ids[:, None]) & nonempty[None, :]
    next_ne = jnp.min(jnp.where(later, expert_ids[None, :], N_EXPERTS), axis=1)
    next_ne = jnp.where(next_ne == N_EXPERTS, -1, next_ne).astype(jnp.int32)
    blk_idx = jnp.arange(n_blk, dtype=jnp.int32)
    blk_first = ((blk_idx * EXPERT_BM == pad_start[blk_e]) & (blk_idx < nvb)).astype(jnp.int32)
    blk_slot = (ordinal[blk_e] % 2).astype(jnp.int32)
    blk_next = next_ne[blk_e]
    ybuf = _expert_call(blk_e, nvb, blk_first, blk_slot, blk_next, xbuf, w_gate[0], w_up[0], w_down[0])
    out = _combine_call(dest, h, rinfo, final_norm.reshape(1, D_MODEL), ybuf)
    return out.reshape(bsz, seq, D_MODEL)
```

```python
import functools

import numpy as np
import jax
import jax.numpy as jnp
from jax import lax
from jax.experimental import pallas as pl
from jax.experimental.pallas import tpu as pltpu

F32 = jnp.float32
BF16 = jnp.bfloat16

D_MODEL = 1024
HEAD_DIM = 64
N_HEADS = 8
ATTN_W = N_HEADS * HEAD_DIM
CONV_W = 512
N_META = 16
N_GROUPS = 4
EXPERTS_PER_GROUP = 8
N_EXPERTS = N_GROUPS * EXPERTS_PER_GROUP
D_EXPERT = 512
EPS = 1e-6
NEG_BIG = -1e30
LOG2E = 1.4426950408889634

LANES = 128
ROW_SUB = D_MODEL // LANES
HEAD_SLOT = 2 * HEAD_DIM
V_SLOT = HEAD_DIM + 16
V_AUG = N_HEADS * V_SLOT
AUG_W = N_HEADS * HEAD_SLOT
N_PARTS = 3
PART_ROWS = 16
ONES_LANE = 8
ROUTER_LANE0 = N_GROUPS

PROJ_T = 512
ATT_T = 256
META_PAD = ATT_T
MIX_T = 512
DISPATCH_T = 512
EXPERT_BM = 512
COMBINE_T = 512
DMA_UNROLL = 8
VMEM_LIMIT = 48 * 1024 * 1024


def _split3(x):
    hi = x.astype(BF16)
    r1 = x - hi.astype(F32)
    mid = r1.astype(BF16)
    r2 = r1 - mid.astype(F32)
    return hi, mid, r2.astype(BF16)


def _log_sigmoid(x):
    return jnp.minimum(x, 0.0) - jnp.log1p(jnp.exp(-jnp.abs(x)))


def _head_offset(h):
    return HEAD_DIM if h % 2 == 0 else 0


def _bias_slot(j, h):
    return N_PARTS + N_HEADS * j + h


def _selection_matrix():
    selk = np.zeros((N_PARTS * LANES, 2 * LANES), np.float32)
    for base in (_head_offset(0), LANES + _head_offset(1)):
        for j in range(N_PARTS):
            selk[ONES_LANE, base + j] = 1.0
            for h in range(N_HEADS):
                selk[j * LANES + h, base + _bias_slot(j, h)] = 1.0
    return jnp.asarray(selk, BF16)


def _proj_kernel(x_ref, g_ref, wrow_ref, wt_ref, bfr_ref, bfc_ref, convw_ref, selk_ref,
                 cin_row_ref, cin_col_ref, uin_ref,
                 kaug_ref, qaug_ref, vt_ref, conv_ref, *rest, t_rows, n_valid, emit_carry, tk):
    if emit_carry:
        cumrow_out, cumt_out, u_out, c_row, c_col, u_prev = rest
    else:
        c_row, c_col, u_prev = rest

    @pl.when(pl.program_id(1) == 0)
    def _():
        c_row[...] = cin_row_ref[...]
        c_col[...] = cin_col_ref[...]
        u_prev[...] = uin_ref[...]

    T = t_rows
    x = x_ref[...]
    ms = jnp.mean(x * x, axis=-1, keepdims=True)
    z = x * lax.rsqrt(ms + EPS) * g_ref[...]
    zb = z.astype(BF16)
    r = jnp.dot(zb, wrow_ref[...], preferred_element_type=F32)
    tt = lax.dot_general(wt_ref[...], zb, (((1,), (1,)), ((), ())),
                         preferred_element_type=F32)
    k = r[:, 0:ATTN_W]
    xc = r[:, ATTN_W:ATTN_W + CONV_W]
    bg = r[:, ATTN_W + CONV_W:ATTN_W + 2 * CONV_W]
    cg = r[:, ATTN_W + 2 * CONV_W:ATTN_W + 3 * CONV_W]
    fr = r[:, ATTN_W + 3 * CONV_W:ATTN_W + 3 * CONV_W + LANES]
    qt = tt[0:ATTN_W]
    vt = tt[ATTN_W:2 * ATTN_W]
    ft = tt[2 * ATTN_W:2 * ATTN_W + PART_ROWS]

    lane_r = lax.broadcasted_iota(jnp.int32, (T, LANES), 1)
    row_r = lax.broadcasted_iota(jnp.int32, (T, LANES), 0)
    lfr = jnp.where(lane_r < N_HEADS, _log_sigmoid(fr + bfr_ref[...]), 0.0)
    lfr = lfr + jnp.where(row_r == 0, c_row[0:1, :], 0.0)
    tri_r = lax.broadcasted_iota(jnp.int32, (T, T), 0)
    tri_c = lax.broadcasted_iota(jnp.int32, (T, T), 1)
    tri_l = (tri_c <= tri_r).astype(BF16)
    tri_u = (tri_r <= tri_c).astype(BF16)
    c3 = jnp.dot(tri_l, jnp.concatenate(_split3(lfr), axis=1), preferred_element_type=F32)
    cum_row = c3[:, 0:LANES] + c3[:, LANES:2 * LANES] + c3[:, 2 * LANES:3 * LANES]

    kb = cum_row * (-LOG2E)
    if n_valid < T:
        kb = kb + jnp.where(row_r >= n_valid, NEG_BIG, 0.0)
    kb = jnp.where(lane_r == ONES_LANE, 1.0, kb)
    e = jnp.dot(jnp.concatenate(_split3(kb), axis=1), selk_ref[...],
                preferred_element_type=F32)
    e_even = e[:, 0:LANES].astype(BF16)
    e_odd = e[:, LANES:2 * LANES].astype(BF16)
    for p in range(N_HEADS // 2):
        kp = k[:, p * LANES:(p + 1) * LANES].astype(BF16)
        kaug_ref[:, 2 * p * LANES:(2 * p + 1) * LANES] = jnp.where(lane_r < HEAD_DIM, kp, e_even)
        kaug_ref[:, (2 * p + 1) * LANES:(2 * p + 2) * LANES] = jnp.where(lane_r >= HEAD_DIM, kp, e_odd)

    lane_c = lax.broadcasted_iota(jnp.int32, (PART_ROWS, T), 1)
    row_c = lax.broadcasted_iota(jnp.int32, (PART_ROWS, T), 0)
    lfc = jnp.where(row_c < N_HEADS, _log_sigmoid(ft + bfc_ref[:, 0:1]), 0.0)
    lfc = lfc + jnp.where(lane_c == 0, c_col[:, 0:1], 0.0)
    c3c = jnp.dot(jnp.concatenate(_split3(lfc), axis=0), tri_u, preferred_element_type=F32)
    cum_t = c3c[0:PART_ROWS] + c3c[PART_ROWS:2 * PART_ROWS] + c3c[2 * PART_ROWS:3 * PART_ROWS]
    cq_parts = [p.astype(F32) for p in _split3(cum_t * LOG2E)]
    bias_rows = HEAD_DIM // 2
    row_b = lax.broadcasted_iota(jnp.int32, (bias_rows, T), 0)
    zero_rows = jnp.zeros((HEAD_DIM - bias_rows, T), BF16)
    for h in range(N_HEADS):
        qh = (qt[h * HEAD_DIM:(h + 1) * HEAD_DIM] * LOG2E).astype(BF16)
        own = (row_b == _bias_slot(0, h)) | (row_b == _bias_slot(1, h)) | (row_b == _bias_slot(2, h))
        bias = jnp.where(own, 1.0, 0.0)
        for j in range(N_PARTS):
            bias = jnp.where(row_b == j, cq_parts[j][h:h + 1, :], bias)
        lo = h * HEAD_SLOT + _head_offset(h)
        qaug_ref[lo:lo + bias_rows, :] = bias.astype(BF16)
        qaug_ref[lo + bias_rows:lo + HEAD_DIM, :] = zero_rows
        lo_q = h * HEAD_SLOT + (HEAD_DIM - _head_offset(h))
        qaug_ref[lo_q:lo_q + HEAD_DIM, :] = qh

    vtb = vt.astype(BF16)
    ones_rows = jnp.ones((V_SLOT - HEAD_DIM, tk), BF16)
    for s in range(T // tk):
        for h in range(N_HEADS):
            vt_ref[s, h * V_SLOT:h * V_SLOT + HEAD_DIM, :] = vtb[h * HEAD_DIM:(h + 1) * HEAD_DIM, s * tk:(s + 1) * tk]
            vt_ref[s, h * V_SLOT + HEAD_DIM:(h + 1) * V_SLOT, :] = ones_rows

    u = cg * xc
    row_u = lax.broadcasted_iota(jnp.int32, (T, CONV_W), 0)
    p1 = u_prev[7:8, :]
    p2 = u_prev[6:7, :]
    u1 = jnp.where(row_u == 0, p1, pltpu.roll(u, 1, axis=0))
    u2 = jnp.where(row_u == 0, p2, jnp.where(row_u == 1, p1, pltpu.roll(u, 2, axis=0)))
    zc = convw_ref[0:1, :] * u2 + convw_ref[1:2, :] * u1 + convw_ref[2:3, :] * u
    conv_ref[...] = bg * zc

    if emit_carry:
        cumrow_out[...] = cum_row
        cumt_out[...] = cum_t
        u_out[...] = u
    c_row[0:1, :] = cum_row[T - 1:T, :]
    c_col[...] = jnp.broadcast_to(cum_t[:, T - 1:T], (PART_ROWS, LANES))
    u_prev[...] = u[T - 8:T, :]


def _proj_call(xs, g, wrow, wt, bfr, bfc, convw, selk, cin_row, cin_col, uin,
               *, t_rows, n_valid, emit_carry):
    bn, ln, _ = xs.shape
    T = t_rows
    tk = min(ATT_T, T)
    nt = ln // T
    const2 = lambda b, t: (0, 0)
    in_specs = [
        pl.BlockSpec((None, T, D_MODEL), lambda b, t: (b, t, 0)),
        pl.BlockSpec(g.shape, const2),
        pl.BlockSpec(wrow.shape, const2),
        pl.BlockSpec(wt.shape, const2),
        pl.BlockSpec(bfr.shape, const2),
        pl.BlockSpec(bfc.shape, const2),
        pl.BlockSpec(convw.shape, const2),
        pl.BlockSpec(selk.shape, const2),
        pl.BlockSpec(cin_row.shape, const2),
        pl.BlockSpec(cin_col.shape, const2),
        pl.BlockSpec(uin.shape, const2),
    ]
    out_shape = [
        jax.ShapeDtypeStruct((bn, ln, AUG_W), BF16),
        jax.ShapeDtypeStruct((bn, AUG_W, ln), BF16),
        jax.ShapeDtypeStruct((bn, ln // tk, V_AUG, tk), BF16),
        jax.ShapeDtypeStruct((bn, ln, CONV_W), F32),
    ]
    out_specs = [
        pl.BlockSpec((None, T, AUG_W), lambda b, t: (b, t, 0)),
        pl.BlockSpec((None, AUG_W, T), lambda b, t: (b, 0, t)),
        pl.BlockSpec((None, T // tk, V_AUG, tk), lambda b, t: (b, t, 0, 0)),
        pl.BlockSpec((None, T, CONV_W), lambda b, t: (b, t, 0)),
    ]
    if emit_carry:
        out_shape += [
            jax.ShapeDtypeStruct((bn, ln, LANES), F32),
            jax.ShapeDtypeStruct((bn, PART_ROWS, ln), F32),
            jax.ShapeDtypeStruct((bn, ln, CONV_W), F32),
        ]
        out_specs += [
            pl.BlockSpec((None, T, LANES), lambda b, t: (b, t, 0)),
            pl.BlockSpec((None, PART_ROWS, T), lambda b, t: (b, 0, t)),
            pl.BlockSpec((None, T, CONV_W), lambda b, t: (b, t, 0)),
        ]
    kern = functools.partial(_proj_kernel, t_rows=T, n_valid=n_valid, emit_carry=emit_carry, tk=tk)
    return pl.pallas_call(
        kern,
        grid=(bn, nt),
        in_specs=in_specs,
        out_specs=out_specs,
        out_shape=out_shape,
        scratch_shapes=[pltpu.VMEM((8, LANES), F32), pltpu.VMEM((PART_ROWS, LANES), F32),
                        pltpu.VMEM((8, CONV_W), F32)],
        compiler_params=pltpu.CompilerParams(dimension_semantics=("arbitrary", "arbitrary"),
                                             vmem_limit_bytes=VMEM_LIMIT),
        name="proj_meta" if emit_carry else "proj",
    )(xs, g, wrow, wt, bfr, bfc, convw, selk, cin_row, cin_col, uin)


def _attn_kernel(q_ref, k_ref, v_ref, km_ref, vm_ref, g_ref, o_ref, s_s, m_s, acc_s):
    i = pl.program_id(1)
    T = ATT_T
    causal = (lax.broadcasted_iota(jnp.int32, (T, T), 0) <= lax.broadcasted_iota(jnp.int32, (T, T), 1))

    def scores(h, kblk, slot, masked):
        qa = q_ref[h * HEAD_SLOT:(h + 1) * HEAD_SLOT, :]
        s = jnp.dot(kblk, qa, preferred_element_type=F32)
        if masked:
            s = jnp.where(causal, s, NEG_BIG)
        s_s[slot, h] = s

    def consume(h, vblk, slot):
        s = s_s[slot, h]
        m_old = m_s[h, 0:1, :]
        m_new = jnp.maximum(m_old, jnp.max(s, axis=0, keepdims=True))
        alpha = jnp.exp2(m_old - m_new)
        pm = jnp.exp2(s - m_new).astype(BF16)
        acc_s[h] = alpha * acc_s[h] + jnp.dot(vblk, pm, preferred_element_type=F32)
        m_s[h, 0:1, :] = m_new

    def step(k_next, next_slot, masked, v_cur, cur_slot):
        ahead = 2
        for h in range(ahead):
            scores(h, k_next(h), next_slot, masked)
        for h in range(N_HEADS):
            consume(h, v_cur(h), cur_slot)
            if h + ahead < N_HEADS:
                scores(h + ahead, k_next(h + ahead), next_slot, masked)

    def k_of(j):
        row = pl.ds(pl.multiple_of(j * T, T), T)
        return lambda h: k_ref[row, h * HEAD_SLOT:(h + 1) * HEAD_SLOT]

    def v_of(j):
        return lambda h: v_ref[j, h * V_SLOT:(h + 1) * V_SLOT, :]

    k_meta = lambda h: km_ref[:, h * HEAD_SLOT:(h + 1) * HEAD_SLOT]
    v_meta = lambda h: vm_ref[h * V_SLOT:(h + 1) * V_SLOT, :]

    m_s[...] = jnp.full(m_s.shape, NEG_BIG, F32)
    acc_s[...] = jnp.zeros(acc_s.shape, F32)
    for h in range(N_HEADS):
        scores(h, k_meta(h), 0, False)

    def v_at(t):
        return lambda h: jnp.where(t == 0, v_meta(h), v_ref[jnp.maximum(t - 1, 0), h * V_SLOT:(h + 1) * V_SLOT, :])

    def pair_body(p, c):
        t0 = 2 * p
        step(k_of(t0), 1, False, v_at(t0), 0)
        step(k_of(t0 + 1), 0, False, v_at(t0 + 1), 1)
        return c

    lax.fori_loop(0, i // 2, pair_body, 0)

    @pl.when(i % 2 == 1)
    def _():
        step(k_of(i - 1), 1, False, v_at(i - 1), 0)
        step(k_of(i), 0, True, v_at(i), 1)
        for h in range(N_HEADS):
            consume(h, v_of(i)(h), 0)

    @pl.when(i % 2 == 0)
    def _():
        step(k_of(i), 1, True, v_at(i), 0)
        for h in range(N_HEADS):
            consume(h, v_of(i)(h), 1)

    for p in range(N_HEADS // 2):
        halves = []
        for h in (2 * p, 2 * p + 1):
            o = acc_s[h, 0:HEAD_DIM, :] * (1.0 / acc_s[h, HEAD_DIM:HEAD_DIM + 1, :])
            halves.append(o * lax.rsqrt(jnp.mean(o * o, axis=0, keepdims=True) + EPS))
        pair = jnp.concatenate(halves, axis=0)
        o_ref[:, p * LANES:(p + 1) * LANES] = (pair.T * g_ref[:, p * LANES:(p + 1) * LANES]).astype(o_ref.dtype)


def _attn_call(qaug, kaug, vt, kmeta, vmeta, gain):
    bn, _, ln = qaug.shape
    T = ATT_T
    nq = ln // T
    return pl.pallas_call(
        _attn_kernel,
        grid=(bn, nq),
        in_specs=[
            pl.BlockSpec((None, AUG_W, T), lambda b, i: (b, 0, i)),
            pl.BlockSpec((None, ln, AUG_W), lambda b, i: (b, 0, 0)),
            pl.BlockSpec((None, ln // T, V_AUG, T), lambda b, i: (b, 0, 0, 0)),
            pl.BlockSpec(kmeta.shape, lambda b, i: (0, 0)),
            pl.BlockSpec(vmeta.shape, lambda b, i: (0, 0)),
            pl.BlockSpec(gain.shape, lambda b, i: (0, 0)),
        ],
        out_specs=pl.BlockSpec((None, T, ATTN_W), lambda b, i: (b, i, 0)),
        out_shape=jax.ShapeDtypeStruct((bn, ln, ATTN_W), BF16),
        scratch_shapes=[pltpu.VMEM((2, N_HEADS, T, T), F32),
                        pltpu.VMEM((N_HEADS, 8, T), F32),
                        pltpu.VMEM((N_HEADS, V_SLOT, T), F32)],
        compiler_params=pltpu.CompilerParams(dimension_semantics=("arbitrary", "arbitrary"),
                                             vmem_limit_bytes=VMEM_LIMIT),
        name="attention",
    )(qaug, kaug, vt, kmeta, vmeta, gain)


def _mix_kernel(attn_ref, conv_ref, x_ref, gc_ref, gsum_ref, woa_ref, woc_ref, fg_ref,
                wrh_ref, br_ref, h_ref, z_ref, rinfo_ref, counts_ref, rinfo_t_ref, cnt):
    T = MIX_T

    @pl.when(pl.program_id(0) == 0)
    def _():
        cnt[...] = jnp.zeros_like(cnt)

    c = conv_ref[...]
    c2 = c * c
    c2h = c2.astype(BF16)
    c2l = (c2 - c2h.astype(F32)).astype(BF16)
    ss = (jnp.dot(c2h, gsum_ref[...], preferred_element_type=F32)
          + jnp.dot(c2l, gsum_ref[...], preferred_element_type=F32))
    cn = (c * lax.rsqrt(ss * (1.0 / HEAD_DIM) + EPS) * gc_ref[...]).astype(BF16)
    hadd = (jnp.dot(attn_ref[...], woa_ref[...], preferred_element_type=F32)
            + jnp.dot(cn, woc_ref[...], preferred_element_type=F32))
    h = x_ref[...] + hadd
    h_ref[...] = h
    z = h * lax.rsqrt(jnp.mean(h * h, axis=-1, keepdims=True) + EPS) * fg_ref[...]
    z_ref[...] = z.reshape(T, ROW_SUB, LANES)
    zh = z.astype(BF16)
    zl = (z - zh.astype(F32)).astype(BF16)
    both = jnp.dot(zh, wrh_ref[...], preferred_element_type=F32)
    logits = (both[:, 0:LANES] + both[:, LANES:2 * LANES]
              + jnp.dot(zl, wrh_ref[:, 0:LANES], preferred_element_type=F32)) + br_ref[...]

    lane = lax.broadcasted_iota(jnp.int32, (T, LANES), 1).astype(F32)
    big = float(LANES)
    gl = jnp.where(lane < N_GROUPS, logits, -jnp.inf)
    gmax = jnp.max(gl, axis=-1, keepdims=True)
    gsum = jnp.sum(jnp.exp(gl - gmax), axis=-1, keepdims=True)
    g_w = 1.0 / gsum
    g_idx = jnp.min(jnp.where(gl == gmax, lane, big), axis=-1, keepdims=True)
    e_lo = ROUTER_LANE0 + EXPERTS_PER_GROUP * g_idx
    emask = (lane >= e_lo) & (lane < e_lo + EXPERTS_PER_GROUP)
    el = jnp.where(emask, logits, -jnp.inf)
    emax = jnp.max(el, axis=-1, keepdims=True)
    eexp = jnp.exp(el - emax)
    probs = eexp / jnp.sum(eexp, axis=-1, keepdims=True)
    probs = jnp.where(emask, probs, -1.0)
    p1 = jnp.max(probs, axis=-1, keepdims=True)
    i1 = jnp.min(jnp.where(probs == p1, lane, big), axis=-1, keepdims=True)
    probs2 = jnp.where(lane == i1, -1.0, probs)
    p2 = jnp.max(probs2, axis=-1, keepdims=True)
    i2 = jnp.min(jnp.where(probs2 == p2, lane, big), axis=-1, keepdims=True)
    psum = p1 + p2
    w1 = g_w * (p1 / psum)
    w2 = g_w * (p2 / psum)

    hit = ((lane == i1) | (lane == i2)).astype(F32)
    tri_r = lax.broadcasted_iota(jnp.int32, (T, T), 0)
    tri_c = lax.broadcasted_iota(jnp.int32, (T, T), 1)
    strict_l = (tri_c < tri_r).astype(BF16)
    before = jnp.dot(strict_l, hit.astype(BF16), preferred_element_type=F32) + cnt[0:1, :]
    r1 = jnp.sum(jnp.where(lane == i1, before, 0.0), axis=-1, keepdims=True)
    r2 = jnp.sum(jnp.where(lane == i2, before, 0.0), axis=-1, keepdims=True)
    cnt[0:1, :] = cnt[0:1, :] + jnp.sum(hit, axis=0, keepdims=True)
    counts_ref[...] = cnt[...]

    e1 = i1 - ROUTER_LANE0
    e2 = i2 - ROUTER_LANE0
    rinfo = jnp.where(lane == 0, e1, jnp.where(lane == 1, e2, jnp.where(lane == 2, w1, jnp.where(
        lane == 3, w2, jnp.where(lane == 4, r1, jnp.where(lane == 5, r2, 0.0))))))
    rinfo_ref[...] = rinfo
    rinfo_t_ref[...] = rinfo.T[0:8, :]


def _mix_call(attn, conv, x, gc, gsum, woa, woc, fg, wrh, br):
    n = x.shape[0]
    T = MIX_T
    const = lambda i: (0, 0)
    row = lambda w: pl.BlockSpec((T, w), lambda i: (i, 0))
    return pl.pallas_call(
        _mix_kernel,
        grid=(n // T,),
        in_specs=[row(ATTN_W), row(CONV_W), row(D_MODEL)] + [
            pl.BlockSpec(a.shape, const) for a in (gc, gsum, woa, woc, fg, wrh, br)],
        out_specs=[row(D_MODEL), pl.BlockSpec((T, ROW_SUB, LANES), lambda i: (i, 0, 0)), row(LANES),
                   pl.BlockSpec((8, LANES), const), pl.BlockSpec((8, T), lambda i: (0, i))],
        out_shape=[jax.ShapeDtypeStruct((n, D_MODEL), F32), jax.ShapeDtypeStruct((n, ROW_SUB, LANES), F32),
                   jax.ShapeDtypeStruct((n, LANES), F32), jax.ShapeDtypeStruct((8, LANES), F32),
                   jax.ShapeDtypeStruct((8, n), F32)],
        scratch_shapes=[pltpu.VMEM((8, LANES), F32)],
        compiler_params=pltpu.CompilerParams(dimension_semantics=("arbitrary",),
                                             vmem_limit_bytes=VMEM_LIMIT),
        name="mix",
    )(attn, conv, x, gc, gsum, woa, woc, fg, wrh, br)


def _dispatch_kernel(pad_end_ref, cnt_ref, dest_ref, z_ref, xbuf_ref, zero_v, sem, zsem):
    T = DISPATCH_T

    @pl.when(pl.program_id(0) == 0)
    def _():
        zero_v[...] = jnp.zeros_like(zero_v)

        def zero_copy(e):
            first = pl.multiple_of(jnp.maximum(pad_end_ref[e] - EXPERT_BM, 0), EXPERT_BM)
            return pltpu.make_async_copy(zero_v, xbuf_ref.at[pl.ds(first, EXPERT_BM)], zsem)

        for e in range(N_EXPERTS):
            @pl.when(cnt_ref[e] > 0)
            def _():
                zero_copy(e).start()
        for e in range(N_EXPERTS):
            @pl.when(cnt_ref[e] > 0)
            def _():
                zero_copy(e).wait()

        def tail_copy(b):
            return pltpu.make_async_copy(
                zero_v, xbuf_ref.at[pl.ds(pl.multiple_of(b * EXPERT_BM, EXPERT_BM), EXPERT_BM)], zsem)

        n_used = pad_end_ref[N_EXPERTS - 1] // EXPERT_BM
        n_all = xbuf_ref.shape[0] // EXPERT_BM
        lax.fori_loop(n_used, n_all, lambda b, c: (tail_copy(b).start(), c)[1], 0)
        lax.fori_loop(n_used, n_all, lambda b, c: (tail_copy(b).wait(), c)[1], 0)

    def row_copy(r, slot):
        d = dest_ref[0, slot * T + r]
        return pltpu.make_async_copy(z_ref.at[pl.ds(r, 1)], xbuf_ref.at[pl.ds(d, 1)], sem)

    def start(r, c):
        row_copy(r, 0).start(priority=0)
        row_copy(r, 1).start(priority=1)
        return c

    def wait(r, c):
        row_copy(r, 0).wait()
        row_copy(r, 1).wait()
        return c

    lax.fori_loop(0, T, start, 0, unroll=DMA_UNROLL)
    lax.fori_loop(0, T, wait, 0, unroll=DMA_UNROLL)


def _dispatch_call(pad_end, cnt, dest, z, n_rows):
    n = z.shape[0]
    T = DISPATCH_T
    grid_spec = pltpu.PrefetchScalarGridSpec(
        num_scalar_prefetch=2,
        grid=(n // T,),
        in_specs=[
            pl.BlockSpec((None, 1, 2 * T), lambda i, pe, ct: (i, 0, 0), memory_space=pltpu.SMEM),
            pl.BlockSpec((T, ROW_SUB, LANES), lambda i, pe, ct: (i, 0, 0)),
        ],
        out_specs=pl.BlockSpec(memory_space=pl.ANY),
        scratch_shapes=[pltpu.VMEM((EXPERT_BM, ROW_SUB, LANES), F32), pltpu.SemaphoreType.DMA(()),
                        pltpu.SemaphoreType.DMA(())],
    )
    return pl.pallas_call(
        _dispatch_kernel,
        grid_spec=grid_spec,
        out_shape=jax.ShapeDtypeStruct((n_rows, ROW_SUB, LANES), F32),
        compiler_params=pltpu.CompilerParams(dimension_semantics=("arbitrary",), vmem_limit_bytes=VMEM_LIMIT),
        name="dispatch",
    )(pad_end, cnt, dest, z)


def _expert_kernel(blk_e_ref, nvb_ref, first_ref, slot_ref, next_ref, x_ref, wg_hbm, wu_hbm, wd_hbm, y_ref,
                   wg_v, wu_v, wd_v, wgu_s, wd_s, sem):
    blk = pl.program_id(0)
    valid = blk < nvb_ref[0]

    def fetch(e, s):
        return (pltpu.make_async_copy(wg_hbm.at[e], wg_v.at[s], sem.at[s]),
                pltpu.make_async_copy(wu_hbm.at[e], wu_v.at[s], sem.at[s]),
                pltpu.make_async_copy(wd_hbm.at[e], wd_v.at[s], sem.at[s]))

    @pl.when(blk == 0)
    def _():
        for c in fetch(blk_e_ref[0], 0):
            c.start()

    @pl.when(valid & (first_ref[blk] != 0))
    def _():
        s = slot_ref[blk]
        for c in fetch(blk_e_ref[blk], s):
            c.wait()
        wgu_s[:, 0:D_EXPERT] = wg_v[s].astype(BF16)
        wgu_s[:, D_EXPERT:2 * D_EXPERT] = wu_v[s].astype(BF16)
        wd_s[...] = wd_v[s].astype(BF16)

        @pl.when(next_ref[blk] >= 0)
        def _():
            for c in fetch(next_ref[blk], 1 - s):
                c.start()

    @pl.when(valid)
    def _():
        xb = x_ref[...].reshape(EXPERT_BM, D_MODEL).astype(BF16)
        gu = jnp.dot(xb, wgu_s[...], preferred_element_type=F32)
        g = gu[:, 0:D_EXPERT]
        u = gu[:, D_EXPERT:2 * D_EXPERT]
        a = (g * jax.nn.sigmoid(g)) * u
        y = jnp.dot(a.astype(BF16), wd_s[...], preferred_element_type=F32)
        y_ref[...] = y.reshape(EXPERT_BM, ROW_SUB, LANES)


def _expert_call(blk_e, nvb, blk_first, blk_slot, blk_next, xbuf, w_gate, w_up, w_down):
    rows = xbuf.shape[0]
    bm = EXPERT_BM
    used = lambda i, be, nv, bf, bs, bn: (jnp.minimum(i, nv[0] - 1), 0, 0)
    grid_spec = pltpu.PrefetchScalarGridSpec(
        num_scalar_prefetch=5,
        grid=(rows // bm,),
        in_specs=[
            pl.BlockSpec((bm, ROW_SUB, LANES), used),
            pl.BlockSpec(memory_space=pl.ANY),
            pl.BlockSpec(memory_space=pl.ANY),
            pl.BlockSpec(memory_space=pl.ANY),
        ],
        out_specs=pl.BlockSpec((bm, ROW_SUB, LANES), used),
        scratch_shapes=[pltpu.VMEM((2, D_MODEL, D_EXPERT), F32), pltpu.VMEM((2, D_MODEL, D_EXPERT), F32),
                        pltpu.VMEM((2, D_EXPERT, D_MODEL), F32),
                        pltpu.VMEM((D_MODEL, 2 * D_EXPERT), BF16), pltpu.VMEM((D_EXPERT, D_MODEL), BF16),
                        pltpu.SemaphoreType.DMA((2,))],
    )
    return pl.pallas_call(
        _expert_kernel,
        grid_spec=grid_spec,
        out_shape=jax.ShapeDtypeStruct((rows, ROW_SUB, LANES), F32),
        input_output_aliases={5: 0},
        compiler_params=pltpu.CompilerParams(dimension_semantics=("arbitrary",),
                                             vmem_limit_bytes=VMEM_LIMIT),
        name="experts",
    )(blk_e, nvb, blk_first, blk_slot, blk_next, xbuf, w_gate, w_up, w_down)


def _combine_kernel(dest_ref, dest_next_ref, h_ref, rinfo_ref, fn_ref, ybuf_ref, o_ref, ybuf_v, sem):
    T = COMBINE_T
    i = pl.program_id(0)
    cur = i % 2

    def row_copy(dref, buf, r, k):
        d = dref[0, k * T + r]
        return pltpu.make_async_copy(ybuf_ref.at[pl.ds(d, 1)], ybuf_v.at[buf, k, pl.ds(r, 1)], sem.at[buf])

    def issue(dref, buf):
        def start(r, c):
            row_copy(dref, buf, r, 0).start(priority=0)
            row_copy(dref, buf, r, 1).start(priority=1)
            return c
        lax.fori_loop(0, T, start, 0, unroll=DMA_UNROLL)

    @pl.when(i == 0)
    def _():
        issue(dest_ref, 0)

    @pl.when(i + 1 < pl.num_programs(0))
    def _():
        issue(dest_next_ref, 1 - cur)

    def wait(r, c):
        row_copy(dest_ref, cur, r, 0).wait()
        row_copy(dest_ref, cur, r, 1).wait()
        return c

    lax.fori_loop(0, T, wait, 0, unroll=DMA_UNROLL)
    rinfo = rinfo_ref[...]
    w1 = rinfo[:, 2:3]
    w2 = rinfo[:, 3:4]
    y1 = ybuf_v[cur, 0].reshape(T, D_MODEL)
    y2 = ybuf_v[cur, 1].reshape(T, D_MODEL)
    h = h_ref[...] + (y1 * w1 + y2 * w2)
    o_ref[...] = h * lax.rsqrt(jnp.mean(h * h, axis=-1, keepdims=True) + EPS) * fn_ref[...]


def _combine_call(dest, h, rinfo, fn, ybuf):
    n = h.shape[0]
    T = COMBINE_T
    nt = n // T
    return pl.pallas_call(
        _combine_kernel,
        grid=(nt,),
        in_specs=[
            pl.BlockSpec((None, 1, 2 * T), lambda i: (i, 0, 0), memory_space=pltpu.SMEM),
            pl.BlockSpec((None, 1, 2 * T), lambda i: (jnp.minimum(i + 1, nt - 1), 0, 0), memory_space=pltpu.SMEM),
            pl.BlockSpec((T, D_MODEL), lambda i: (i, 0)),
            pl.BlockSpec((T, LANES), lambda i: (i, 0)),
            pl.BlockSpec(fn.shape, lambda i: (0, 0)),
            pl.BlockSpec(memory_space=pl.ANY),
        ],
        out_specs=pl.BlockSpec((T, D_MODEL), lambda i: (i, 0)),
        out_shape=jax.ShapeDtypeStruct((n, D_MODEL), F32),
        scratch_shapes=[pltpu.VMEM((2, 2, T, ROW_SUB, LANES), F32), pltpu.SemaphoreType.DMA((2,))],
        compiler_params=pltpu.CompilerParams(dimension_semantics=("arbitrary",),
                                             vmem_limit_bytes=VMEM_LIMIT),
        name="combine",
    )(dest, dest, h, rinfo, fn, ybuf)


def kernel(x, meta_tokens, attn_norm, w_in, b_forget, conv_w, mix_norm, w_out, ffn_norm,
           w_router_group, b_router_group, w_router_expert, b_router_expert,
           w_gate, w_up, w_down, final_norm):
    bsz, seq, d = x.shape
    assert d == D_MODEL and seq % PROJ_T == 0 and meta_tokens.shape[0] == N_META
    assert attn_norm.shape[0] == 1, "single-layer block"
    n_tok = bsz * seq

    wi = w_in[0]
    o = 0
    w_q = wi[:, o:o + ATTN_W]; o += ATTN_W
    w_k = wi[:, o:o + ATTN_W]; o += ATTN_W
    w_v = wi[:, o:o + ATTN_W]; o += ATTN_W
    w_f = wi[:, o:o + N_HEADS]; o += N_HEADS
    w_xc = wi[:, o:o + CONV_W]; o += CONV_W
    w_bg = wi[:, o:o + CONV_W]; o += CONV_W
    w_cg = wi[:, o:o + CONV_W]
    wrow = jnp.concatenate([w_k, w_xc, w_bg, w_cg, w_f, jnp.zeros((D_MODEL, LANES - N_HEADS), F32)],
                           axis=1).astype(BF16)
    wt = jnp.concatenate([w_q.T * (HEAD_DIM ** -0.5), w_v.T, w_f.T,
                          jnp.zeros((PART_ROWS - N_HEADS, D_MODEL), F32)], axis=0).astype(BF16)
    g_attn = attn_norm[0].reshape(1, D_MODEL)
    bfr = jnp.zeros((1, LANES), F32).at[0, :N_HEADS].set(b_forget[0])
    bfc = jnp.zeros((PART_ROWS, LANES), F32).at[:N_HEADS, :].set(b_forget[0][:, None])
    convw = jnp.zeros((8, CONV_W), F32).at[:3].set(conv_w[0])
    selk = _selection_matrix()

    meta_pad = jnp.zeros((1, META_PAD, D_MODEL), F32).at[0, :N_META].set(meta_tokens)
    zrow = jnp.zeros((8, LANES), F32)
    zcol = jnp.zeros((PART_ROWS, LANES), F32)
    zu = jnp.zeros((8, CONV_W), F32)
    kmeta, _, vmeta, _, cumrow_m, cumt_m, u_m = _proj_call(
        meta_pad, g_attn, wrow, wt, bfr, bfc, convw, selk, zrow, zcol, zu,
        t_rows=META_PAD, n_valid=N_META, emit_carry=True)
    cin_row = jnp.zeros((8, LANES), F32).at[0].set(cumrow_m[0, N_META - 1])
    cin_col = jnp.broadcast_to(cumt_m[0, :, N_META - 1:N_META], (PART_ROWS, LANES))
    uin = u_m[0, N_META - 8:N_META]

    kaug, qaug, vt, conv = _proj_call(
        x, g_attn, wrow, wt, bfr, bfc, convw, selk, cin_row, cin_col, uin,
        t_rows=PROJ_T, n_valid=PROJ_T, emit_carry=False)

    mixg = mix_norm[0]
    attn = _attn_call(qaug, kaug, vt, kmeta[0], vmeta[0, 0], mixg[:ATTN_W].reshape(1, ATTN_W))

    gidx = np.arange(CONV_W) // HEAD_DIM
    gsum = jnp.asarray(gidx[:, None] == gidx[None, :], BF16)
    wo = w_out[0].astype(BF16)
    wr = jnp.zeros((D_MODEL, LANES), F32)
    wr = wr.at[:, :N_GROUPS].set(w_router_group[0]).at[:, ROUTER_LANE0:ROUTER_LANE0 + N_EXPERTS].set(
        w_router_expert[0])
    wr_hi = wr.astype(BF16)
    wrh = jnp.concatenate([wr_hi, (wr - wr_hi.astype(F32)).astype(BF16)], axis=1)
    br = jnp.zeros((1, LANES), F32)
    br = br.at[0, :N_GROUPS].set(b_router_group[0]).at[0, ROUTER_LANE0:ROUTER_LANE0 + N_EXPERTS].set(
        b_router_expert[0])
    h, z, rinfo, counts, rinfo_t = _mix_call(
        attn.reshape(n_tok, ATTN_W), conv.reshape(n_tok, CONV_W), x.reshape(n_tok, D_MODEL),
        mixg[ATTN_W:].reshape(1, CONV_W), gsum, wo[:ATTN_W], wo[ATTN_W:], ffn_norm[0].reshape(1, D_MODEL),
        wrh, br)

    cnt = counts[0, ROUTER_LANE0:ROUTER_LANE0 + N_EXPERTS].astype(jnp.int32)
    padded = (cnt + EXPERT_BM - 1) // EXPERT_BM * EXPERT_BM
    pad_end = jnp.cumsum(padded)
    pad_start = pad_end - padded
    eid = rinfo_t[0:2].astype(jnp.int32)
    rank = rinfo_t[4:6].astype(jnp.int32)
    expert_ids = jnp.arange(N_EXPERTS, dtype=jnp.int32)
    dest = rank + jnp.sum(jnp.where(eid[None, :, :] == expert_ids[:, None, None], pad_start[:, None, None], 0),
                          axis=0)
    dest = dest.reshape(2, n_tok // DISPATCH_T, DISPATCH_T).transpose(1, 0, 2).reshape(
        n_tok // DISPATCH_T, 1, 2 * DISPATCH_T)
    n_blk = (2 * n_tok) // EXPERT_BM + N_EXPERTS
    nvb = (pad_end[-1] // EXPERT_BM).reshape(1)
    blk_row = jnp.minimum(jnp.arange(n_blk, dtype=jnp.int32), nvb - 1) * EXPERT_BM
    blk_e = jnp.sum(blk_row[:, None] >= pad_end[None, :], axis=1).astype(jnp.int32)

    xbuf = _dispatch_call(pad_end, cnt, dest, z, n_blk * EXPERT_BM)
    nonempty = cnt > 0
    ordinal = jnp.cumsum(nonempty.astype(jnp.int32)) - 1
    later = (expert_ids[None, :] > expert_ids[:, None]) & nonempty[None, :]
    next_ne = jnp.min(jnp.where(later, expert_ids[None, :], N_EXPERTS), axis=1)
    next_ne = jnp.where(next_ne == N_EXPERTS, -1, next_ne).astype(jnp.int32)
    blk_idx = jnp.arange(n_blk, dtype=jnp.int32)
    blk_first = ((blk_idx * EXPERT_BM == pad_start[blk_e]) & (blk_idx < nvb)).astype(jnp.int32)
    blk_slot = (ordinal[blk_e] % 2).astype(jnp.int32)
    blk_next = next_ne[blk_e]
    ybuf = _expert_call(blk_e, nvb, blk_first, blk_slot, blk_next, xbuf, w_gate[0], w_up[0], w_down[0])
    out = _combine_call(dest, h, rinfo, final_norm.reshape(1, D_MODEL), ybuf)
    return out.reshape(bsz, seq, D_MODEL)
```

```python
import functools

import numpy as np
import jax
import jax.numpy as jnp
from jax import lax
from jax.experimental import pallas as pl
from jax.experimental.pallas import tpu as pltpu

F32 = jnp.float32
BF16 = jnp.bfloat16

D_MODEL = 1024
HEAD_DIM = 64
N_HEADS = 8
ATTN_W = N_HEADS * HEAD_DIM
CONV_W = 512
N_META = 16
N_GROUPS = 4
EXPERTS_PER_GROUP = 8
N_EXPERTS = N_GROUPS * EXPERTS_PER_GROUP
D_EXPERT = 512
EPS = 1e-6
NEG_BIG = -1e30
LOG2E = 1.4426950408889634

LANES = 128
ROW_SUB = D_MODEL // LANES
HEAD_SLOT = 2 * HEAD_DIM
V_SLOT = HEAD_DIM + 16
V_AUG = N_HEADS * V_SLOT
AUG_W = N_HEADS * HEAD_SLOT
N_PARTS = 3
PART_ROWS = 16
ONES_LANE = 8
ROUTER_LANE0 = N_GROUPS

PROJ_T = 512
ATT_T = 256
META_PAD = ATT_T
MIX_T = 512
DISPATCH_T = 512
EXPERT_BM = 512
COMBINE_T = 512
DMA_UNROLL = 8
VMEM_LIMIT = 48 * 1024 * 1024


def _split3(x):
    hi = x.astype(BF16)
    r1 = x - hi.astype(F32)
    mid = r1.astype(BF16)
    r2 = r1 - mid.astype(F32)
    return hi, mid, r2.astype(BF16)


def _log_sigmoid(x):
    return jnp.minimum(x, 0.0) - jnp.log1p(jnp.exp(-jnp.abs(x)))


def _head_offset(h):
    return HEAD_DIM if h % 2 == 0 else 0


def _bias_slot(j, h):
    return N_PARTS + N_HEADS * j + h


def _selection_matrix():
    selk = np.zeros((N_PARTS * LANES, 2 * LANES), np.float32)
    for base in (_head_offset(0), LANES + _head_offset(1)):
        for j in range(N_PARTS):
            selk[ONES_LANE, base + j] = 1.0
            for h in range(N_HEADS):
                selk[j * LANES + h, base + _bias_slot(j, h)] = 1.0
    return jnp.asarray(selk, BF16)


def _proj_kernel(x_ref, g_ref, wrow_ref, wt_ref, bfr_ref, bfc_ref, convw_ref, selk_ref,
                 cin_row_ref, cin_col_ref, uin_ref,
                 kaug_ref, qaug_ref, vt_ref, conv_ref, *rest, t_rows, n_valid, emit_carry, tk):
    if emit_carry:
        cumrow_out, cumt_out, u_out, c_row, c_col, u_prev = rest
    else:
        c_row, c_col, u_prev = rest

    @pl.when(pl.program_id(1) == 0)
    def _():
        c_row[...] = cin_row_ref[...]
        c_col[...] = cin_col_ref[...]
        u_prev[...] = uin_ref[...]

    T = t_rows
    x = x_ref[...]
    ms = jnp.mean(x * x, axis=-1, keepdims=True)
    z = x * lax.rsqrt(ms + EPS) * g_ref[...]
    zb = z.astype(BF16)
    r = jnp.dot(zb, wrow_ref[...], preferred_element_type=F32)
    tt = lax.dot_general(wt_ref[...], zb, (((1,), (1,)), ((), ())),
                         preferred_element_type=F32)
    k = r[:, 0:ATTN_W]
    xc = r[:, ATTN_W:ATTN_W + CONV_W]
    bg = r[:, ATTN_W + CONV_W:ATTN_W + 2 * CONV_W]
    cg = r[:, ATTN_W + 2 * CONV_W:ATTN_W + 3 * CONV_W]
    fr = r[:, ATTN_W + 3 * CONV_W:ATTN_W + 3 * CONV_W + LANES]
    qt = tt[0:ATTN_W]
    vt = tt[ATTN_W:2 * ATTN_W]
    ft = tt[2 * ATTN_W:2 * ATTN_W + PART_ROWS]

    lane_r = lax.broadcasted_iota(jnp.int32, (T, LANES), 1)
    row_r = lax.broadcasted_iota(jnp.int32, (T, LANES), 0)
    lfr = jnp.where(lane_r < N_HEADS, _log_sigmoid(fr + bfr_ref[...]), 0.0)
    lfr = lfr + jnp.where(row_r == 0, c_row[0:1, :], 0.0)
    tri_r = lax.broadcasted_iota(jnp.int32, (T, T), 0)
    tri_c = lax.broadcasted_iota(jnp.int32, (T, T), 1)
    tri_l = (tri_c <= tri_r).astype(BF16)
    tri_u = (tri_r <= tri_c).astype(BF16)
    c3 = jnp.dot(tri_l, jnp.concatenate(_split3(lfr), axis=1), preferred_element_type=F32)
    cum_row = c3[:, 0:LANES] + c3[:, LANES:2 * LANES] + c3[:, 2 * LANES:3 * LANES]

    kb = cum_row * (-LOG2E)
    if n_valid < T:
        kb = kb + jnp.where(row_r >= n_valid, NEG_BIG, 0.0)
    kb = jnp.where(lane_r == ONES_LANE, 1.0, kb)
    e = jnp.dot(jnp.concatenate(_split3(kb), axis=1), selk_ref[...],
                preferred_element_type=F32)
    e_even = e[:, 0:LANES].astype(BF16)
    e_odd = e[:, LANES:2 * LANES].astype(BF16)
    for p in range(N_HEADS // 2):
        kp = k[:, p * LANES:(p + 1) * LANES].astype(BF16)
        kaug_ref[:, 2 * p * LANES:(2 * p + 1) * LANES] = jnp.where(lane_r < HEAD_DIM, kp, e_even)
        kaug_ref[:, (2 * p + 1) * LANES:(2 * p + 2) * LANES] = jnp.where(lane_r >= HEAD_DIM, kp, e_odd)

    lane_c = lax.broadcasted_iota(jnp.int32, (PART_ROWS, T), 1)
    row_c = lax.broadcasted_iota(jnp.int32, (PART_ROWS, T), 0)
    lfc = jnp.where(row_c < N_HEADS, _log_sigmoid(ft + bfc_ref[:, 0:1]), 0.0)
    lfc = lfc + jnp.where(lane_c == 0, c_col[:, 0:1], 0.0)
    c3c = jnp.dot(jnp.concatenate(_split3(lfc), axis=0), tri_u, preferred_element_type=F32)
    cum_t = c3c[0:PART_ROWS] + c3c[PART_ROWS:2 * PART_ROWS] + c3c[2 * PART_ROWS:3 * PART_ROWS]
    cq_parts = [p.astype(F32) for p in _split3(cum_t * LOG2E)]
    bias_rows = HEAD_DIM // 2
    row_b = lax.broadcasted_iota(jnp.int32, (bias_rows, T), 0)
    zero_rows = jnp.zeros((HEAD_DIM - bias_rows, T), BF16)
    for h in range(N_HEADS):
        qh = (qt[h * HEAD_DIM:(h + 1) * HEAD_DIM] * LOG2E).astype(BF16)
        own = (row_b == _bias_slot(0, h)) | (row_b == _bias_slot(1, h)) | (row_b == _bias_slot(2, h))
        bias = jnp.where(own, 1.0, 0.0)
        for j in range(N_PARTS):
            bias = jnp.where(row_b == j, cq_parts[j][h:h + 1, :], bias)
        lo = h * HEAD_SLOT + _head_offset(h)
        qaug_ref[lo:lo + bias_rows, :] = bias.astype(BF16)
        qaug_ref[lo + bias_rows:lo + HEAD_DIM, :] = zero_rows
        lo_q = h * HEAD_SLOT + (HEAD_DIM - _head_offset(h))
        qaug_ref[lo_q:lo_q + HEAD_DIM, :] = qh

    vtb = vt.astype(BF16)
    ones_rows = jnp.ones((V_SLOT - HEAD_DIM, tk), BF16)
    for s in range(T // tk):
        for h in range(N_HEADS):
            vt_ref[s, h * V_SLOT:h * V_SLOT + HEAD_DIM, :] = vtb[h * HEAD_DIM:(h + 1) * HEAD_DIM, s * tk:(s + 1) * tk]
            vt_ref[s, h * V_SLOT + HEAD_DIM:(h + 1) * V_SLOT, :] = ones_rows

    u = cg * xc
    row_u = lax.broadcasted_iota(jnp.int32, (T, CONV_W), 0)
    p1 = u_prev[7:8, :]
    p2 = u_prev[6:7, :]
    u1 = jnp.where(row_u == 0, p1, pltpu.roll(u, 1, axis=0))
    u2 = jnp.where(row_u == 0, p2, jnp.where(row_u == 1, p1, pltpu.roll(u, 2, axis=0)))
    zc = convw_ref[0:1, :] * u2 + convw_ref[1:2, :] * u1 + convw_ref[2:3, :] * u
    conv_ref[...] = bg * zc

    if emit_carry:
        cumrow_out[...] = cum_row
        cumt_out[...] = cum_t
        u_out[...] = u
    c_row[0:1, :] = cum_row[T - 1:T, :]
    c_col[...] = jnp.broadcast_to(cum_t[:, T - 1:T], (PART_ROWS, LANES))
    u_prev[...] = u[T - 8:T, :]


def _proj_call(xs, g, wrow, wt, bfr, bfc, convw, selk, cin_row, cin_col, uin,
               *, t_rows, n_valid, emit_carry):
    bn, ln, _ = xs.shape
    T = t_rows
    tk = min(ATT_T, T)
    nt = ln // T
    const2 = lambda b, t: (0, 0)
    in_specs = [
        pl.BlockSpec((None, T, D_MODEL), lambda b, t: (b, t, 0)),
        pl.BlockSpec(g.shape, const2),
        pl.BlockSpec(wrow.shape, const2),
        pl.BlockSpec(wt.shape, const2),
        pl.BlockSpec(bfr.shape, const2),
        pl.BlockSpec(bfc.shape, const2),
        pl.BlockSpec(convw.shape, const2),
        pl.BlockSpec(selk.shape, const2),
        pl.BlockSpec(cin_row.shape, const2),
        pl.BlockSpec(cin_col.shape, const2),
        pl.BlockSpec(uin.shape, const2),
    ]
    out_shape = [
        jax.ShapeDtypeStruct((bn, ln, AUG_W), BF16),
        jax.ShapeDtypeStruct((bn, AUG_W, ln), BF16),
        jax.ShapeDtypeStruct((bn, ln // tk, V_AUG, tk), BF16),
        jax.ShapeDtypeStruct((bn, ln, CONV_W), F32),
    ]
    out_specs = [
        pl.BlockSpec((None, T, AUG_W), lambda b, t: (b, t, 0)),
        pl.BlockSpec((None, AUG_W, T), lambda b, t: (b, 0, t)),
        pl.BlockSpec((None, T // tk, V_AUG, tk), lambda b, t: (b, t, 0, 0)),
        pl.BlockSpec((None, T, CONV_W), lambda b, t: (b, t, 0)),
    ]
    if emit_carry:
        out_shape += [
            jax.ShapeDtypeStruct((bn, ln, LANES), F32),
            jax.ShapeDtypeStruct((bn, PART_ROWS, ln), F32),
            jax.ShapeDtypeStruct((bn, ln, CONV_W), F32),
        ]
        out_specs += [
            pl.BlockSpec((None, T, LANES), lambda b, t: (b, t, 0)),
            pl.BlockSpec((None, PART_ROWS, T), lambda b, t: (b, 0, t)),
            pl.BlockSpec((None, T, CONV_W), lambda b, t: (b, t, 0)),
        ]
    kern = functools.partial(_proj_kernel, t_rows=T, n_valid=n_valid, emit_carry=emit_carry, tk=tk)
    return pl.pallas_call(
        kern,
        grid=(bn, nt),
        in_specs=in_specs,
        out_specs=out_specs,
        out_shape=out_shape,
        scratch_shapes=[pltpu.VMEM((8, LANES), F32), pltpu.VMEM((PART_ROWS, LANES), F32),
                        pltpu.VMEM((8, CONV_W), F32)],
        compiler_params=pltpu.CompilerParams(dimension_semantics=("arbitrary", "arbitrary"),
                                             vmem_limit_bytes=VMEM_LIMIT),
        name="proj_meta" if emit_carry else "proj",
    )(xs, g, wrow, wt, bfr, bfc, convw, selk, cin_row, cin_col, uin)


def _attn_kernel(q_ref, k_ref, v_ref, km_ref, vm_ref, g_ref, o_ref, s_s, m_s, acc_s):
    i = pl.program_id(1)
    T = ATT_T
    causal = (lax.broadcasted_iota(jnp.int32, (T, T), 0) <= lax.broadcasted_iota(jnp.int32, (T, T), 1))

    def scores(h, kblk, slot, masked):
        qa = q_ref[h * HEAD_SLOT:(h + 1) * HEAD_SLOT, :]
        s = jnp.dot(kblk, qa, preferred_element_type=F32)
        if masked:
            s = jnp.where(causal, s, NEG_BIG)
        s_s[slot, h] = s

    def consume(h, vblk, slot):
        s = s_s[slot, h]
        m_old = m_s[h, 0:1, :]
        m_new = jnp.maximum(m_old, jnp.max(s, axis=0, keepdims=True))
        alpha = jnp.exp2(m_old - m_new)
        pm = jnp.exp2(s - m_new).astype(BF16)
        acc_s[h] = alpha * acc_s[h] + jnp.dot(vblk, pm, preferred_element_type=F32)
        m_s[h, 0:1, :] = m_new

    def step(k_next, next_slot, masked, v_cur, cur_slot):
        ahead = 2
        for h in range(ahead):
            scores(h, k_next(h), next_slot, masked)
        for h in range(N_HEADS):
            consume(h, v_cur(h), cur_slot)
            if h + ahead < N_HEADS:
                scores(h + ahead, k_next(h + ahead), next_slot, masked)

    def k_of(j):
        row = pl.ds(pl.multiple_of(j * T, T), T)
        return lambda h: k_ref[row, h * HEAD_SLOT:(h + 1) * HEAD_SLOT]

    def v_of(j):
        return lambda h: v_ref[j, h * V_SLOT:(h + 1) * V_SLOT, :]

    k_meta = lambda h: km_ref[:, h * HEAD_SLOT:(h + 1) * HEAD_SLOT]
    v_meta = lambda h: vm_ref[h * V_SLOT:(h + 1) * V_SLOT, :]

    m_s[...] = jnp.full(m_s.shape, NEG_BIG, F32)
    acc_s[...] = jnp.zeros(acc_s.shape, F32)
    for h in range(N_HEADS):
        scores(h, k_meta(h), 0, False)

    def v_at(t):
        return lambda h: jnp.where(t == 0, v_meta(h), v_ref[jnp.maximum(t - 1, 0), h * V_SLOT:(h + 1) * V_SLOT, :])

    def pair_body(p, c):
        t0 = 2 * p
        step(k_of(t0), 1, False, v_at(t0), 0)
        step(k_of(t0 + 1), 0, False, v_at(t0 + 1), 1)
        return c

    lax.fori_loop(0, i // 2, pair_body, 0)

    @pl.when(i % 2 == 1)
    def _():
        step(k_of(i - 1), 1, False, v_at(i - 1), 0)
        step(k_of(i), 0, True, v_at(i), 1)
        for h in range(N_HEADS):
            consume(h, v_of(i)(h), 0)

    @pl.when(i % 2 == 0)
    def _():
        step(k_of(i), 1, True, v_at(i), 0)
        for h in range(N_HEADS):
            consume(h, v_of(i)(h), 1)

    for p in range(N_HEADS // 2):
        halves = []
        for h in (2 * p, 2 * p + 1):
            o = acc_s[h, 0:HEAD_DIM, :] * (1.0 / acc_s[h, HEAD_DIM:HEAD_DIM + 1, :])
            halves.append(o * lax.rsqrt(jnp.mean(o * o, axis=0, keepdims=True) + EPS))
        pair = jnp.concatenate(halves, axis=0)
        o_ref[:, p * LANES:(p + 1) * LANES] = (pair.T * g_ref[:, p * LANES:(p + 1) * LANES]).astype(o_ref.dtype)


def _attn_call(qaug, kaug, vt, kmeta, vmeta, gain):
    bn, _, ln = qaug.shape
    T = ATT_T
    nq = ln // T
    return pl.pallas_call(
        _attn_kernel,
        grid=(bn, nq),
        in_specs=[
            pl.BlockSpec((None, AUG_W, T), lambda b, i: (b, 0, i)),
            pl.BlockSpec((None, ln, AUG_W), lambda b, i: (b, 0, 0)),
            pl.BlockSpec((None, ln // T, V_AUG, T), lambda b, i: (b, 0, 0, 0)),
            pl.BlockSpec(kmeta.shape, lambda b, i: (0, 0)),
            pl.BlockSpec(vmeta.shape, lambda b, i: (0, 0)),
            pl.BlockSpec(gain.shape, lambda b, i: (0, 0)),
        ],
        out_specs=pl.BlockSpec((None, T, ATTN_W), lambda b, i: (b, i, 0)),
        out_shape=jax.ShapeDtypeStruct((bn, ln, ATTN_W), BF16),
        scratch_shapes=[pltpu.VMEM((2, N_HEADS, T, T), F32),
                        pltpu.VMEM((N_HEADS, 8, T), F32),
                        pltpu.VMEM((N_HEADS, V_SLOT, T), F32)],
        compiler_params=pltpu.CompilerParams(dimension_semantics=("arbitrary", "arbitrary"),
                                             vmem_limit_bytes=VMEM_LIMIT),
        name="attention",
    )(qaug, kaug, vt, kmeta, vmeta, gain)


def _mix_kernel(attn_ref, conv_ref, x_ref, gc_ref, gsum_ref, woa_ref, woc_ref, fg_ref,
                wrh_ref, br_ref, h_ref, z_ref, rinfo_ref, counts_ref, rinfo_t_ref, cnt, logits_s):
    T = MIX_T
    step = pl.program_id(0)

    @pl.when(step == 0)
    def _():
        cnt[...] = jnp.zeros_like(cnt)
        logits_s[...] = jnp.zeros_like(logits_s)

    logits = logits_s[...]
    c = conv_ref[...]
    c2 = c * c
    c2h = c2.astype(BF16)
    c2l = (c2 - c2h.astype(F32)).astype(BF16)
    ss = (jnp.dot(c2h, gsum_ref[...], preferred_element_type=F32)
          + jnp.dot(c2l, gsum_ref[...], preferred_element_type=F32))
    cn = (c * lax.rsqrt(ss * (1.0 / HEAD_DIM) + EPS) * gc_ref[...]).astype(BF16)
    hadd = (jnp.dot(attn_ref[...], woa_ref[...], preferred_element_type=F32)
            + jnp.dot(cn, woc_ref[...], preferred_element_type=F32))
    h = x_ref[...] + hadd
    h_ref[...] = h
    z = h * lax.rsqrt(jnp.mean(h * h, axis=-1, keepdims=True) + EPS) * fg_ref[...]
    z_ref[...] = z.reshape(T, ROW_SUB, LANES)
    zh = z.astype(BF16)
    zl = (z - zh.astype(F32)).astype(BF16)
    both = jnp.dot(zh, wrh_ref[...], preferred_element_type=F32)
    logits_s[...] = (both[:, 0:LANES] + both[:, LANES:2 * LANES]
                     + jnp.dot(zl, wrh_ref[:, 0:LANES], preferred_element_type=F32)) + br_ref[...]

    lane = lax.broadcasted_iota(jnp.int32, (T, LANES), 1).astype(F32)
    big = float(LANES)
    gl = jnp.where(lane < N_GROUPS, logits, -jnp.inf)
    gmax = jnp.max(gl, axis=-1, keepdims=True)
    gsum = jnp.sum(jnp.exp(gl - gmax), axis=-1, keepdims=True)
    g_w = 1.0 / gsum
    g_idx = jnp.min(jnp.where(gl == gmax, lane, big), axis=-1, keepdims=True)
    e_lo = ROUTER_LANE0 + EXPERTS_PER_GROUP * g_idx
    emask = (lane >= e_lo) & (lane < e_lo + EXPERTS_PER_GROUP)
    el = jnp.where(emask, logits, -jnp.inf)
    emax = jnp.max(el, axis=-1, keepdims=True)
    eexp = jnp.exp(el - emax)
    probs = eexp / jnp.sum(eexp, axis=-1, keepdims=True)
    probs = jnp.where(emask, probs, -1.0)
    p1 = jnp.max(probs, axis=-1, keepdims=True)
    i1 = jnp.min(jnp.where(probs == p1, lane, big), axis=-1, keepdims=True)
    probs2 = jnp.where(lane == i1, -1.0, probs)
    p2 = jnp.max(probs2, axis=-1, keepdims=True)
    i2 = jnp.min(jnp.where(probs2 == p2, lane, big), axis=-1, keepdims=True)
    psum = p1 + p2
    w1 = g_w * (p1 / psum)
    w2 = g_w * (p2 / psum)

    hit = ((lane == i1) | (lane == i2)).astype(F32)
    tri_r = lax.broadcasted_iota(jnp.int32, (T, T), 0)
    tri_c = lax.broadcasted_iota(jnp.int32, (T, T), 1)
    strict_l = (tri_c < tri_r).astype(BF16)
    before = jnp.dot(strict_l, hit.astype(BF16), preferred_element_type=F32) + cnt[0:1, :]
    r1 = jnp.sum(jnp.where(lane == i1, before, 0.0), axis=-1, keepdims=True)
    r2 = jnp.sum(jnp.where(lane == i2, before, 0.0), axis=-1, keepdims=True)
    cnt[0:1, :] = cnt[0:1, :] + jnp.sum(hit, axis=0, keepdims=True) * (step > 0).astype(F32)
    counts_ref[...] = cnt[...]

    e1 = i1 - ROUTER_LANE0
    e2 = i2 - ROUTER_LANE0
    rinfo = jnp.where(lane == 0, e1, jnp.where(lane == 1, e2, jnp.where(lane == 2, w1, jnp.where(
        lane == 3, w2, jnp.where(lane == 4, r1, jnp.where(lane == 5, r2, 0.0))))))
    rinfo_ref[...] = rinfo
    rinfo_t_ref[...] = rinfo.T[0:8, :]


def _mix_call(attn, conv, x, gc, gsum, woa, woc, fg, wrh, br):
    n = x.shape[0]
    T = MIX_T
    nt = n // T
    const = lambda i: (0, 0)
    cur = lambda i: jnp.minimum(i, nt - 1)
    prev = lambda i: jnp.maximum(i - 1, 0)
    row = lambda w: pl.BlockSpec((T, w), lambda i: (cur(i), 0))
    return pl.pallas_call(
        _mix_kernel,
        grid=(nt + 1,),
        in_specs=[row(ATTN_W), row(CONV_W), row(D_MODEL)] + [
            pl.BlockSpec(a.shape, const) for a in (gc, gsum, woa, woc, fg, wrh, br)],
        out_specs=[row(D_MODEL), pl.BlockSpec((T, ROW_SUB, LANES), lambda i: (cur(i), 0, 0)),
                   pl.BlockSpec((T, LANES), lambda i: (prev(i), 0)),
                   pl.BlockSpec((8, LANES), const), pl.BlockSpec((8, T), lambda i: (0, prev(i)))],
        out_shape=[jax.ShapeDtypeStruct((n, D_MODEL), F32), jax.ShapeDtypeStruct((n, ROW_SUB, LANES), F32),
                   jax.ShapeDtypeStruct((n, LANES), F32), jax.ShapeDtypeStruct((8, LANES), F32),
                   jax.ShapeDtypeStruct((8, n), F32)],
        scratch_shapes=[pltpu.VMEM((8, LANES), F32), pltpu.VMEM((T, LANES), F32)],
        compiler_params=pltpu.CompilerParams(dimension_semantics=("arbitrary",),
                                             vmem_limit_bytes=VMEM_LIMIT),
        name="mix",
    )(attn, conv, x, gc, gsum, woa, woc, fg, wrh, br)


def _dispatch_kernel(pad_end_ref, cnt_ref, dest_ref, z_ref, xbuf_ref, zero_v, sem, zsem):
    T = DISPATCH_T

    @pl.when(pl.program_id(0) == 0)
    def _():
        zero_v[...] = jnp.zeros_like(zero_v)

        def zero_copy(e):
            first = pl.multiple_of(jnp.maximum(pad_end_ref[e] - EXPERT_BM, 0), EXPERT_BM)
            return pltpu.make_async_copy(zero_v, xbuf_ref.at[pl.ds(first, EXPERT_BM)], zsem)

        for e in range(N_EXPERTS):
            @pl.when(cnt_ref[e] > 0)
            def _():
                zero_copy(e).start()
        for e in range(N_EXPERTS):
            @pl.when(cnt_ref[e] > 0)
            def _():
                zero_copy(e).wait()

        def tail_copy(b):
            return pltpu.make_async_copy(
                zero_v, xbuf_ref.at[pl.ds(pl.multiple_of(b * EXPERT_BM, EXPERT_BM), EXPERT_BM)], zsem)

        n_used = pad_end_ref[N_EXPERTS - 1] // EXPERT_BM
        n_all = xbuf_ref.shape[0] // EXPERT_BM
        lax.fori_loop(n_used, n_all, lambda b, c: (tail_copy(b).start(), c)[1], 0)
        lax.fori_loop(n_used, n_all, lambda b, c: (tail_copy(b).wait(), c)[1], 0)

    def row_copy(r, slot):
        d = dest_ref[0, slot * T + r]
        return pltpu.make_async_copy(z_ref.at[pl.ds(r, 1)], xbuf_ref.at[pl.ds(d, 1)], sem)

    def start(r, c):
        row_copy(r, 0).start(priority=0)
        row_copy(r, 1).start(priority=1)
        return c

    def wait(r, c):
        row_copy(r, 0).wait()
        row_copy(r, 1).wait()
        return c

    lax.fori_loop(0, T, start, 0, unroll=DMA_UNROLL)
    lax.fori_loop(0, T, wait, 0, unroll=DMA_UNROLL)


def _dispatch_call(pad_end, cnt, dest, z, n_rows):
    n = z.shape[0]
    T = DISPATCH_T
    grid_spec = pltpu.PrefetchScalarGridSpec(
        num_scalar_prefetch=2,
        grid=(n // T,),
        in_specs=[
            pl.BlockSpec((None, 1, 2 * T), lambda i, pe, ct: (i, 0, 0), memory_space=pltpu.SMEM),
            pl.BlockSpec((T, ROW_SUB, LANES), lambda i, pe, ct: (i, 0, 0)),
        ],
        out_specs=pl.BlockSpec(memory_space=pl.ANY),
        scratch_shapes=[pltpu.VMEM((EXPERT_BM, ROW_SUB, LANES), F32), pltpu.SemaphoreType.DMA(()),
                        pltpu.SemaphoreType.DMA(())],
    )
    return pl.pallas_call(
        _dispatch_kernel,
        grid_spec=grid_spec,
        out_shape=jax.ShapeDtypeStruct((n_rows, ROW_SUB, LANES), F32),
        compiler_params=pltpu.CompilerParams(dimension_semantics=("arbitrary",), vmem_limit_bytes=VMEM_LIMIT),
        name="dispatch",
    )(pad_end, cnt, dest, z)


def _expert_kernel(blk_e_ref, nvb_ref, first_ref, slot_ref, next_ref, x_ref, wg_hbm, wu_hbm, wd_hbm, y_ref,
                   wg_v, wu_v, wd_v, wgu_s, wd_s, sem):
    blk = pl.program_id(0)
    valid = blk < nvb_ref[0]

    def fetch(e, s):
        return (pltpu.make_async_copy(wg_hbm.at[e], wg_v.at[s], sem.at[s]),
                pltpu.make_async_copy(wu_hbm.at[e], wu_v.at[s], sem.at[s]),
                pltpu.make_async_copy(wd_hbm.at[e], wd_v.at[s], sem.at[s]))

    @pl.when(blk == 0)
    def _():
        for c in fetch(blk_e_ref[0], 0):
            c.start()

    @pl.when(valid & (first_ref[blk] != 0))
    def _():
        s = slot_ref[blk]
        for c in fetch(blk_e_ref[blk], s):
            c.wait()
        wgu_s[:, 0:D_EXPERT] = wg_v[s].astype(BF16)
        wgu_s[:, D_EXPERT:2 * D_EXPERT] = wu_v[s].astype(BF16)
        wd_s[...] = wd_v[s].astype(BF16)

        @pl.when(next_ref[blk] >= 0)
        def _():
            for c in fetch(next_ref[blk], 1 - s):
                c.start()

    @pl.when(valid)
    def _():
        xb = x_ref[...].reshape(EXPERT_BM, D_MODEL).astype(BF16)
        gu = jnp.dot(xb, wgu_s[...], preferred_element_type=F32)
        g = gu[:, 0:D_EXPERT]
        u = gu[:, D_EXPERT:2 * D_EXPERT]
        a = (g * jax.nn.sigmoid(g)) * u
        y = jnp.dot(a.astype(BF16), wd_s[...], preferred_element_type=F32)
        y_ref[...] = y.reshape(EXPERT_BM, ROW_SUB, LANES)


def _expert_call(blk_e, nvb, blk_first, blk_slot, blk_next, xbuf, w_gate, w_up, w_down):
    rows = xbuf.shape[0]
    bm = EXPERT_BM
    used = lambda i, be, nv, bf, bs, bn: (jnp.minimum(i, nv[0] - 1), 0, 0)
    grid_spec = pltpu.PrefetchScalarGridSpec(
        num_scalar_prefetch=5,
        grid=(rows // bm,),
        in_specs=[
            pl.BlockSpec((bm, ROW_SUB, LANES), used),
            pl.BlockSpec(memory_space=pl.ANY),
            pl.BlockSpec(memory_space=pl.ANY),
            pl.BlockSpec(memory_space=pl.ANY),
        ],
        out_specs=pl.BlockSpec((bm, ROW_SUB, LANES), used),
        scratch_shapes=[pltpu.VMEM((2, D_MODEL, D_EXPERT), F32), pltpu.VMEM((2, D_MODEL, D_EXPERT), F32),
                        pltpu.VMEM((2, D_EXPERT, D_MODEL), F32),
                        pltpu.VMEM((D_MODEL, 2 * D_EXPERT), BF16), pltpu.VMEM((D_EXPERT, D_MODEL), BF16),
                        pltpu.SemaphoreType.DMA((2,))],
    )
    return pl.pallas_call(
        _expert_kernel,
        grid_spec=grid_spec,
        out_shape=jax.ShapeDtypeStruct((rows, ROW_SUB, LANES), F32),
        input_output_aliases={5: 0},
        compiler_params=pltpu.CompilerParams(dimension_semantics=("arbitrary",),
                                             vmem_limit_bytes=VMEM_LIMIT),
        name="experts",
    )(blk_e, nvb, blk_first, blk_slot, blk_next, xbuf, w_gate, w_up, w_down)


def _combine_kernel(dest_ref, dest_next_ref, h_ref, rinfo_ref, fn_ref, ybuf_ref, o_ref, ybuf_v, sem):
    T = COMBINE_T
    i = pl.program_id(0)
    cur = i % 2

    def row_copy(dref, buf, r, k):
        d = dref[0, k * T + r]
        return pltpu.make_async_copy(ybuf_ref.at[pl.ds(d, 1)], ybuf_v.at[buf, k, pl.ds(r, 1)], sem.at[buf])

    def issue(dref, buf):
        def start(r, c):
            row_copy(dref, buf, r, 0).start(priority=0)
            row_copy(dref, buf, r, 1).start(priority=1)
            return c
        lax.fori_loop(0, T, start, 0, unroll=DMA_UNROLL)

    @pl.when(i == 0)
    def _():
        issue(dest_ref, 0)

    @pl.when(i + 1 < pl.num_programs(0))
    def _():
        issue(dest_next_ref, 1 - cur)

    def wait(r, c):
        row_copy(dest_ref, cur, r, 0).wait()
        row_copy(dest_ref, cur, r, 1).wait()
        return c

    lax.fori_loop(0, T, wait, 0, unroll=DMA_UNROLL)
    rinfo = rinfo_ref[...]
    w1 = rinfo[:, 2:3]
    w2 = rinfo[:, 3:4]
    y1 = ybuf_v[cur, 0].reshape(T, D_MODEL)
    y2 = ybuf_v[cur, 1].reshape(T, D_MODEL)
    h = h_ref[...] + (y1 * w1 + y2 * w2)
    o_ref[...] = h * lax.rsqrt(jnp.mean(h * h, axis=-1, keepdims=True) + EPS) * fn_ref[...]


def _combine_call(dest, h, rinfo, fn, ybuf):
    n = h.shape[0]
    T = COMBINE_T
    nt = n // T
    return pl.pallas_call(
        _combine_kernel,
        grid=(nt,),
        in_specs=[
            pl.BlockSpec((None, 1, 2 * T), lambda i: (i, 0, 0), memory_space=pltpu.SMEM),
            pl.BlockSpec((None, 1, 2 * T), lambda i: (jnp.minimum(i + 1, nt - 1), 0, 0), memory_space=pltpu.SMEM),
            pl.BlockSpec((T, D_MODEL), lambda i: (i, 0)),
            pl.BlockSpec((T, LANES), lambda i: (i, 0)),
            pl.BlockSpec(fn.shape, lambda i: (0, 0)),
            pl.BlockSpec(memory_space=pl.ANY),
        ],
        out_specs=pl.BlockSpec((T, D_MODEL), lambda i: (i, 0)),
        out_shape=jax.ShapeDtypeStruct((n, D_MODEL), F32),
        scratch_shapes=[pltpu.VMEM((2, 2, T, ROW_SUB, LANES), F32), pltpu.SemaphoreType.DMA((2,))],
        compiler_params=pltpu.CompilerParams(dimension_semantics=("arbitrary",),
                                             vmem_limit_bytes=VMEM_LIMIT),
        name="combine",
    )(dest, dest, h, rinfo, fn, ybuf)


def kernel(x, meta_tokens, attn_norm, w_in, b_forget, conv_w, mix_norm, w_out, ffn_norm,
           w_router_group, b_router_group, w_router_expert, b_router_expert,
           w_gate, w_up, w_down, final_norm):
    bsz, seq, d = x.shape
    assert d == D_MODEL and seq % PROJ_T == 0 and meta_tokens.shape[0] == N_META
    assert attn_norm.shape[0] == 1, "single-layer block"
    n_tok = bsz * seq

    wi = w_in[0]
    o = 0
    w_q = wi[:, o:o + ATTN_W]; o += ATTN_W
    w_k = wi[:, o:o + ATTN_W]; o += ATTN_W
    w_v = wi[:, o:o + ATTN_W]; o += ATTN_W
    w_f = wi[:, o:o + N_HEADS]; o += N_HEADS
    w_xc = wi[:, o:o + CONV_W]; o += CONV_W
    w_bg = wi[:, o:o + CONV_W]; o += CONV_W
    w_cg = wi[:, o:o + CONV_W]
    wrow = jnp.concatenate([w_k, w_xc, w_bg, w_cg, w_f, jnp.zeros((D_MODEL, LANES - N_HEADS), F32)],
                           axis=1).astype(BF16)
    wt = jnp.concatenate([w_q.T * (HEAD_DIM ** -0.5), w_v.T, w_f.T,
                          jnp.zeros((PART_ROWS - N_HEADS, D_MODEL), F32)], axis=0).astype(BF16)
    g_attn = attn_norm[0].reshape(1, D_MODEL)
    bfr = jnp.zeros((1, LANES), F32).at[0, :N_HEADS].set(b_forget[0])
    bfc = jnp.zeros((PART_ROWS, LANES), F32).at[:N_HEADS, :].set(b_forget[0][:, None])
    convw = jnp.zeros((8, CONV_W), F32).at[:3].set(conv_w[0])
    selk = _selection_matrix()

    meta_pad = jnp.zeros((1, META_PAD, D_MODEL), F32).at[0, :N_META].set(meta_tokens)
    zrow = jnp.zeros((8, LANES), F32)
    zcol = jnp.zeros((PART_ROWS, LANES), F32)
    zu = jnp.zeros((8, CONV_W), F32)
    kmeta, _, vmeta, _, cumrow_m, cumt_m, u_m = _proj_call(
        meta_pad, g_attn, wrow, wt, bfr, bfc, convw, selk, zrow, zcol, zu,
        t_rows=META_PAD, n_valid=N_META, emit_carry=True)
    cin_row = jnp.zeros((8, LANES), F32).at[0].set(cumrow_m[0, N_META - 1])
    cin_col = jnp.broadcast_to(cumt_m[0, :, N_META - 1:N_META], (PART_ROWS, LANES))
    uin = u_m[0, N_META - 8:N_META]

    kaug, qaug, vt, conv = _proj_call(
        x, g_attn, wrow, wt, bfr, bfc, convw, selk, cin_row, cin_col, uin,
        t_rows=PROJ_T, n_valid=PROJ_T, emit_carry=False)

    mixg = mix_norm[0]
    attn = _attn_call(qaug, kaug, vt, kmeta[0], vmeta[0, 0], mixg[:ATTN_W].reshape(1, ATTN_W))

    gidx = np.arange(CONV_W) // HEAD_DIM
    gsum = jnp.asarray(gidx[:, None] == gidx[None, :], BF16)
    wo = w_out[0].astype(BF16)
    wr = jnp.zeros((D_MODEL, LANES), F32)
    wr = wr.at[:, :N_GROUPS].set(w_router_group[0]).at[:, ROUTER_LANE0:ROUTER_LANE0 + N_EXPERTS].set(
        w_router_expert[0])
    wr_hi = wr.astype(BF16)
    wrh = jnp.concatenate([wr_hi, (wr - wr_hi.astype(F32)).astype(BF16)], axis=1)
    br = jnp.zeros((1, LANES), F32)
    br = br.at[0, :N_GROUPS].set(b_router_group[0]).at[0, ROUTER_LANE0:ROUTER_LANE0 + N_EXPERTS].set(
        b_router_expert[0])
    h, z, rinfo, counts, rinfo_t = _mix_call(
        attn.reshape(n_tok, ATTN_W), conv.reshape(n_tok, CONV_W), x.reshape(n_tok, D_MODEL),
        mixg[ATTN_W:].reshape(1, CONV_W), gsum, wo[:ATTN_W], wo[ATTN_W:], ffn_norm[0].reshape(1, D_MODEL),
        wrh, br)

    cnt = counts[0, ROUTER_LANE0:ROUTER_LANE0 + N_EXPERTS].astype(jnp.int32)
    padded = (cnt + EXPERT_BM - 1) // EXPERT_BM * EXPERT_BM
    pad_end = jnp.cumsum(padded)
    pad_start = pad_end - padded
    eid = rinfo_t[0:2].astype(jnp.int32)
    rank = rinfo_t[4:6].astype(jnp.int32)
    expert_ids = jnp.arange(N_EXPERTS, dtype=jnp.int32)
    dest = rank + jnp.sum(jnp.where(eid[None, :, :] == expert_ids[:, None, None], pad_start[:, None, None], 0),
                          axis=0)
    dest = dest.reshape(2, n_tok // DISPATCH_T, DISPATCH_T).transpose(1, 0, 2).reshape(
        n_tok // DISPATCH_T, 1, 2 * DISPATCH_T)
    n_blk = (2 * n_tok) // EXPERT_BM + N_EXPERTS
    nvb = (pad_end[-1] // EXPERT_BM).reshape(1)
    blk_row = jnp.minimum(jnp.arange(n_blk, dtype=jnp.int32), nvb - 1) * EXPERT_BM
    blk_e = jnp.sum(blk_row[:, None] >= pad_end[None, :], axis=1).astype(jnp.int32)

    xbuf = _dispatch_call(pad_end, cnt, dest, z, n_blk * EXPERT_BM)
    nonempty = cnt > 0
    ordinal = jnp.cumsum(nonempty.astype(jnp.int32)) - 1
    later = (expert_ids[None, :] > expert_ids[:, None]) & nonempty[None, :]
    next_ne = jnp.min(jnp.where(later, expert_ids[None, :], N_EXPERTS), axis=1)
    next_ne = jnp.where(next_ne == N_EXPERTS, -1, next_ne).astype(jnp.int32)
    blk_idx = jnp.arange(n_blk, dtype=jnp.int32)
    blk_first = ((blk_idx * EXPERT_BM == pad_start[blk_e]) & (blk_idx < nvb)).astype(jnp.int32)
    blk_slot = (ordinal[blk_e] % 2).astype(jnp.int32)
    blk_next = next_ne[blk_e]
    ybuf = _expert_call(blk_e, nvb, blk_first, blk_slot, blk_next, xbuf, w_gate[0], w_up[0], w_down[0])
    out = _combine_call(dest, h, rinfo, final_norm.reshape(1, D_MODEL), ybuf)
    return out.reshape(bsz, seq, D_MODEL)
```

```python
import functools

import numpy as np
import jax
import jax.numpy as jnp
from jax import lax
from jax.experimental import pallas as pl
from jax.experimental.pallas import tpu as pltpu

F32 = jnp.float32
BF16 = jnp.bfloat16

D_MODEL = 1024
HEAD_DIM = 64
N_HEADS = 8
ATTN_W = N_HEADS * HEAD_DIM
CONV_W = 512
N_META = 16
N_GROUPS = 4
EXPERTS_PER_GROUP = 8
N_EXPERTS = N_GROUPS * EXPERTS_PER_GROUP
D_EXPERT = 512
EPS = 1e-6
NEG_BIG = -1e30
LOG2E = 1.4426950408889634

LANES = 128
ROW_SUB = D_MODEL // LANES
HEAD_SLOT = 2 * HEAD_DIM
V_SLOT = HEAD_DIM + 16
V_AUG = N_HEADS * V_SLOT
AUG_W = N_HEADS * HEAD_SLOT
N_PARTS = 3
PART_ROWS = 16
ONES_LANE = 8
ROUTER_LANE0 = N_GROUPS

PROJ_T = 512
ATT_T = 256
META_PAD = ATT_T
MIX_T = 512
DISPATCH_T = 512
EXPERT_BM = 512
COMBINE_T = 512
DMA_UNROLL = 8
VMEM_LIMIT = 48 * 1024 * 1024


def _split3(x):
    hi = x.astype(BF16)
    r1 = x - hi.astype(F32)
    mid = r1.astype(BF16)
    r2 = r1 - mid.astype(F32)
    return hi, mid, r2.astype(BF16)


def _log_sigmoid(x):
    return jnp.minimum(x, 0.0) - jnp.log1p(jnp.exp(-jnp.abs(x)))


def _head_offset(h):
    return HEAD_DIM if h % 2 == 0 else 0


def _bias_slot(j, h):
    return N_PARTS + N_HEADS * j + h


def _selection_matrix():
    selk = np.zeros((N_PARTS * LANES, 2 * LANES), np.float32)
    for base in (_head_offset(0), LANES + _head_offset(1)):
        for j in range(N_PARTS):
            selk[ONES_LANE, base + j] = 1.0
            for h in range(N_HEADS):
                selk[j * LANES + h, base + _bias_slot(j, h)] = 1.0
    return jnp.asarray(selk, BF16)


def _proj_kernel(x_ref, g_ref, wrow_ref, wt_ref, bfr_ref, bfc_ref, convw_ref, selk_ref,
                 cin_row_ref, cin_col_ref, uin_ref,
                 kaug_ref, qaug_ref, vt_ref, conv_ref, *rest, t_rows, n_valid, emit_carry, tk):
    if emit_carry:
        cumrow_out, cumt_out, u_out, c_row, c_col, u_prev = rest
    else:
        c_row, c_col, u_prev = rest

    @pl.when(pl.program_id(1) == 0)
    def _():
        c_row[...] = cin_row_ref[...]
        c_col[...] = cin_col_ref[...]
        u_prev[...] = uin_ref[...]

    T = t_rows
    x = x_ref[...]
    ms = jnp.mean(x * x, axis=-1, keepdims=True)
    z = x * lax.rsqrt(ms + EPS) * g_ref[...]
    zb = z.astype(BF16)
    r = jnp.dot(zb, wrow_ref[...], preferred_element_type=F32)
    tt = lax.dot_general(wt_ref[...], zb, (((1,), (1,)), ((), ())),
                         preferred_element_type=F32)
    k = r[:, 0:ATTN_W]
    xc = r[:, ATTN_W:ATTN_W + CONV_W]
    bg = r[:, ATTN_W + CONV_W:ATTN_W + 2 * CONV_W]
    cg = r[:, ATTN_W + 2 * CONV_W:ATTN_W + 3 * CONV_W]
    fr = r[:, ATTN_W + 3 * CONV_W:ATTN_W + 3 * CONV_W + LANES]
    qt = tt[0:ATTN_W]
    vt = tt[ATTN_W:2 * ATTN_W]
    ft = tt[2 * ATTN_W:2 * ATTN_W + PART_ROWS]

    lane_r = lax.broadcasted_iota(jnp.int32, (T, LANES), 1)
    row_r = lax.broadcasted_iota(jnp.int32, (T, LANES), 0)
    lfr = jnp.where(lane_r < N_HEADS, _log_sigmoid(fr + bfr_ref[...]), 0.0)
    lfr = lfr + jnp.where(row_r == 0, c_row[0:1, :], 0.0)
    tri_r = lax.broadcasted_iota(jnp.int32, (T, T), 0)
    tri_c = lax.broadcasted_iota(jnp.int32, (T, T), 1)
    tri_l = (tri_c <= tri_r).astype(BF16)
    tri_u = (tri_r <= tri_c).astype(BF16)
    c3 = jnp.dot(tri_l, jnp.concatenate(_split3(lfr), axis=1), preferred_element_type=F32)
    cum_row = c3[:, 0:LANES] + c3[:, LANES:2 * LANES] + c3[:, 2 * LANES:3 * LANES]

    kb = cum_row * (-LOG2E)
    if n_valid < T:
        kb = kb + jnp.where(row_r >= n_valid, NEG_BIG, 0.0)
    kb = jnp.where(lane_r == ONES_LANE, 1.0, kb)
    e = jnp.dot(jnp.concatenate(_split3(kb), axis=1), selk_ref[...],
                preferred_element_type=F32)
    e_even = e[:, 0:LANES].astype(BF16)
    e_odd = e[:, LANES:2 * LANES].astype(BF16)
    for p in range(N_HEADS // 2):
        kp = k[:, p * LANES:(p + 1) * LANES].astype(BF16)
        kaug_ref[:, 2 * p * LANES:(2 * p + 1) * LANES] = jnp.where(lane_r < HEAD_DIM, kp, e_even)
        kaug_ref[:, (2 * p + 1) * LANES:(2 * p + 2) * LANES] = jnp.where(lane_r >= HEAD_DIM, kp, e_odd)

    lane_c = lax.broadcasted_iota(jnp.int32, (PART_ROWS, T), 1)
    row_c = lax.broadcasted_iota(jnp.int32, (PART_ROWS, T), 0)
    lfc = jnp.where(row_c < N_HEADS, _log_sigmoid(ft + bfc_ref[:, 0:1]), 0.0)
    lfc = lfc + jnp.where(lane_c == 0, c_col[:, 0:1], 0.0)
    c3c = jnp.dot(jnp.concatenate(_split3(lfc), axis=0), tri_u, preferred_element_type=F32)
    cum_t = c3c[0:PART_ROWS] + c3c[PART_ROWS:2 * PART_ROWS] + c3c[2 * PART_ROWS:3 * PART_ROWS]
    cq_parts = [p.astype(F32) for p in _split3(cum_t * LOG2E)]
    bias_rows = HEAD_DIM // 2
    row_b = lax.broadcasted_iota(jnp.int32, (bias_rows, T), 0)
    zero_rows = jnp.zeros((HEAD_DIM - bias_rows, T), BF16)
    for h in range(N_HEADS):
        qh = (qt[h * HEAD_DIM:(h + 1) * HEAD_DIM] * LOG2E).astype(BF16)
        own = (row_b == _bias_slot(0, h)) | (row_b == _bias_slot(1, h)) | (row_b == _bias_slot(2, h))
        bias = jnp.where(own, 1.0, 0.0)
        for j in range(N_PARTS):
            bias = jnp.where(row_b == j, cq_parts[j][h:h + 1, :], bias)
        lo = h * HEAD_SLOT + _head_offset(h)
        qaug_ref[lo:lo + bias_rows, :] = bias.astype(BF16)
        qaug_ref[lo + bias_rows:lo + HEAD_DIM, :] = zero_rows
        lo_q = h * HEAD_SLOT + (HEAD_DIM - _head_offset(h))
        qaug_ref[lo_q:lo_q + HEAD_DIM, :] = qh

    vtb = vt.astype(BF16)
    ones_rows = jnp.ones((V_SLOT - HEAD_DIM, tk), BF16)
    for s in range(T // tk):
        for h in range(N_HEADS):
            vt_ref[s, h * V_SLOT:h * V_SLOT + HEAD_DIM, :] = vtb[h * HEAD_DIM:(h + 1) * HEAD_DIM, s * tk:(s + 1) * tk]
            vt_ref[s, h * V_SLOT + HEAD_DIM:(h + 1) * V_SLOT, :] = ones_rows

    u = cg * xc
    row_u = lax.broadcasted_iota(jnp.int32, (T, CONV_W), 0)
    p1 = u_prev[7:8, :]
    p2 = u_prev[6:7, :]
    u1 = jnp.where(row_u == 0, p1, pltpu.roll(u, 1, axis=0))
    u2 = jnp.where(row_u == 0, p2, jnp.where(row_u == 1, p1, pltpu.roll(u, 2, axis=0)))
    zc = convw_ref[0:1, :] * u2 + convw_ref[1:2, :] * u1 + convw_ref[2:3, :] * u
    conv_ref[...] = bg * zc

    if emit_carry:
        cumrow_out[...] = cum_row
        cumt_out[...] = cum_t
        u_out[...] = u
    c_row[0:1, :] = cum_row[T - 1:T, :]
    c_col[...] = jnp.broadcast_to(cum_t[:, T - 1:T], (PART_ROWS, LANES))
    u_prev[...] = u[T - 8:T, :]


def _proj_call(xs, g, wrow, wt, bfr, bfc, convw, selk, cin_row, cin_col, uin,
               *, t_rows, n_valid, emit_carry):
    bn, ln, _ = xs.shape
    T = t_rows
    tk = min(ATT_T, T)
    nt = ln // T
    const2 = lambda b, t: (0, 0)
    in_specs = [
        pl.BlockSpec((None, T, D_MODEL), lambda b, t: (b, t, 0)),
        pl.BlockSpec(g.shape, const2),
        pl.BlockSpec(wrow.shape, const2),
        pl.BlockSpec(wt.shape, const2),
        pl.BlockSpec(bfr.shape, const2),
        pl.BlockSpec(bfc.shape, const2),
        pl.BlockSpec(convw.shape, const2),
        pl.BlockSpec(selk.shape, const2),
        pl.BlockSpec(cin_row.shape, const2),
        pl.BlockSpec(cin_col.shape, const2),
        pl.BlockSpec(uin.shape, const2),
    ]
    out_shape = [
        jax.ShapeDtypeStruct((bn, ln, AUG_W), BF16),
        jax.ShapeDtypeStruct((bn, AUG_W, ln), BF16),
        jax.ShapeDtypeStruct((bn, ln // tk, V_AUG, tk), BF16),
        jax.ShapeDtypeStruct((bn, ln, CONV_W), F32),
    ]
    out_specs = [
        pl.BlockSpec((None, T, AUG_W), lambda b, t: (b, t, 0)),
        pl.BlockSpec((None, AUG_W, T), lambda b, t: (b, 0, t)),
        pl.BlockSpec((None, T // tk, V_AUG, tk), lambda b, t: (b, t, 0, 0)),
        pl.BlockSpec((None, T, CONV_W), lambda b, t: (b, t, 0)),
    ]
    if emit_carry:
        out_shape += [
            jax.ShapeDtypeStruct((bn, ln, LANES), F32),
            jax.ShapeDtypeStruct((bn, PART_ROWS, ln), F32),
            jax.ShapeDtypeStruct((bn, ln, CONV_W), F32),
        ]
        out_specs += [
            pl.BlockSpec((None, T, LANES), lambda b, t: (b, t, 0)),
            pl.BlockSpec((None, PART_ROWS, T), lambda b, t: (b, 0, t)),
            pl.BlockSpec((None, T, CONV_W), lambda b, t: (b, t, 0)),
        ]
    kern = functools.partial(_proj_kernel, t_rows=T, n_valid=n_valid, emit_carry=emit_carry, tk=tk)
    return pl.pallas_call(
        kern,
        grid=(bn, nt),
        in_specs=in_specs,
        out_specs=out_specs,
        out_shape=out_shape,
        scratch_shapes=[pltpu.VMEM((8, LANES), F32), pltpu.VMEM((PART_ROWS, LANES), F32),
                        pltpu.VMEM((8, CONV_W), F32)],
        compiler_params=pltpu.CompilerParams(dimension_semantics=("arbitrary", "arbitrary"),
                                             vmem_limit_bytes=VMEM_LIMIT),
        name="proj_meta" if emit_carry else "proj",
    )(xs, g, wrow, wt, bfr, bfc, convw, selk, cin_row, cin_col, uin)


def _attn_kernel(q_ref, k_ref, v_ref, km_ref, vm_ref, g_ref, o_ref, s_s, m_s, acc_s):
    i = pl.program_id(1)
    T = ATT_T
    causal = (lax.broadcasted_iota(jnp.int32, (T, T), 0) <= lax.broadcasted_iota(jnp.int32, (T, T), 1))

    def scores(h, kblk, slot, masked):
        qa = q_ref[h * HEAD_SLOT:(h + 1) * HEAD_SLOT, :]
        s = jnp.dot(kblk, qa, preferred_element_type=F32)
        if masked:
            s = jnp.where(causal, s, NEG_BIG)
        s_s[slot, h] = s

    def consume(h, vblk, slot):
        s = s_s[slot, h]
        m_old = m_s[h, 0:1, :]
        m_new = jnp.maximum(m_old, jnp.max(s, axis=0, keepdims=True))
        alpha = jnp.exp2(m_old - m_new)
        pm = jnp.exp2(s - m_new).astype(BF16)
        acc_s[h] = alpha * acc_s[h] + jnp.dot(vblk, pm, preferred_element_type=F32)
        m_s[h, 0:1, :] = m_new

    def step(k_next, next_slot, masked, v_cur, cur_slot):
        ahead = 2
        for h in range(ahead):
            scores(h, k_next(h), next_slot, masked)
        for h in range(N_HEADS):
            consume(h, v_cur(h), cur_slot)
            if h + ahead < N_HEADS:
                scores(h + ahead, k_next(h + ahead), next_slot, masked)

    def k_of(j):
        row = pl.ds(pl.multiple_of(j * T, T), T)
        return lambda h: k_ref[row, h * HEAD_SLOT:(h + 1) * HEAD_SLOT]

    def v_of(j):
        return lambda h: v_ref[j, h * V_SLOT:(h + 1) * V_SLOT, :]

    k_meta = lambda h: km_ref[:, h * HEAD_SLOT:(h + 1) * HEAD_SLOT]
    v_meta = lambda h: vm_ref[h * V_SLOT:(h + 1) * V_SLOT, :]

    m_s[...] = jnp.full(m_s.shape, NEG_BIG, F32)
    acc_s[...] = jnp.zeros(acc_s.shape, F32)
    for h in range(N_HEADS):
        scores(h, k_meta(h), 0, False)

    def v_at(t):
        return lambda h: jnp.where(t == 0, v_meta(h), v_ref[jnp.maximum(t - 1, 0), h * V_SLOT:(h + 1) * V_SLOT, :])

    def pair(t0):
        step(k_of(t0), 1, False, v_at(t0), 0)
        step(k_of(t0 + 1), 0, False, v_at(t0 + 1), 1)

    def quad_body(p, c):
        pair(4 * p)
        pair(4 * p + 2)
        return c

    lax.fori_loop(0, i // 4, quad_body, 0)

    @pl.when(i % 4 >= 2)
    def _():
        pair(4 * (i // 4))

    @pl.when(i % 2 == 1)
    def _():
        step(k_of(i - 1), 1, False, v_at(i - 1), 0)
        step(k_of(i), 0, True, v_at(i), 1)
        for h in range(N_HEADS):
            consume(h, v_of(i)(h), 0)

    @pl.when(i % 2 == 0)
    def _():
        step(k_of(i), 1, True, v_at(i), 0)
        for h in range(N_HEADS):
            consume(h, v_of(i)(h), 1)

    for p in range(N_HEADS // 2):
        halves = []
        for h in (2 * p, 2 * p + 1):
            o = acc_s[h, 0:HEAD_DIM, :] * (1.0 / acc_s[h, HEAD_DIM:HEAD_DIM + 1, :])
            halves.append(o * lax.rsqrt(jnp.mean(o * o, axis=0, keepdims=True) + EPS))
        pair = jnp.concatenate(halves, axis=0)
        o_ref[:, p * LANES:(p + 1) * LANES] = (pair.T * g_ref[:, p * LANES:(p + 1) * LANES]).astype(o_ref.dtype)


def _attn_call(qaug, kaug, vt, kmeta, vmeta, gain):
    bn, _, ln = qaug.shape
    T = ATT_T
    nq = ln // T
    return pl.pallas_call(
        _attn_kernel,
        grid=(bn, nq),
        in_specs=[
            pl.BlockSpec((None, AUG_W, T), lambda b, i: (b, 0, i)),
            pl.BlockSpec((None, ln, AUG_W), lambda b, i: (b, 0, 0)),
            pl.BlockSpec((None, ln // T, V_AUG, T), lambda b, i: (b, 0, 0, 0)),
            pl.BlockSpec(kmeta.shape, lambda b, i: (0, 0)),
            pl.BlockSpec(vmeta.shape, lambda b, i: (0, 0)),
            pl.BlockSpec(gain.shape, lambda b, i: (0, 0)),
        ],
        out_specs=pl.BlockSpec((None, T, ATTN_W), lambda b, i: (b, i, 0)),
        out_shape=jax.ShapeDtypeStruct((bn, ln, ATTN_W), BF16),
        scratch_shapes=[pltpu.VMEM((2, N_HEADS, T, T), F32),
                        pltpu.VMEM((N_HEADS, 8, T), F32),
                        pltpu.VMEM((N_HEADS, V_SLOT, T), F32)],
        compiler_params=pltpu.CompilerParams(dimension_semantics=("arbitrary", "arbitrary"),
                                             vmem_limit_bytes=VMEM_LIMIT),
        name="attention",
    )(qaug, kaug, vt, kmeta, vmeta, gain)


def _mix_kernel(attn_ref, conv_ref, x_ref, gc_ref, gsum_ref, woa_ref, woc_ref, fg_ref,
                wrh_ref, br_ref, h_ref, z_ref, rinfo_ref, counts_ref, rinfo_t_ref, cnt, logits_s):
    T = MIX_T
    step = pl.program_id(0)

    @pl.when(step == 0)
    def _():
        cnt[...] = jnp.zeros_like(cnt)
        logits_s[...] = jnp.zeros_like(logits_s)

    logits = logits_s[...]
    c = conv_ref[...]
    c2 = c * c
    c2h = c2.astype(BF16)
    c2l = (c2 - c2h.astype(F32)).astype(BF16)
    ss = (jnp.dot(c2h, gsum_ref[...], preferred_element_type=F32)
          + jnp.dot(c2l, gsum_ref[...], preferred_element_type=F32))
    cn = (c * lax.rsqrt(ss * (1.0 / HEAD_DIM) + EPS) * gc_ref[...]).astype(BF16)
    hadd = (jnp.dot(attn_ref[...], woa_ref[...], preferred_element_type=F32)
            + jnp.dot(cn, woc_ref[...], preferred_element_type=F32))
    h = x_ref[...] + hadd
    h_ref[...] = h
    z = h * lax.rsqrt(jnp.mean(h * h, axis=-1, keepdims=True) + EPS) * fg_ref[...]
    z_ref[...] = z.reshape(T, ROW_SUB, LANES)
    zh = z.astype(BF16)
    zl = (z - zh.astype(F32)).astype(BF16)
    both = jnp.dot(zh, wrh_ref[...], preferred_element_type=F32)
    logits_s[...] = (both[:, 0:LANES] + both[:, LANES:2 * LANES]
                     + jnp.dot(zl, wrh_ref[:, 0:LANES], preferred_element_type=F32)) + br_ref[...]

    lane = lax.broadcasted_iota(jnp.int32, (T, LANES), 1).astype(F32)
    big = float(LANES)
    gl = jnp.where(lane < N_GROUPS, logits, -jnp.inf)
    gmax = jnp.max(gl, axis=-1, keepdims=True)
    gsum = jnp.sum(jnp.exp(gl - gmax), axis=-1, keepdims=True)
    g_w = 1.0 / gsum
    g_idx = jnp.min(jnp.where(gl == gmax, lane, big), axis=-1, keepdims=True)
    e_lo = ROUTER_LANE0 + EXPERTS_PER_GROUP * g_idx
    emask = (lane >= e_lo) & (lane < e_lo + EXPERTS_PER_GROUP)
    el = jnp.where(emask, logits, -jnp.inf)
    emax = jnp.max(el, axis=-1, keepdims=True)
    eexp = jnp.exp(el - emax)
    probs = eexp / jnp.sum(eexp, axis=-1, keepdims=True)
    probs = jnp.where(emask, probs, -1.0)
    p1 = jnp.max(probs, axis=-1, keepdims=True)
    i1 = jnp.min(jnp.where(probs == p1, lane, big), axis=-1, keepdims=True)
    probs2 = jnp.where(lane == i1, -1.0, probs)
    p2 = jnp.max(probs2, axis=-1, keepdims=True)
    i2 = jnp.min(jnp.where(probs2 == p2, lane, big), axis=-1, keepdims=True)
    psum = p1 + p2
    w1 = g_w * (p1 / psum)
    w2 = g_w * (p2 / psum)

    hit = ((lane == i1) | (lane == i2)).astype(F32)
    tri_r = lax.broadcasted_iota(jnp.int32, (T, T), 0)
    tri_c = lax.broadcasted_iota(jnp.int32, (T, T), 1)
    strict_l = (tri_c < tri_r).astype(BF16)
    before = jnp.dot(strict_l, hit.astype(BF16), preferred_element_type=F32) + cnt[0:1, :]
    r1 = jnp.sum(jnp.where(lane == i1, before, 0.0), axis=-1, keepdims=True)
    r2 = jnp.sum(jnp.where(lane == i2, before, 0.0), axis=-1, keepdims=True)
    cnt[0:1, :] = cnt[0:1, :] + jnp.sum(hit, axis=0, keepdims=True) * (step > 0).astype(F32)
    counts_ref[...] = cnt[...]

    e1 = i1 - ROUTER_LANE0
    e2 = i2 - ROUTER_LANE0
    rinfo = jnp.where(lane == 0, e1, jnp.where(lane == 1, e2, jnp.where(lane == 2, w1, jnp.where(
        lane == 3, w2, jnp.where(lane == 4, r1, jnp.where(lane == 5, r2, 0.0))))))
    rinfo_ref[...] = rinfo
    rinfo_t_ref[...] = rinfo.T[0:8, :]


def _mix_call(attn, conv, x, gc, gsum, woa, woc, fg, wrh, br):
    n = x.shape[0]
    T = MIX_T
    nt = n // T
    const = lambda i: (0, 0)
    cur = lambda i: jnp.minimum(i, nt - 1)
    prev = lambda i: jnp.maximum(i - 1, 0)
    row = lambda w: pl.BlockSpec((T, w), lambda i: (cur(i), 0))
    return pl.pallas_call(
        _mix_kernel,
        grid=(nt + 1,),
        in_specs=[row(ATTN_W), row(CONV_W), row(D_MODEL)] + [
            pl.BlockSpec(a.shape, const) for a in (gc, gsum, woa, woc, fg, wrh, br)],
        out_specs=[row(D_MODEL), pl.BlockSpec((T, ROW_SUB, LANES), lambda i: (cur(i), 0, 0)),
                   pl.BlockSpec((T, LANES), lambda i: (prev(i), 0)),
                   pl.BlockSpec((8, LANES), const), pl.BlockSpec((8, T), lambda i: (0, prev(i)))],
        out_shape=[jax.ShapeDtypeStruct((n, D_MODEL), F32), jax.ShapeDtypeStruct((n, ROW_SUB, LANES), F32),
                   jax.ShapeDtypeStruct((n, LANES), F32), jax.ShapeDtypeStruct((8, LANES), F32),
                   jax.ShapeDtypeStruct((8, n), F32)],
        scratch_shapes=[pltpu.VMEM((8, LANES), F32), pltpu.VMEM((T, LANES), F32)],
        compiler_params=pltpu.CompilerParams(dimension_semantics=("arbitrary",),
                                             vmem_limit_bytes=VMEM_LIMIT),
        name="mix",
    )(attn, conv, x, gc, gsum, woa, woc, fg, wrh, br)


def _dispatch_kernel(pad_end_ref, cnt_ref, dest_ref, z_ref, xbuf_ref, zero_v, sem, zsem):
    T = DISPATCH_T

    @pl.when(pl.program_id(0) == 0)
    def _():
        zero_v[...] = jnp.zeros_like(zero_v)

        def zero_copy(e):
            first = pl.multiple_of(jnp.maximum(pad_end_ref[e] - EXPERT_BM, 0), EXPERT_BM)
            return pltpu.make_async_copy(zero_v, xbuf_ref.at[pl.ds(first, EXPERT_BM)], zsem)

        for e in range(N_EXPERTS):
            @pl.when(cnt_ref[e] > 0)
            def _():
                zero_copy(e).start()
        for e in range(N_EXPERTS):
            @pl.when(cnt_ref[e] > 0)
            def _():
                zero_copy(e).wait()

        def tail_copy(b):
            return pltpu.make_async_copy(
                zero_v, xbuf_ref.at[pl.ds(pl.multiple_of(b * EXPERT_BM, EXPERT_BM), EXPERT_BM)], zsem)

        n_used = pad_end_ref[N_EXPERTS - 1] // EXPERT_BM
        n_all = xbuf_ref.shape[0] // EXPERT_BM
        lax.fori_loop(n_used, n_all, lambda b, c: (tail_copy(b).start(), c)[1], 0)
        lax.fori_loop(n_used, n_all, lambda b, c: (tail_copy(b).wait(), c)[1], 0)

    def row_copy(r, slot):
        d = dest_ref[0, slot * T + r]
        return pltpu.make_async_copy(z_ref.at[pl.ds(r, 1)], xbuf_ref.at[pl.ds(d, 1)], sem)

    def start(r, c):
        row_copy(r, 0).start(priority=0)
        row_copy(r, 1).start(priority=1)
        return c

    def wait(r, c):
        row_copy(r, 0).wait()
        row_copy(r, 1).wait()
        return c

    lax.fori_loop(0, T, start, 0, unroll=DMA_UNROLL)
    lax.fori_loop(0, T, wait, 0, unroll=DMA_UNROLL)


def _dispatch_call(pad_end, cnt, dest, z, n_rows):
    n = z.shape[0]
    T = DISPATCH_T
    grid_spec = pltpu.PrefetchScalarGridSpec(
        num_scalar_prefetch=2,
        grid=(n // T,),
        in_specs=[
            pl.BlockSpec((None, 1, 2 * T), lambda i, pe, ct: (i, 0, 0), memory_space=pltpu.SMEM),
            pl.BlockSpec((T, ROW_SUB, LANES), lambda i, pe, ct: (i, 0, 0)),
        ],
        out_specs=pl.BlockSpec(memory_space=pl.ANY),
        scratch_shapes=[pltpu.VMEM((EXPERT_BM, ROW_SUB, LANES), F32), pltpu.SemaphoreType.DMA(()),
                        pltpu.SemaphoreType.DMA(())],
    )
    return pl.pallas_call(
        _dispatch_kernel,
        grid_spec=grid_spec,
        out_shape=jax.ShapeDtypeStruct((n_rows, ROW_SUB, LANES), F32),
        compiler_params=pltpu.CompilerParams(dimension_semantics=("arbitrary",), vmem_limit_bytes=VMEM_LIMIT),
        name="dispatch",
    )(pad_end, cnt, dest, z)


def _expert_kernel(blk_e_ref, nvb_ref, first_ref, slot_ref, next_ref, x_ref, wg_hbm, wu_hbm, wd_hbm, y_ref,
                   wg_v, wu_v, wd_v, wgu_s, wd_s, sem):
    blk = pl.program_id(0)
    valid = blk < nvb_ref[0]

    def fetch(e, s):
        return (pltpu.make_async_copy(wg_hbm.at[e], wg_v.at[s], sem.at[s]),
                pltpu.make_async_copy(wu_hbm.at[e], wu_v.at[s], sem.at[s]),
                pltpu.make_async_copy(wd_hbm.at[e], wd_v.at[s], sem.at[s]))

    @pl.when(blk == 0)
    def _():
        for c in fetch(blk_e_ref[0], 0):
            c.start()

    @pl.when(valid & (first_ref[blk] != 0))
    def _():
        s = slot_ref[blk]
        for c in fetch(blk_e_ref[blk], s):
            c.wait()
        wgu_s[:, 0:D_EXPERT] = wg_v[s].astype(BF16)
        wgu_s[:, D_EXPERT:2 * D_EXPERT] = wu_v[s].astype(BF16)
        wd_s[...] = wd_v[s].astype(BF16)

        @pl.when(next_ref[blk] >= 0)
        def _():
            for c in fetch(next_ref[blk], 1 - s):
                c.start()

    @pl.when(valid)
    def _():
        xb = x_ref[...].reshape(EXPERT_BM, D_MODEL).astype(BF16)
        gu = jnp.dot(xb, wgu_s[...], preferred_element_type=F32)
        g = gu[:, 0:D_EXPERT]
        u = gu[:, D_EXPERT:2 * D_EXPERT]
        a = (g * jax.nn.sigmoid(g)) * u
        y = jnp.dot(a.astype(BF16), wd_s[...], preferred_element_type=F32)
        y_ref[...] = y.reshape(EXPERT_BM, ROW_SUB, LANES)


def _expert_call(blk_e, nvb, blk_first, blk_slot, blk_next, xbuf, w_gate, w_up, w_down):
    rows = xbuf.shape[0]
    bm = EXPERT_BM
    used = lambda i, be, nv, bf, bs, bn: (jnp.minimum(i, nv[0] - 1), 0, 0)
    grid_spec = pltpu.PrefetchScalarGridSpec(
        num_scalar_prefetch=5,
        grid=(rows // bm,),
        in_specs=[
            pl.BlockSpec((bm, ROW_SUB, LANES), used),
            pl.BlockSpec(memory_space=pl.ANY),
            pl.BlockSpec(memory_space=pl.ANY),
            pl.BlockSpec(memory_space=pl.ANY),
        ],
        out_specs=pl.BlockSpec((bm, ROW_SUB, LANES), used),
        scratch_shapes=[pltpu.VMEM((2, D_MODEL, D_EXPERT), F32), pltpu.VMEM((2, D_MODEL, D_EXPERT), F32),
                        pltpu.VMEM((2, D_EXPERT, D_MODEL), F32),
                        pltpu.VMEM((D_MODEL, 2 * D_EXPERT), BF16), pltpu.VMEM((D_EXPERT, D_MODEL), BF16),
                        pltpu.SemaphoreType.DMA((2,))],
    )
    return pl.pallas_call(
        _expert_kernel,
        grid_spec=grid_spec,
        out_shape=jax.ShapeDtypeStruct((rows, ROW_SUB, LANES), F32),
        input_output_aliases={5: 0},
        compiler_params=pltpu.CompilerParams(dimension_semantics=("arbitrary",),
                                             vmem_limit_bytes=VMEM_LIMIT),
        name="experts",
    )(blk_e, nvb, blk_first, blk_slot, blk_next, xbuf, w_gate, w_up, w_down)


def _combine_kernel(dest_ref, dest_next_ref, h_ref, rinfo_ref, fn_ref, ybuf_ref, o_ref, ybuf_v, sem):
    T = COMBINE_T
    i = pl.program_id(0)
    cur = i % 2

    def row_copy(dref, buf, r, k):
        d = dref[0, k * T + r]
        return pltpu.make_async_copy(ybuf_ref.at[pl.ds(d, 1)], ybuf_v.at[buf, k, pl.ds(r, 1)], sem.at[buf])

    def issue(dref, buf):
        def start(r, c):
            row_copy(dref, buf, r, 0).start(priority=0)
            row_copy(dref, buf, r, 1).start(priority=1)
            return c
        lax.fori_loop(0, T, start, 0, unroll=DMA_UNROLL)

    @pl.when(i == 0)
    def _():
        issue(dest_ref, 0)

    @pl.when(i + 1 < pl.num_programs(0))
    def _():
        issue(dest_next_ref, 1 - cur)

    def wait(r, c):
        row_copy(dest_ref, cur, r, 0).wait()
        row_copy(dest_ref, cur, r, 1).wait()
        return c

    lax.fori_loop(0, T, wait, 0, unroll=DMA_UNROLL)
    rinfo = rinfo_ref[...]
    w1 = rinfo[:, 2:3]
    w2 = rinfo[:, 3:4]
    y1 = ybuf_v[cur, 0].reshape(T, D_MODEL)
    y2 = ybuf_v[cur, 1].reshape(T, D_MODEL)
    h = h_ref[...] + (y1 * w1 + y2 * w2)
    o_ref[...] = h * lax.rsqrt(jnp.mean(h * h, axis=-1, keepdims=True) + EPS) * fn_ref[...]


def _combine_call(dest, h, rinfo, fn, ybuf):
    n = h.shape[0]
    T = COMBINE_T
    nt = n // T
    return pl.pallas_call(
        _combine_kernel,
        grid=(nt,),
        in_specs=[
            pl.BlockSpec((None, 1, 2 * T), lambda i: (i, 0, 0), memory_space=pltpu.SMEM),
            pl.BlockSpec((None, 1, 2 * T), lambda i: (jnp.minimum(i + 1, nt - 1), 0, 0), memory_space=pltpu.SMEM),
            pl.BlockSpec((T, D_MODEL), lambda i: (i, 0)),
            pl.BlockSpec((T, LANES), lambda i: (i, 0)),
            pl.BlockSpec(fn.shape, lambda i: (0, 0)),
            pl.BlockSpec(memory_space=pl.ANY),
        ],
        out_specs=pl.BlockSpec((T, D_MODEL), lambda i: (i, 0)),
        out_shape=jax.ShapeDtypeStruct((n, D_MODEL), F32),
        scratch_shapes=[pltpu.VMEM((2, 2, T, ROW_SUB, LANES), F32), pltpu.SemaphoreType.DMA((2,))],
        compiler_params=pltpu.CompilerParams(dimension_semantics=("arbitrary",),
                                             vmem_limit_bytes=VMEM_LIMIT),
        name="combine",
    )(dest, dest, h, rinfo, fn, ybuf)


def kernel(x, meta_tokens, attn_norm, w_in, b_forget, conv_w, mix_norm, w_out, ffn_norm,
           w_router_group, b_router_group, w_router_expert, b_router_expert,
           w_gate, w_up, w_down, final_norm):
    bsz, seq, d = x.shape
    assert d == D_MODEL and seq % PROJ_T == 0 and meta_tokens.shape[0] == N_META
    assert attn_norm.shape[0] == 1, "single-layer block"
    n_tok = bsz * seq

    wi = w_in[0]
    o = 0
    w_q = wi[:, o:o + ATTN_W]; o += ATTN_W
    w_k = wi[:, o:o + ATTN_W]; o += ATTN_W
    w_v = wi[:, o:o + ATTN_W]; o += ATTN_W
    w_f = wi[:, o:o + N_HEADS]; o += N_HEADS
    w_xc = wi[:, o:o + CONV_W]; o += CONV_W
    w_bg = wi[:, o:o + CONV_W]; o += CONV_W
    w_cg = wi[:, o:o + CONV_W]
    wrow = jnp.concatenate([w_k, w_xc, w_bg, w_cg, w_f, jnp.zeros((D_MODEL, LANES - N_HEADS), F32)],
                           axis=1).astype(BF16)
    wt = jnp.concatenate([w_q.T * (HEAD_DIM ** -0.5), w_v.T, w_f.T,
                          jnp.zeros((PART_ROWS - N_HEADS, D_MODEL), F32)], axis=0).astype(BF16)
    g_attn = attn_norm[0].reshape(1, D_MODEL)
    bfr = jnp.zeros((1, LANES), F32).at[0, :N_HEADS].set(b_forget[0])
    bfc = jnp.zeros((PART_ROWS, LANES), F32).at[:N_HEADS, :].set(b_forget[0][:, None])
    convw = jnp.zeros((8, CONV_W), F32).at[:3].set(conv_w[0])
    selk = _selection_matrix()

    meta_pad = jnp.zeros((1, META_PAD, D_MODEL), F32).at[0, :N_META].set(meta_tokens)
    zrow = jnp.zeros((8, LANES), F32)
    zcol = jnp.zeros((PART_ROWS, LANES), F32)
    zu = jnp.zeros((8, CONV_W), F32)
    kmeta, _, vmeta, _, cumrow_m, cumt_m, u_m = _proj_call(
        meta_pad, g_attn, wrow, wt, bfr, bfc, convw, selk, zrow, zcol, zu,
        t_rows=META_PAD, n_valid=N_META, emit_carry=True)
    cin_row = jnp.zeros((8, LANES), F32).at[0].set(cumrow_m[0, N_META - 1])
    cin_col = jnp.broadcast_to(cumt_m[0, :, N_META - 1:N_META], (PART_ROWS, LANES))
    uin = u_m[0, N_META - 8:N_META]

    kaug, qaug, vt, conv = _proj_call(
        x, g_attn, wrow, wt, bfr, bfc, convw, selk, cin_row, cin_col, uin,
        t_rows=PROJ_T, n_valid=PROJ_T, emit_carry=False)

    mixg = mix_norm[0]
    attn = _attn_call(qaug, kaug, vt, kmeta[0], vmeta[0, 0], mixg[:ATTN_W].reshape(1, ATTN_W))

    gidx = np.arange(CONV_W) // HEAD_DIM
    gsum = jnp.asarray(gidx[:, None] == gidx[None, :], BF16)
    wo = w_out[0].astype(BF16)
    wr = jnp.zeros((D_MODEL, LANES), F32)
    wr = wr.at[:, :N_GROUPS].set(w_router_group[0]).at[:, ROUTER_LANE0:ROUTER_LANE0 + N_EXPERTS].set(
        w_router_expert[0])
    wr_hi = wr.astype(BF16)
    wrh = jnp.concatenate([wr_hi, (wr - wr_hi.astype(F32)).astype(BF16)], axis=1)
    br = jnp.zeros((1, LANES), F32)
    br = br.at[0, :N_GROUPS].set(b_router_group[0]).at[0, ROUTER_LANE0:ROUTER_LANE0 + N_EXPERTS].set(
        b_router_expert[0])
    h, z, rinfo, counts, rinfo_t = _mix_call(
        attn.reshape(n_tok, ATTN_W), conv.reshape(n_tok, CONV_W), x.reshape(n_tok, D_MODEL),
        mixg[ATTN_W:].reshape(1, CONV_W), gsum, wo[:ATTN_W], wo[ATTN_W:], ffn_norm[0].reshape(1, D_MODEL),
        wrh, br)

    cnt = counts[0, ROUTER_LANE0:ROUTER_LANE0 + N_EXPERTS].astype(jnp.int32)
    padded = (cnt + EXPERT_BM - 1) // EXPERT_BM * EXPERT_BM
    pad_end = jnp.cumsum(padded)
    pad_start = pad_end - padded
    eid = rinfo_t[0:2].astype(jnp.int32)
    rank = rinfo_t[4:6].astype(jnp.int32)
    expert_ids = jnp.arange(N_EXPERTS, dtype=jnp.int32)
    dest = rank + jnp.sum(jnp.where(eid[None, :, :] == expert_ids[:, None, None], pad_start[:, None, None], 0),
                          axis=0)
    dest = dest.reshape(2, n_tok // DISPATCH_T, DISPATCH_T).transpose(1, 0, 2).reshape(
        n_tok // DISPATCH_T, 1, 2 * DISPATCH_T)
    n_blk = (2 * n_tok) // EXPERT_BM + N_EXPERTS
    nvb = (pad_end[-1] // EXPERT_BM).reshape(1)
    blk_row = jnp.minimum(jnp.arange(n_blk, dtype=jnp.int32), nvb - 1) * EXPERT_BM
    blk_e = jnp.sum(blk_row[:, None] >= pad_end[None, :], axis=1).astype(jnp.int32)

    xbuf = _dispatch_call(pad_end, cnt, dest, z, n_blk * EXPERT_BM)
    nonempty = cnt > 0
    ordinal = jnp.cumsum(nonempty.astype(jnp.int32)) - 1
    later = (expert_ids[None, :] > expert_ids[:, None]) & nonempty[None, :]
    next_ne = jnp.min(jnp.where(later, expert_ids[None, :], N_EXPERTS), axis=1)
    next_ne = jnp.where(next_ne == N_EXPERTS, -1, next_ne).astype(jnp.int32)
    blk_idx = jnp.arange(n_blk, dtype=jnp.int32)
    blk_first = ((blk_idx * EXPERT_BM == pad_start[blk_e]) & (blk_idx < nvb)).astype(jnp.int32)
    blk_slot = (ordinal[blk_e] % 2).astype(jnp.int32)
    blk_next = next_ne[blk_e]
    ybuf = _expert_call(blk_e, nvb, blk_first, blk_slot, blk_next, xbuf, w_gate[0], w_up[0], w_down[0])
    out = _combine_call(dest, h, rinfo, final_norm.reshape(1, D_MODEL), ybuf)
    return out.reshape(bsz, seq, D_MODEL)
```
